```python
import jax
import jax.numpy as jnp
from jax import lax
import numpy as np

D_MODEL = 1024
BATCH = 16
SEQ = 2048
DEPTH = 1

GRID_W = 64
CTX_LEN = 256
HY_WIDTH = 512
HY_BANDS = 16
HY_EMB = 1 + 2 * HY_BANDS
HY_FILT_HIDDEN = 64
HY_DECAY_MIN = 2.0
HY_DECAY_MAX = 20.0
GLA_HEADS = 4
GLA_DK = 64
GLA_DV = 128
GLA_LOWRANK = 16
GLA_TAU = 16.0
GLA_CHUNK = 64
FFN_HIDDEN = 2816
EPS = 1e-6

HY_COLS = 3 * HY_WIDTH
GLA_QK = GLA_HEADS * GLA_DK
GLA_VW = GLA_HEADS * GLA_DV
IN_SIZES = (HY_COLS, GLA_QK, GLA_QK, GLA_VW, GLA_VW, GLA_LOWRANK, GLA_LOWRANK, D_MODEL, D_MODEL)
IN_COLS = HY_COLS + 2 * GLA_QK + 2 * GLA_VW + 2 * GLA_LOWRANK + 2 * D_MODEL

kernel_name = "hyena_gla_convffn_prefix_block"


def rmsnorm(x, w):
    xf = x.astype(jnp.float32)
    y = xf * lax.rsqrt(jnp.mean(xf * xf, axis=-1, keepdims=True) + EPS)
    return (y * w.astype(jnp.float32)).astype(x.dtype)


def adaln_params(cond, w_ada, b_ada):
    mod = jax.nn.silu(cond) @ w_ada + b_ada
    return jnp.split(mod, 6, axis=-1)


def split_in(proj):
    idx = np.cumsum(np.array(IN_SIZES))[:-1].tolist()
    return jnp.split(proj, idx, axis=-1)


def flip_t(t):
    return jnp.flip(t, axis=1)


def short_conv(u, w):
    L = u.shape[1]
    up = jnp.pad(u, ((0, 0), (1, 1), (0, 0)))
    return up[:, :L] * w[0] + up[:, 1:L + 1] * w[1] + up[:, 2:] * w[2]


def hyena_filter(L, p):
    f32 = jnp.float32
    t = jnp.linspace(0.0, 1.0, L, dtype=f32)[:, None]
    w = (2.0 * np.pi / L) * jnp.arange(L, dtype=f32)[:, None]
    f = jnp.linspace(1e-4, HY_BANDS - 1, HY_BANDS, dtype=f32)[None, :]
    feats = jnp.concatenate([t, jnp.cos(f * w), -jnp.sin(f * w)], axis=-1)
    freq = p["hy_filt_freq"].astype(f32)
    hdn = jnp.sin(freq * (feats @ p["hy_filt_w1"].astype(f32) + p["hy_filt_b1"].astype(f32)))
    hdn = jnp.sin(freq * (hdn @ p["hy_filt_w2"].astype(f32) + p["hy_filt_b2"].astype(f32)))
    k = (hdn @ p["hy_filt_w3"].astype(f32)) * jnp.exp(-t * p["hy_decay"].astype(f32))
    k_fwd = k[:, :HY_WIDTH]
    k_bwd = jnp.flip(k[1:, HY_WIDTH:], axis=0)
    norm = jnp.sum(jnp.abs(k_fwd), axis=0) + jnp.sum(jnp.abs(k_bwd), axis=0)
    kbuf = jnp.concatenate([k_fwd, jnp.zeros((1, HY_WIDTH), f32), k_bwd], axis=0)
    return kbuf / norm


def hyena_branch(u, p):
    L = u.shape[1]
    u = short_conv(u, p["hy_conv_w"])
    x0, x1, v = jnp.split(u, 3, axis=-1)
    z = (x1 * v).astype(jnp.float32)
    kbuf = hyena_filter(L, p)
    zf = jnp.fft.rfft(z, n=2 * L, axis=1)
    kf = jnp.fft.rfft(kbuf, n=2 * L, axis=0)
    y = jnp.fft.irfft(zf * kf[None], n=2 * L, axis=1)[:, :L]
    y = y + p["hy_bias"].astype(jnp.float32) * z
    return (x0.astype(jnp.float32) * y).astype(u.dtype)


def heads(t, d):
    return t.astype(jnp.float32).reshape(t.shape[0], t.shape[1], GLA_HEADS, d)


def log_decay(a, w_up, bias):
    la = jax.nn.log_sigmoid(a.astype(jnp.float32) @ w_up.astype(jnp.float32) + bias.astype(jnp.float32))
    return heads(la, GLA_DK) / GLA_TAU


def gla_chunked(q, k, v, log_a, s0):
    B, L, H, _ = q.shape
    N = L // GLA_CHUNK
    rs = lambda t: t.reshape(B, N, GLA_CHUNK, H, t.shape[-1])
    q, k, v, log_a = rs(q), rs(k), rs(v), rs(log_a)
    b = jnp.cumsum(log_a, axis=2)
    b_last = b[:, :, -1:]
    qe = q * jnp.exp(b)
    ke = k * jnp.exp(-b)
    kd = k * jnp.exp(b_last - b)
    mask = jnp.tril(jnp.ones((GLA_CHUNK, GLA_CHUNK), dtype=bool))
    scores = jnp.where(mask, jnp.einsum("bnchd,bnshd->bnhcs", qe, ke), 0.0)
    intra = jnp.einsum("bnhcs,bnshv->bnchv", scores, v)
    upd = jnp.einsum("bnshd,bnshv->bnhdv", kd, v)
    g = jnp.exp(b_last[:, :, 0])

    def step(s, xs):
        gn, un = xs
        return gn[..., None] * s + un, s

    _, s_prev = lax.scan(step, s0, (jnp.moveaxis(g, 1, 0), jnp.moveaxis(upd, 1, 0)))
    s_prev = jnp.moveaxis(s_prev, 0, 1)
    inter = jnp.einsum("bnchd,bnhdv->bnchv", qe, s_prev)
    return (intra + inter).reshape(B, L, H, v.shape[-1])


def gla_final_state(k, v, log_a):
    b = jnp.cumsum(log_a, axis=1)
    return jnp.einsum("blhd,blhv->bhdv", k * jnp.exp(b[:, -1:] - b), v)


def mixer(proj, p, s_f0, s_b0):
    u_hy, q, k, v, r, a_f, a_b, g_hy, g_gla = split_in(proj)
    y_hy = hyena_branch(u_hy, p)
    q = heads(q, GLA_DK) * (GLA_DK ** -0.5)
    k = heads(k, GLA_DK)
    v = heads(v, GLA_DV)
    la_f = log_decay(a_f, p["gla_a_up_f"], p["gla_a_bias_f"])
    la_b = log_decay(a_b, p["gla_a_up_b"], p["gla_a_bias_b"])
    o_f = gla_chunked(q, k, v, la_f, s_f0)
    o_b = flip_t(gla_chunked(flip_t(q), flip_t(k), flip_t(v), flip_t(la_b), s_b0))
    o = o_f + o_b
    o = o * lax.rsqrt(jnp.mean(o * o, axis=-1, keepdims=True) + EPS)
    B, L = o.shape[0], o.shape[1]
    y_gla = (o.reshape(B, L, GLA_VW) * p["gla_norm_w"].astype(jnp.float32)
             * jax.nn.silu(r.astype(jnp.float32))).astype(proj.dtype)
    merged = (jax.nn.sigmoid(g_hy) * (y_hy @ p["w_hy_proj"])
              + jax.nn.sigmoid(g_gla) * (y_gla @ p["w_gla_proj"]))
    return merged @ p["w_out"]


def conv_ffn(h, p, grid_w):
    B, L, _ = h.shape
    rows = L // grid_w
    up, gate = jnp.split(h @ p["ffn_w_in"], 2, axis=-1)
    g = gate.reshape(B, rows, grid_w, FFN_HIDDEN)
    g = lax.conv_general_dilated(g, p["ffn_dw"][:, :, None, :], window_strides=(1, 1), padding="SAME",
                                 dimension_numbers=("NHWC", "HWIO", "NHWC"),
                                 feature_group_count=FFN_HIDDEN)
    g = g.reshape(B, L, FFN_HIDDEN) + p["ffn_dw_bias"]
    return (jax.nn.silu(g) * up) @ p["ffn_w_out"]


def layer(x, ctx, c, c_ctx, p, update_ctx):
    sh1, sc1, g1, sh2, sc2, g2 = adaln_params(c[:, None, :], p["w_ada"], p["b_ada"])
    csh1, csc1, cg1, csh2, csc2, cg2 = adaln_params(c_ctx, p["w_ada"], p["b_ada"])
    hc = rmsnorm(ctx, p["norm1_w"]) * (1.0 + csc1) + csh1
    pc = hc @ p["w_in"]
    _, _, kc, vc, _, ac_f, ac_b, _, _ = split_in(pc)
    kc = heads(kc, GLA_DK)
    vc = heads(vc, GLA_DV)
    s_f = gla_final_state(kc, vc, log_decay(ac_f, p["gla_a_up_f"], p["gla_a_bias_f"]))
    s_b = gla_final_state(flip_t(kc), flip_t(vc),
                          flip_t(log_decay(ac_b, p["gla_a_up_b"], p["gla_a_bias_b"])))
    hx = rmsnorm(x, p["norm1_w"]) * (1.0 + sc1) + sh1
    x_new = x + g1 * mixer(hx @ p["w_in"], p, s_f, s_b)
    x_new = x_new + g2 * conv_ffn(rmsnorm(x_new, p["norm2_w"]) * (1.0 + sc2) + sh2, p, GRID_W)
    if update_ctx:
        z0 = jnp.zeros_like(s_f)
        ctx = ctx + cg1 * mixer(pc, p, z0, z0)
        ctx = ctx + cg2 * conv_ffn(rmsnorm(ctx, p["norm2_w"]) * (1.0 + csc2) + csh2, p, ctx.shape[1])
    return x_new, ctx


def setup_inputs(seed: int = 0) -> dict:
    key = jax.random.key(seed)
    ks = jax.random.split(key, 32)
    f32 = jnp.float32
    nrm = lambda k, shape, s: jax.random.normal(k, shape, f32) * s
    L_ = DEPTH
    return {
        "x": nrm(ks[0], (BATCH, SEQ, D_MODEL), 1.0),
        "c": nrm(ks[1], (BATCH, D_MODEL), 1.0),
        "ctx": nrm(ks[2], (BATCH, CTX_LEN, D_MODEL), 1.0),
        "c_ctx": nrm(ks[3], (D_MODEL,), 1.0),
        "w_ada": nrm(ks[4], (L_, D_MODEL, 6 * D_MODEL), 0.5 * D_MODEL ** -0.5),
        "b_ada": nrm(ks[5], (L_, 6 * D_MODEL), 0.02),
        "norm1_w": 1.0 + nrm(ks[6], (L_, D_MODEL), 0.02),
        "norm2_w": 1.0 + nrm(ks[7], (L_, D_MODEL), 0.02),
        "w_in": nrm(ks[8], (L_, D_MODEL, IN_COLS), D_MODEL ** -0.5),
        "hy_conv_w": nrm(ks[9], (L_, 3, HY_COLS), 3 ** -0.5),
        "hy_filt_w1": nrm(ks[10], (L_, HY_EMB, HY_FILT_HIDDEN), HY_EMB ** -0.5),
        "hy_filt_b1": nrm(ks[11], (L_, HY_FILT_HIDDEN), 0.1),
        "hy_filt_w2": nrm(ks[12], (L_, HY_FILT_HIDDEN, HY_FILT_HIDDEN), HY_FILT_HIDDEN ** -0.5),
        "hy_filt_b2": nrm(ks[13], (L_, HY_FILT_HIDDEN), 0.1),
        "hy_filt_w3": nrm(ks[14], (L_, HY_FILT_HIDDEN, 2 * HY_WIDTH), HY_FILT_HIDDEN ** -0.5),
        "hy_filt_freq": 1.0 + nrm(ks[15], (L_, HY_FILT_HIDDEN), 0.1),
        "hy_decay": jax.random.uniform(ks[16], (L_, 2 * HY_WIDTH), f32, HY_DECAY_MIN, HY_DECAY_MAX),
        "hy_bias": nrm(ks[17], (L_, HY_WIDTH), 0.5),
        "gla_a_up_f": nrm(ks[18], (L_, GLA_LOWRANK, GLA_QK), GLA_LOWRANK ** -0.5),
        "gla_a_bias_f": nrm(ks[19], (L_, GLA_QK), 0.5),
        "gla_a_up_b": nrm(ks[20], (L_, GLA_LOWRANK, GLA_QK), GLA_LOWRANK ** -0.5),
        "gla_a_bias_b": nrm(ks[21], (L_, GLA_QK), 0.5),
        "gla_norm_w": 1.0 + nrm(ks[22], (L_, GLA_VW), 0.02),
        "w_hy_proj": nrm(ks[23], (L_, HY_WIDTH, D_MODEL), HY_WIDTH ** -0.5),
        "w_gla_proj": nrm(ks[24], (L_, GLA_VW, D_MODEL), GLA_VW ** -0.5),
        "w_out": nrm(ks[25], (L_, D_MODEL, D_MODEL), D_MODEL ** -0.5),
        "ffn_w_in": nrm(ks[26], (L_, D_MODEL, 2 * FFN_HIDDEN), D_MODEL ** -0.5),
        "ffn_dw": nrm(ks[27], (L_, 3, 3, FFN_HIDDEN), 1.0 / 3.0),
        "ffn_dw_bias": nrm(ks[28], (L_, FFN_HIDDEN), 0.02),
        "ffn_w_out": nrm(ks[29], (L_, FFN_HIDDEN, D_MODEL), FFN_HIDDEN ** -0.5),
        "final_norm_w": 1.0 + nrm(ks[30], (D_MODEL,), 0.02),
    }


def reference(x, c, ctx, c_ctx, w_ada, b_ada, norm1_w, norm2_w, w_in, hy_conv_w,
              hy_filt_w1, hy_filt_b1, hy_filt_w2, hy_filt_b2, hy_filt_w3, hy_filt_freq,
              hy_decay, hy_bias, gla_a_up_f, gla_a_bias_f, gla_a_up_b, gla_a_bias_b,
              gla_norm_w, w_hy_proj, w_gla_proj, w_out, ffn_w_in, ffn_dw, ffn_dw_bias,
              ffn_w_out, final_norm_w):
    for l in range(DEPTH):
        p = {
            "w_ada": w_ada[l], "b_ada": b_ada[l], "norm1_w": norm1_w[l], "norm2_w": norm2_w[l],
            "w_in": w_in[l], "hy_conv_w": hy_conv_w[l],
            "hy_filt_w1": hy_filt_w1[l], "hy_filt_b1": hy_filt_b1[l],
            "hy_filt_w2": hy_filt_w2[l], "hy_filt_b2": hy_filt_b2[l],
            "hy_filt_w3": hy_filt_w3[l], "hy_filt_freq": hy_filt_freq[l],
            "hy_decay": hy_decay[l], "hy_bias": hy_bias[l],
            "gla_a_up_f": gla_a_up_f[l], "gla_a_bias_f": gla_a_bias_f[l],
            "gla_a_up_b": gla_a_up_b[l], "gla_a_bias_b": gla_a_bias_b[l],
            "gla_norm_w": gla_norm_w[l], "w_hy_proj": w_hy_proj[l], "w_gla_proj": w_gla_proj[l],
            "w_out": w_out[l], "ffn_w_in": ffn_w_in[l], "ffn_dw": ffn_dw[l],
            "ffn_dw_bias": ffn_dw_bias[l], "ffn_w_out": ffn_w_out[l],
        }
        x, ctx = layer(x, ctx, c, c_ctx, p, l < DEPTH - 1)
    return rmsnorm(x, final_norm_w)
```

```python
import functools
import math

import jax
import jax.numpy as jnp
import numpy as np
from jax import lax
from jax.experimental import pallas as pl
from jax.experimental.pallas import tpu as pltpu

F32 = jnp.float32
BF16 = jnp.bfloat16
HIGHEST = lax.Precision.HIGHEST

D_MODEL = 1024
GRID_W = 64
HY_WIDTH = 512
HY_BANDS = 16
HY_EMB = 1 + 2 * HY_BANDS
HY_FILT_HIDDEN = 64
GLA_HEADS = 4
GLA_DK = 64
GLA_DV = 128
GLA_LOWRANK = 16
GLA_TAU = 16.0
GLA_CHUNK = 64
FFN_HIDDEN = 2816
EPS = 1e-6
HY_COLS = 3 * HY_WIDTH
GLA_QK = GLA_HEADS * GLA_DK
GLA_VW = GLA_HEADS * GLA_DV

LANES = 128
PAIR = 2 * GLA_CHUNK
VMEM_LIMIT = 56 * 1024 * 1024


def _cp(sem):
    return pltpu.CompilerParams(dimension_semantics=sem, vmem_limit_bytes=VMEM_LIMIT)


def _dot(a, b):
    return jnp.dot(a, b, preferred_element_type=F32)


def _dot_hi(a, b):
    return jnp.dot(a, b, preferred_element_type=F32, precision=HIGHEST)


def _sigmoid(x):
    return 1.0 / (1.0 + jnp.exp(-x))


def _silu(x):
    return x * _sigmoid(x)


def _log_sigmoid(x):
    return jnp.minimum(x, 0.0) - jnp.log(1.0 + jnp.exp(-jnp.abs(x)))


def _rms_scale(xf):
    return xf * lax.rsqrt(jnp.mean(xf * xf, axis=-1, keepdims=True) + EPS)


def _ada_kernel(c_ref, w_ref, b_ref, o_ref):
    o_ref[...] = _dot_hi(_silu(c_ref[...]), w_ref[...]) + b_ref[...]


def ada_call(c_rows, w_ada, b_ada):
    rows, d = c_rows.shape
    n = w_ada.shape[1]
    bn = 512
    return pl.pallas_call(
        _ada_kernel,
        grid=(n // bn,),
        in_specs=[pl.BlockSpec((rows, d), lambda j: (0, 0)),
                  pl.BlockSpec((d, bn), lambda j: (0, j)),
                  pl.BlockSpec((1, bn), lambda j: (0, j))],
        out_specs=pl.BlockSpec((rows, bn), lambda j: (0, j)),
        out_shape=jax.ShapeDtypeStruct((rows, n), F32),
        compiler_params=_cp(("arbitrary",)),
        name="ada",
    )(c_rows, w_ada, b_ada.reshape(1, n))


def _filt_mlp_kernel(feats_ref, t_ref, w1_ref, b1_ref, w2_ref, b2_ref, w3_ref, freq_ref, decay_ref,
                     ksum_ref, kdiff_ref, knyq_ref):
    freq = freq_ref[...]
    h = jnp.sin(freq * (_dot_hi(feats_ref[...], w1_ref[...]) + b1_ref[...]))
    h = jnp.sin(freq * (_dot_hi(h, w2_ref[...]) + b2_ref[...]))
    k = _dot_hi(h, w3_ref[...]) * jnp.exp(-t_ref[...] * decay_ref[...])
    L = k.shape[0]
    row = lax.broadcasted_iota(jnp.int32, (L, HY_WIDTH), 0)
    k_f = k[:, :HY_WIDTH]
    k_b = jnp.where(row == 0, 0.0, k[:, HY_WIDTH:])
    norm = jnp.sum(jnp.abs(k_f), axis=0, keepdims=True) + jnp.sum(jnp.abs(k_b), axis=0, keepdims=True)
    k_f = k_f / norm
    k_b = k_b / norm
    ksum = k_f + k_b
    ksum_ref[...] = ksum
    kdiff_ref[...] = k_b - k_f
    sign = jnp.where((row & 1) == 0, 1.0, -1.0)
    knyq_ref[...] = jnp.sum(sign * ksum, axis=0, keepdims=True) * (1.0 / (2 * L))


def _filt_dft_kernel(c_ref, s_ref, ksum_ref, kdiff_ref, a_ref, b_ref, *, n_fft, fc):
    j = pl.program_id(0)
    f = lax.broadcasted_iota(jnp.int32, (fc, 1), 0) + j * fc
    w = jnp.where(f == 0, 1.0 / n_fft, 2.0 / n_fft)
    a_ref[...] = w * _dot_hi(c_ref[...], ksum_ref[...])
    b_ref[...] = w * _dot_hi(s_ref[...], kdiff_ref[...])


def filter_call(L, hy_filt_w1, hy_filt_b1, hy_filt_w2, hy_filt_b2, hy_filt_w3, hy_filt_freq, hy_decay,
                cos32, sin32):
    t = jnp.linspace(0.0, 1.0, L, dtype=F32)[:, None]
    w = (2.0 * np.pi / L) * jnp.arange(L, dtype=F32)[:, None]
    f = jnp.linspace(1e-4, HY_BANDS - 1, HY_BANDS, dtype=F32)[None, :]
    feats = jnp.concatenate([t, jnp.cos(f * w), -jnp.sin(f * w)], axis=-1)
    feats = jnp.pad(feats, ((0, 0), (0, LANES - HY_EMB)))
    w1 = jnp.pad(hy_filt_w1, ((0, LANES - HY_EMB), (0, 0)))
    hid = HY_FILT_HIDDEN
    ksum, kdiff, knyq = pl.pallas_call(
        _filt_mlp_kernel,
        out_shape=(jax.ShapeDtypeStruct((L, HY_WIDTH), F32),
                   jax.ShapeDtypeStruct((L, HY_WIDTH), F32),
                   jax.ShapeDtypeStruct((1, HY_WIDTH), F32)),
        compiler_params=pltpu.CompilerParams(vmem_limit_bytes=VMEM_LIMIT),
        name="filt_mlp",
    )(feats, t, w1, hy_filt_b1.reshape(1, hid), hy_filt_w2, hy_filt_b2.reshape(1, hid), hy_filt_w3,
      hy_filt_freq.reshape(1, hid), hy_decay.reshape(1, 2 * HY_WIDTH))
    fc = 256
    a_co, b_co = pl.pallas_call(
        functools.partial(_filt_dft_kernel, n_fft=2 * L, fc=fc),
        grid=(L // fc,),
        in_specs=[pl.BlockSpec((fc, L), lambda j: (j, 0)),
                  pl.BlockSpec((fc, L), lambda j: (j, 0)),
                  pl.BlockSpec((L, HY_WIDTH), lambda j: (0, 0)),
                  pl.BlockSpec((L, HY_WIDTH), lambda j: (0, 0))],
        out_specs=(pl.BlockSpec((fc, HY_WIDTH), lambda j: (j, 0)),
                   pl.BlockSpec((fc, HY_WIDTH), lambda j: (j, 0))),
        out_shape=(jax.ShapeDtypeStruct((L, HY_WIDTH), F32),
                   jax.ShapeDtypeStruct((L, HY_WIDTH), F32)),
        compiler_params=_cp(("arbitrary",)),
        name="filt_dft",
    )(cos32, sin32, ksum, kdiff)
    return a_co, b_co, knyq


def dft_tables(L):
    n_fft = 2 * L
    idx = (jnp.arange(L, dtype=jnp.int32)[:, None] * jnp.arange(L, dtype=jnp.int32)[None, :]) % n_fft
    ang = idx.astype(F32) * (2.0 * np.pi / n_fft)
    return jnp.cos(ang), jnp.sin(ang)


def _modulated_norm(x_ref, nw_ref, sc_ref, sh_ref):
    y = _rms_scale(x_ref[0].astype(F32)) * nw_ref[...]
    return (y * (1.0 + sc_ref[0]) + sh_ref[0]).astype(BF16)


def _inproj_kernel(x_ref, nw_ref, sc_ref, sh_ref, w_ref, wkt_ref, wa_ref, wat_ref,
                   hy_ref, qvr_ref, gate_ref, kt_ref, a_ref, at_ref, *, tm):
    h = _modulated_norm(x_ref, nw_ref, sc_ref, sh_ref)
    col = 0
    for o_ref in (hy_ref, qvr_ref, gate_ref):
        width = o_ref.shape[-1]
        done = 0
        while done < width:
            step = min(512, width - done)
            o_ref[0, :, done:done + step] = _dot(h, w_ref[:, col:col + step]).astype(o_ref.dtype)
            done += step
            col += step
    nt = (((1,), (1,)), ((), ()))
    kt = lax.dot_general(wkt_ref[...], h, nt, preferred_element_type=F32).astype(BF16)
    for p in range(tm // PAIR):
        kt_ref[0, p] = kt[:, p * PAIR:(p + 1) * PAIR]
    a_ref[0] = _dot(h, wa_ref[...])
    at_ref[0] = lax.dot_general(wat_ref[...], h, nt, preferred_element_type=F32)


def inproj_call(x, norm_w, sc, sh, w_main, w_kt, w_a, w_at, *, tm=512):
    B, L, D = x.shape
    n_main = w_main.shape[1]
    const = lambda b, i: (0, 0)
    tok = lambda b, i: (b, i, 0)
    bvec = lambda b, i: (b, 0, 0)
    return pl.pallas_call(
        functools.partial(_inproj_kernel, tm=tm),
        grid=(B, L // tm),
        in_specs=[pl.BlockSpec((1, tm, D), tok),
                  pl.BlockSpec((1, D), const),
                  pl.BlockSpec((1, 1, D), bvec),
                  pl.BlockSpec((1, 1, D), bvec),
                  pl.BlockSpec((D, n_main), const, pipeline_mode=pl.Buffered(1)),
                  pl.BlockSpec((GLA_QK, D), const),
                  pl.BlockSpec((D, LANES), const),
                  pl.BlockSpec((LANES, D), const)],
        out_specs=(pl.BlockSpec((1, tm, HY_COLS), tok),
                   pl.BlockSpec((1, tm, GLA_QK + 2 * GLA_VW), tok),
                   pl.BlockSpec((1, tm, 2 * D), tok),
                   pl.BlockSpec((1, tm // PAIR, GLA_QK, PAIR), lambda b, i: (b, i, 0, 0)),
                   pl.BlockSpec((1, tm, LANES), tok),
                   pl.BlockSpec((1, LANES, tm), lambda b, i: (b, 0, i))),
        out_shape=(jax.ShapeDtypeStruct((B, L, HY_COLS), BF16),
                   jax.ShapeDtypeStruct((B, L, GLA_QK + 2 * GLA_VW), BF16),
                   jax.ShapeDtypeStruct((B, L, 2 * D), BF16),
                   jax.ShapeDtypeStruct((B, L // PAIR, GLA_QK, PAIR), BF16),
                   jax.ShapeDtypeStruct((B, L, LANES), F32),
                   jax.ShapeDtypeStruct((B, LANES, L), F32)),
        compiler_params=_cp(("arbitrary", "arbitrary")),
        name="inproj",
    )(x, norm_w.reshape(1, D), sc, sh, w_main, w_kt, w_a, w_at)


def _short_conv_rows(u_ref, w_ref, t0, rows, col0, ncol, L):
    halo = 16
    lo = max(t0 - halo, 0)
    hi = min(t0 + rows + halo, L)
    n = hi - lo
    blk = u_ref[0, lo:hi, col0:col0 + ncol].astype(F32)
    ridx = lax.broadcasted_iota(jnp.int32, (n, 1), 0)
    prev = pltpu.roll(blk, 1, axis=0)
    nxt = pltpu.roll(blk, n - 1, axis=0)
    if lo == 0:
        prev = jnp.where(ridx == 0, 0.0, prev)
    if hi == L:
        nxt = jnp.where(ridx == n - 1, 0.0, nxt)
    w = w_ref[:, col0:col0 + ncol]
    out = prev * w[0:1] + blk * w[1:2] + nxt * w[2:3]
    return out[t0 - lo:t0 - lo + rows]


def _hyena_kernel(u_ref, cw_ref, bias_ref, knyq_ref, crow_ref, srow_ref, ccol_ref, scol_ref,
                  a_ref, b_ref, o_ref, zb_ref, acc_ref, znyq_ref, *, L, rb):
    j = pl.program_id(1)
    W = HY_WIDTH

    def z_rows(t0):
        x1 = _short_conv_rows(u_ref, cw_ref, t0, rb, W, W, L)
        v = _short_conv_rows(u_ref, cw_ref, t0, rb, 2 * W, W, L)
        return x1 * v

    def sign_rows(t0):
        t = lax.broadcasted_iota(jnp.int32, (rb, 1), 0) + t0
        return jnp.where((t & 1) == 0, 1.0, -1.0)

    @pl.when(j == 0)
    def _():
        znyq = jnp.zeros((1, W), F32)
        for t0 in range(0, L, rb):
            z = z_rows(t0)
            zb_ref[t0:t0 + rb, :] = z.astype(BF16)
            znyq = znyq + jnp.sum(sign_rows(t0) * z, axis=0, keepdims=True)
        znyq_ref[...] = jnp.broadcast_to(znyq, znyq_ref.shape)
        acc_ref[...] = jnp.zeros_like(acc_ref)

    zb = zb_ref[...]
    zre = _dot(crow_ref[...], zb)
    p = _dot(srow_ref[...], zb)
    a = a_ref[...]
    b = b_ref[...]
    u = (zre * a + p * b).astype(BF16)
    v = (p * a - zre * b).astype(BF16)
    acc_ref[...] += _dot(ccol_ref[...], u) + _dot(scol_ref[...], v)

    @pl.when(j == pl.num_programs(1) - 1)
    def _():
        ynyq = znyq_ref[0:1, :] * knyq_ref[...]
        bias = bias_ref[...]
        for t0 in range(0, L, rb):
            z = z_rows(t0)
            x0 = _short_conv_rows(u_ref, cw_ref, t0, rb, 0, W, L)
            y = acc_ref[t0:t0 + rb, :] + sign_rows(t0) * ynyq + bias * z
            o_ref[0, t0:t0 + rb, :] = (x0 * y).astype(o_ref.dtype)


def hyena_call(u_hy, conv_w, hy_bias, knyq, cos_b, sin_b, a_co, b_co, *, fc=256, rb=256):
    B, L, _ = u_hy.shape
    W = HY_WIDTH
    const = lambda b, j: (0, 0)
    rowblk = lambda b, j: (j, 0)
    colblk = lambda b, j: (0, j)
    return pl.pallas_call(
        functools.partial(_hyena_kernel, L=L, rb=rb),
        grid=(B, L // fc),
        in_specs=[pl.BlockSpec((1, L, HY_COLS), lambda b, j: (b, 0, 0)),
                  pl.BlockSpec((3, HY_COLS), const),
                  pl.BlockSpec((1, W), const),
                  pl.BlockSpec((1, W), const),
                  pl.BlockSpec((fc, L), rowblk),
                  pl.BlockSpec((fc, L), rowblk),
                  pl.BlockSpec((L, fc), colblk),
                  pl.BlockSpec((L, fc), colblk),
                  pl.BlockSpec((fc, W), rowblk),
                  pl.BlockSpec((fc, W), rowblk)],
        out_specs=pl.BlockSpec((1, L, W), lambda b, j: (b, 0, 0)),
        out_shape=jax.ShapeDtypeStruct((B, L, W), BF16),
        scratch_shapes=[pltpu.VMEM((L, W), BF16),
                        pltpu.VMEM((L, W), F32),
                        pltpu.VMEM((8, W), F32)],
        compiler_params=_cp(("arbitrary", "arbitrary")),
        name="hyena",
    )(u_hy, conv_w, hy_bias.reshape(1, W), knyq, cos_b, sin_b, cos_b, sin_b, a_co, b_co)


def _split3(x):
    x1 = x.astype(BF16)
    r = x - x1.astype(F32)
    x2 = r.astype(BF16)
    x3 = (r - x2.astype(F32)).astype(BF16)
    return x1, x2, x3


def _tri_left(m, x):
    x1, x2, x3 = _split3(x)
    return _dot(m, x1) + _dot(m, x2) + _dot(m, x3)


def _tri_right(x, m):
    x1, x2, x3 = _split3(x)
    return _dot(x1, m) + _dot(x2, m) + _dot(x3, m)


def _tri_mats(n, block):
    r = np.arange(n)
    same = (r[:, None] // block) == (r[None, :] // block)
    lower = same & (r[:, None] >= r[None, :])
    upper = same & (r[:, None] <= r[None, :])
    return jnp.asarray(lower, BF16), jnp.asarray(upper, BF16)


def _log_decay_tok(a, wup_ref, bias_ref):
    return _log_sigmoid(_dot_hi(a, wup_ref[...]) + bias_ref[...]) * (1.0 / GLA_TAU)


def _log_decay_feat(at, wupt_ref, biast_ref):
    return _log_sigmoid(_dot_hi(wupt_ref[...], at) + biast_ref[...]) * (1.0 / GLA_TAU)


def _ctx_kernel(x_ref, nw_ref, sc_ref, sh_ref, wkt_ref, wv_ref, wat_ref,
                wuptf_ref, biastf_ref, wuptb_ref, biastb_ref, lo_ref, up_ref, sf_ref, sb_ref):
    h = _modulated_norm(x_ref, nw_ref, sc_ref, sh_ref)
    nt = (((1,), (1,)), ((), ()))
    kt = lax.dot_general(wkt_ref[...], h, nt, preferred_element_type=F32)
    v = _dot(h, wv_ref[...]).astype(BF16)
    at = lax.dot_general(wat_ref[...], h, nt, preferred_element_type=F32)
    la_f = _log_decay_feat(at, wuptf_ref, biastf_ref)
    la_b = _log_decay_feat(at, wuptb_ref, biastb_ref)
    lo = lo_ref[...]
    up = up_ref[...]
    w_f = jnp.exp(_tri_right(la_f, lo) - la_f)
    w_b = jnp.exp(_tri_right(la_b, up) - la_b)
    kf = (kt * w_f).astype(BF16)
    kb = (kt * w_b).astype(BF16)
    for hd in range(GLA_HEADS):
        r0, c0 = hd * GLA_DK, hd * GLA_DV
        sf_ref[0, hd] = _dot(kf[r0:r0 + GLA_DK], v[:, c0:c0 + GLA_DV])
        sb_ref[0, hd] = _dot(kb[r0:r0 + GLA_DK], v[:, c0:c0 + GLA_DV])


def ctx_call(ctx, norm_w, csc, csh, w_kt, w_v, w_at, wupt_f, biast_f, wupt_b, biast_b):
    B, Lc, D = ctx.shape
    lo, up = _tri_mats(Lc, Lc)
    const2 = lambda b: (0, 0)
    const3 = lambda b: (0, 0, 0)
    st = jax.ShapeDtypeStruct((B, GLA_HEADS, GLA_DK, GLA_DV), F32)
    st_spec = pl.BlockSpec((1, GLA_HEADS, GLA_DK, GLA_DV), lambda b: (b, 0, 0, 0))
    return pl.pallas_call(
        _ctx_kernel,
        grid=(B,),
        in_specs=[pl.BlockSpec((1, Lc, D), lambda b: (b, 0, 0)),
                  pl.BlockSpec((1, D), const2),
                  pl.BlockSpec((1, 1, D), const3),
                  pl.BlockSpec((1, 1, D), const3),
                  pl.BlockSpec((GLA_QK, D), const2),
                  pl.BlockSpec((D, GLA_VW), const2),
                  pl.BlockSpec((LANES, D), const2),
                  pl.BlockSpec((GLA_QK, LANES), const2),
                  pl.BlockSpec((GLA_QK, 1), const2),
                  pl.BlockSpec((GLA_QK, LANES), const2),
                  pl.BlockSpec((GLA_QK, 1), const2),
                  pl.BlockSpec((Lc, Lc), const2),
                  pl.BlockSpec((Lc, Lc), const2)],
        out_specs=(st_spec, st_spec),
        out_shape=(st, st),
        compiler_params=_cp(("arbitrary",)),
        name="ctx",
    )(ctx, norm_w.reshape(1, D), csc, csh, w_kt, w_v, w_at, wupt_f, biast_f, wupt_b, biast_b, lo, up)


def _gla_kernel(qvr_ref, kt_ref, a_ref, at_ref, sf_ref, sb_ref,
                wupf_ref, biasf_ref, wupb_ref, biasb_ref,
                wuptf_ref, biastf_ref, wuptb_ref, biastb_ref,
                lo_ref, up_ref, nw_ref, o_ref,
                laf_ref, lab_ref, latf_ref, latb_ref, acc_ref, st_ref, *, L, rb):
    n_pairs = L // PAIR
    C = GLA_CHUNK
    H, DK, DV = GLA_HEADS, GLA_DK, GLA_DV
    scale = DK ** -0.5

    for t0 in range(0, L, rb):
        a = a_ref[0, t0:t0 + rb, :]
        laf_ref[t0:t0 + rb, :] = _log_decay_tok(a, wupf_ref, biasf_ref)
        lab_ref[t0:t0 + rb, :] = _log_decay_tok(a, wupb_ref, biasb_ref)
    for p in range(n_pairs):
        at = at_ref[0, :, p * PAIR:(p + 1) * PAIR]
        latf_ref[p] = _log_decay_feat(at, wuptf_ref, biastf_ref)
        latb_ref[p] = _log_decay_feat(at, wuptb_ref, biastb_ref)

    lo = lo_ref[...]
    up = up_ref[...]
    rr = lax.broadcasted_iota(jnp.int32, (C, C), 0)
    cc = lax.broadcasted_iota(jnp.int32, (C, C), 1)

    def direction(la_ref, lat_ref, s0_ref, forward):
        m_tok = lo if forward else up
        m_feat = up if forward else lo
        keep = (rr >= cc) if forward else (rr <= cc)
        last = C - 1 if forward else 0
        for hd in range(H):
            st_ref[hd] = s0_ref[0, hd]

        def pair_body(i, carry):
            p = i if forward else n_pairs - 1 - i
            t0 = pl.multiple_of(p * PAIR, PAIR)
            b_tok = _tri_left(m_tok, la_ref[pl.ds(t0, PAIR), :])
            b_feat = _tri_right(lat_ref[p], m_feat)
            q = qvr_ref[0, pl.ds(t0, PAIR), 0:GLA_QK].astype(F32)
            qe = (q * scale * jnp.exp(b_tok)).astype(BF16)
            kt = kt_ref[0, p].astype(F32)
            ket = (kt * jnp.exp(-b_feat)).astype(BF16)
            v = qvr_ref[0, pl.ds(t0, PAIR), GLA_QK:GLA_QK + GLA_VW]
            for half in ((0, 1) if forward else (1, 0)):
                r0 = half * C
                b_last = b_feat[:, r0 + last:r0 + last + 1]
                kdt = (kt[:, r0:r0 + C] * jnp.exp(b_last - b_feat[:, r0:r0 + C])).astype(BF16)
                g = jnp.exp(b_last)
                for hd in range(H):
                    k0, v0 = hd * DK, hd * DV
                    qe_h = qe[r0:r0 + C, k0:k0 + DK]
                    v_h = v[r0:r0 + C, v0:v0 + DV]
                    s_prev = st_ref[hd]
                    sc = _dot(qe_h, ket[k0:k0 + DK, r0:r0 + C])
                    sc = jnp.where(keep, sc, 0.0).astype(BF16)
                    out = _dot(sc, v_h) + _dot(qe_h, s_prev.astype(BF16))
                    rows = pl.ds(t0 + r0, C)
                    if forward:
                        acc_ref[rows, v0:v0 + DV] = out
                    else:
                        acc_ref[rows, v0:v0 + DV] += out
                    st_ref[hd] = g[k0:k0 + DK] * s_prev + _dot(kdt[k0:k0 + DK], v_h)
            return carry

        lax.fori_loop(0, n_pairs, pair_body, 0)

    direction(laf_ref, latf_ref, sf_ref, True)
    direction(lab_ref, latb_ref, sb_ref, False)

    for t0 in range(0, L, rb):
        r = qvr_ref[0, t0:t0 + rb, GLA_QK + GLA_VW:GLA_QK + 2 * GLA_VW].astype(F32)
        for hd in range(H):
            v0 = hd * DV
            o = _rms_scale(acc_ref[t0:t0 + rb, v0:v0 + DV])
            y = o * nw_ref[:, v0:v0 + DV] * _silu(r[:, v0:v0 + DV])
            o_ref[0, t0:t0 + rb, v0:v0 + DV] = y.astype(o_ref.dtype)


def gla_call(qvr, kt, a, at, s_f, s_b, wup_f, bias_f, wup_b, bias_b,
             wupt_f, biast_f, wupt_b, biast_b, norm_w, *, rb=256):
    B, L, _ = qvr.shape
    lo, up = _tri_mats(PAIR, GLA_CHUNK)
    const2 = lambda b: (0, 0)
    st_spec = pl.BlockSpec((1, GLA_HEADS, GLA_DK, GLA_DV), lambda b: (b, 0, 0, 0))
    return pl.pallas_call(
        functools.partial(_gla_kernel, L=L, rb=rb),
        grid=(B,),
        in_specs=[pl.BlockSpec((1, L, GLA_QK + 2 * GLA_VW), lambda b: (b, 0, 0)),
                  pl.BlockSpec((1, L // PAIR, GLA_QK, PAIR), lambda b: (b, 0, 0, 0)),
                  pl.BlockSpec((1, L, LANES), lambda b: (b, 0, 0)),
                  pl.BlockSpec((1, LANES, L), lambda b: (b, 0, 0)),
                  st_spec, st_spec,
                  pl.BlockSpec((LANES, GLA_QK), const2),
                  pl.BlockSpec((1, GLA_QK), const2),
                  pl.BlockSpec((LANES, GLA_QK), const2),
                  pl.BlockSpec((1, GLA_QK), const2),
                  pl.BlockSpec((GLA_QK, LANES), const2),
                  pl.BlockSpec((GLA_QK, 1), const2),
                  pl.BlockSpec((GLA_QK, LANES), const2),
                  pl.BlockSpec((GLA_QK, 1), const2),
                  pl.BlockSpec((PAIR, PAIR), const2),
                  pl.BlockSpec((PAIR, PAIR), const2),
                  pl.BlockSpec((1, GLA_VW), const2)],
        out_specs=pl.BlockSpec((1, L, GLA_VW), lambda b: (b, 0, 0)),
        out_shape=jax.ShapeDtypeStruct((B, L, GLA_VW), BF16),
        scratch_shapes=[pltpu.VMEM((L, GLA_QK), F32),
                        pltpu.VMEM((L, GLA_QK), F32),
                        pltpu.VMEM((L // PAIR, GLA_QK, PAIR), F32),
                        pltpu.VMEM((L // PAIR, GLA_QK, PAIR), F32),
                        pltpu.VMEM((L, GLA_VW), F32),
                        pltpu.VMEM((GLA_HEADS, GLA_DK, GLA_DV), F32)],
        compiler_params=_cp(("arbitrary",)),
        name="gla",
    )(qvr, kt, a, at, s_f, s_b, wup_f, bias_f, wup_b, bias_b,
      wupt_f, biast_f, wupt_b, biast_b, lo, up, norm_w.reshape(1, GLA_VW))


def _merge_kernel(x_ref, yhy_ref, ygla_ref, gate_ref, g1_ref, whp_ref, wgp_ref, wo_ref,
                  nw_ref, sc_ref, sh_ref, xn_ref, h2_ref):
    D = D_MODEL
    g_hy = _sigmoid(gate_ref[0, :, 0:D].astype(F32))
    g_gla = _sigmoid(gate_ref[0, :, D:2 * D].astype(F32))
    merged = g_hy * _dot(yhy_ref[0], whp_ref[...]) + g_gla * _dot(ygla_ref[0], wgp_ref[...])
    mix = _dot(merged.astype(BF16), wo_ref[...])
    xn = x_ref[0] + g1_ref[0] * mix
    xn_ref[0] = xn
    y = _rms_scale(xn) * nw_ref[...]
    h2_ref[0] = (y * (1.0 + sc_ref[0]) + sh_ref[0]).astype(h2_ref.dtype)


def merge_call(x, y_hy, y_gla, gates, g1, w_hy_proj, w_gla_proj, w_out, norm2_w, sc2, sh2, *, tm=512):
    B, L, D = x.shape
    tok = lambda b, i: (b, i, 0)
    bvec = lambda b, i: (b, 0, 0)
    const = lambda b, i: (0, 0)
    return pl.pallas_call(
        _merge_kernel,
        grid=(B, L // tm),
        in_specs=[pl.BlockSpec((1, tm, D), tok),
                  pl.BlockSpec((1, tm, HY_WIDTH), tok),
                  pl.BlockSpec((1, tm, GLA_VW), tok),
                  pl.BlockSpec((1, tm, 2 * D), tok),
                  pl.BlockSpec((1, 1, D), bvec),
                  pl.BlockSpec((HY_WIDTH, D), const),
                  pl.BlockSpec((GLA_VW, D), const),
                  pl.BlockSpec((D, D), const),
                  pl.BlockSpec((1, D), const),
                  pl.BlockSpec((1, 1, D), bvec),
                  pl.BlockSpec((1, 1, D), bvec)],
        out_specs=(pl.BlockSpec((1, tm, D), tok), pl.BlockSpec((1, tm, D), tok)),
        out_shape=(jax.ShapeDtypeStruct((B, L, D), F32), jax.ShapeDtypeStruct((B, L, D), BF16)),
        compiler_params=_cp(("arbitrary", "arbitrary")),
        name="merge",
    )(x, y_hy, y_gla, gates, g1, w_hy_proj, w_gla_proj, w_out, norm2_w.reshape(1, D), sc2, sh2)


FFN_PAD = GRID_W + 8


def _ffn_kernel(h_ref, xn_ref, g2_ref, wup_ref, wgate_ref, dw_ref, dwb_ref, wout_ref, fw_ref,
                o_ref, gate_ref, up_ref, act_ref, *, L, rb):
    j = pl.program_id(1)
    hc = wup_ref.shape[1]
    h = h_ref[0]
    gate_ref[0:FFN_PAD, :] = jnp.zeros((FFN_PAD, hc), F32)
    gate_ref[FFN_PAD + L:FFN_PAD + L + FFN_PAD, :] = jnp.zeros((FFN_PAD, hc), F32)
    gate_ref[FFN_PAD:FFN_PAD + L, :] = _dot(h, wgate_ref[...])
    up_ref[...] = _dot(h, wup_ref[...])
    dw = dw_ref[...]
    col = lax.broadcasted_iota(jnp.int32, (rb, 1), 0) & (GRID_W - 1)
    for t0 in range(0, L, rb):
        conv = jnp.broadcast_to(dwb_ref[...], (rb, hc))
        for dy in range(3):
            base = FFN_PAD + t0 + (dy - 1) * GRID_W
            centre = gate_ref[base:base + rb, :]
            left = jnp.where(col == 0, 0.0, gate_ref[base - 1:base - 1 + rb, :])
            right = jnp.where(col == GRID_W - 1, 0.0, gate_ref[base + 1:base + 1 + rb, :])
            conv = (conv + left * dw[3 * dy:3 * dy + 1] + centre * dw[3 * dy + 1:3 * dy + 2]
                    + right * dw[3 * dy + 2:3 * dy + 3])
        act_ref[t0:t0 + rb, :] = (_silu(conv) * up_ref[t0:t0 + rb, :]).astype(BF16)

    @pl.when(j == 0)
    def _():
        o_ref[...] = jnp.zeros_like(o_ref)

    o_ref[0] += _dot(act_ref[...], wout_ref[...])

    @pl.when(j == pl.num_programs(1) - 1)
    def _():
        for t0 in range(0, L, rb):
            xo = xn_ref[0, t0:t0 + rb, :] + g2_ref[0] * o_ref[0, t0:t0 + rb, :]
            o_ref[0, t0:t0 + rb, :] = _rms_scale(xo) * fw_ref[...]


def ffn_call(h2, x_new, g2, w_up, w_gate, dw, dw_bias, w_out, final_w, *, hc=256, rb=256):
    B, L, D = h2.shape
    F = w_up.shape[1]
    const = lambda b, j: (0, 0)
    return pl.pallas_call(
        functools.partial(_ffn_kernel, L=L, rb=rb),
        grid=(B, F // hc),
        in_specs=[pl.BlockSpec((1, L, D), lambda b, j: (b, 0, 0)),
                  pl.BlockSpec((1, L, D), lambda b, j: (b, 0, 0), pipeline_mode=pl.Buffered(1)),
                  pl.BlockSpec((1, 1, D), lambda b, j: (b, 0, 0)),
                  pl.BlockSpec((D, hc), lambda b, j: (0, j)),
                  pl.BlockSpec((D, hc), lambda b, j: (0, j)),
                  pl.BlockSpec((9, hc), lambda b, j: (0, j)),
                  pl.BlockSpec((1, hc), lambda b, j: (0, j)),
                  pl.BlockSpec((hc, D), lambda b, j: (j, 0)),
                  pl.BlockSpec((1, D), const)],
        out_specs=pl.BlockSpec((1, L, D), lambda b, j: (b, 0, 0)),
        out_shape=jax.ShapeDtypeStruct((B, L, D), F32),
        scratch_shapes=[pltpu.VMEM((L + 2 * FFN_PAD, hc), F32),
                        pltpu.VMEM((L, hc), F32),
                        pltpu.VMEM((L, hc), BF16)],
        compiler_params=_cp(("arbitrary", "arbitrary")),
        name="ffn",
    )(h2, x_new, g2, w_up, w_gate, dw, dw_bias.reshape(1, F), w_out, final_w.reshape(1, D))


def _pad_rows(w, rows, at=0):
    out = jnp.zeros((rows, w.shape[1]), w.dtype)
    return lax.dynamic_update_slice(out, w, (at, 0))


def kernel(x, c, ctx, c_ctx, w_ada, b_ada, norm1_w, norm2_w, w_in, hy_conv_w, hy_filt_w1, hy_filt_b1,
           hy_filt_w2, hy_filt_b2, hy_filt_w3, hy_filt_freq, hy_decay, hy_bias, gla_a_up_f, gla_a_bias_f,
           gla_a_up_b, gla_a_bias_b, gla_norm_w, w_hy_proj, w_gla_proj, w_out, ffn_w_in, ffn_dw,
           ffn_dw_bias, ffn_w_out, final_norm_w):
    assert w_ada.shape[0] == 1, "single-layer block"
    B, L, D = x.shape
    R = GLA_LOWRANK

    n_rows = -(-(B + 1) // 8) * 8
    c_rows = jnp.zeros((n_rows, D), F32).at[:B].set(c).at[B].set(c_ctx)
    mod = ada_call(c_rows, w_ada[0], b_ada[0])
    sh1, sc1, g1, sh2, sc2, g2 = [m[:B, None, :] for m in jnp.split(mod, 6, axis=-1)]
    csh1, csc1 = mod[B:B + 1, None, 0:D], mod[B:B + 1, None, D:2 * D]

    w = w_in[0]
    o_q = HY_COLS
    o_k = o_q + GLA_QK
    o_v = o_k + GLA_QK
    o_r = o_v + GLA_VW
    o_af = o_r + GLA_VW
    o_ab = o_af + R
    o_g = o_ab + R
    w_main = jnp.concatenate([w[:, :o_k], w[:, o_v:o_af], w[:, o_g:]], axis=1).astype(BF16)
    w_kt = w[:, o_k:o_v].T.astype(BF16)
    w_v = w[:, o_v:o_r].astype(BF16)
    w_a = jnp.pad(w[:, o_af:o_g], ((0, 0), (0, LANES - 2 * R))).astype(BF16)
    w_at = w_a.T
    wup_f = _pad_rows(gla_a_up_f[0], LANES, 0)
    wup_b = _pad_rows(gla_a_up_b[0], LANES, R)
    wupt_f, wupt_b = wup_f.T, wup_b.T
    bias_f, bias_b = gla_a_bias_f[0].reshape(1, GLA_QK), gla_a_bias_b[0].reshape(1, GLA_QK)
    biast_f, biast_b = bias_f.reshape(GLA_QK, 1), bias_b.reshape(GLA_QK, 1)

    cos32, sin32 = dft_tables(L)
    a_co, b_co, knyq = filter_call(L, hy_filt_w1[0], hy_filt_b1[0], hy_filt_w2[0], hy_filt_b2[0],
                                   hy_filt_w3[0], hy_filt_freq[0], hy_decay[0], cos32, sin32)

    s_f, s_b = ctx_call(ctx, norm1_w[0], csc1, csh1, w_kt, w_v, w_at, wupt_f, biast_f, wupt_b, biast_b)

    u_hy, qvr, gates, kt, a, at = inproj_call(x, norm1_w[0], sc1, sh1, w_main, w_kt, w_a, w_at)
    y_hy = hyena_call(u_hy, hy_conv_w[0], hy_bias[0], knyq, cos32.astype(BF16), sin32.astype(BF16), a_co, b_co)
    y_gla = gla_call(qvr, kt, a, at, s_f, s_b, wup_f, bias_f, wup_b, bias_b,
                     wupt_f, biast_f, wupt_b, biast_b, gla_norm_w[0])
    x_new, h2 = merge_call(x, y_hy, y_gla, gates, g1, w_hy_proj[0].astype(BF16), w_gla_proj[0].astype(BF16),
                           w_out[0].astype(BF16), norm2_w[0], sc2, sh2)
    w_ffn = ffn_w_in[0]
    return ffn_call(h2, x_new, g2, w_ffn[:, :FFN_HIDDEN].astype(BF16), w_ffn[:, FFN_HIDDEN:].astype(BF16),
                    ffn_dw[0].reshape(9, FFN_HIDDEN), ffn_dw_bias[0], ffn_w_out[0].astype(BF16), final_norm_w)
```

```python
import functools
import math

import jax
import jax.numpy as jnp
import numpy as np
from jax import lax
from jax.experimental import pallas as pl
from jax.experimental.pallas import tpu as pltpu

F32 = jnp.float32
BF16 = jnp.bfloat16
HIGHEST = lax.Precision.HIGHEST

D_MODEL = 1024
GRID_W = 64
HY_WIDTH = 512
HY_BANDS = 16
HY_EMB = 1 + 2 * HY_BANDS
HY_FILT_HIDDEN = 64
GLA_HEADS = 4
GLA_DK = 64
GLA_DV = 128
GLA_LOWRANK = 16
GLA_TAU = 16.0
GLA_CHUNK = 64
FFN_HIDDEN = 2816
EPS = 1e-6
HY_COLS = 3 * HY_WIDTH
GLA_QK = GLA_HEADS * GLA_DK
GLA_VW = GLA_HEADS * GLA_DV

LANES = 128
PAIR = 2 * GLA_CHUNK
VMEM_LIMIT = 56 * 1024 * 1024


def _cp(sem):
    return pltpu.CompilerParams(dimension_semantics=sem, vmem_limit_bytes=VMEM_LIMIT)


def _dot(a, b):
    return jnp.dot(a, b, preferred_element_type=F32)


def _dot_hi(a, b):
    return jnp.dot(a, b, preferred_element_type=F32, precision=HIGHEST)


def _sigmoid(x):
    return 1.0 / (1.0 + jnp.exp(-x))


def _silu(x):
    return x * _sigmoid(x)


def _log_sigmoid(x):
    return jnp.minimum(x, 0.0) - jnp.log(1.0 + jnp.exp(-jnp.abs(x)))


def _rms_scale(xf):
    return xf * lax.rsqrt(jnp.mean(xf * xf, axis=-1, keepdims=True) + EPS)


def _ada_kernel(c_ref, w_ref, b_ref, o_ref):
    o_ref[...] = _dot_hi(_silu(c_ref[...]), w_ref[...]) + b_ref[...]


def ada_call(c_rows, w_ada, b_ada):
    rows, d = c_rows.shape
    n = w_ada.shape[1]
    bn = 512
    return pl.pallas_call(
        _ada_kernel,
        grid=(n // bn,),
        in_specs=[pl.BlockSpec((rows, d), lambda j: (0, 0)),
                  pl.BlockSpec((d, bn), lambda j: (0, j)),
                  pl.BlockSpec((1, bn), lambda j: (0, j))],
        out_specs=pl.BlockSpec((rows, bn), lambda j: (0, j)),
        out_shape=jax.ShapeDtypeStruct((rows, n), F32),
        compiler_params=_cp(("arbitrary",)),
        name="ada",
    )(c_rows, w_ada, b_ada.reshape(1, n))


def _filt_mlp_kernel(feats_ref, t_ref, w1_ref, b1_ref, w2_ref, b2_ref, w3_ref, freq_ref, decay_ref,
                     ksum_ref, kdiff_ref, knyq_ref):
    freq = freq_ref[...]
    h = jnp.sin(freq * (_dot_hi(feats_ref[...], w1_ref[...]) + b1_ref[...]))
    h = jnp.sin(freq * (_dot_hi(h, w2_ref[...]) + b2_ref[...]))
    k = _dot_hi(h, w3_ref[...]) * jnp.exp(-t_ref[...] * decay_ref[...])
    L = k.shape[0]
    row = lax.broadcasted_iota(jnp.int32, (L, HY_WIDTH), 0)
    k_f = k[:, :HY_WIDTH]
    k_b = jnp.where(row == 0, 0.0, k[:, HY_WIDTH:])
    norm = jnp.sum(jnp.abs(k_f), axis=0, keepdims=True) + jnp.sum(jnp.abs(k_b), axis=0, keepdims=True)
    k_f = k_f / norm
    k_b = k_b / norm
    ksum = k_f + k_b
    ksum_ref[...] = ksum
    kdiff_ref[...] = k_b - k_f
    sign = jnp.where((row & 1) == 0, 1.0, -1.0)
    knyq_ref[...] = jnp.sum(sign * ksum, axis=0, keepdims=True) * (1.0 / (2 * L))


def _filt_dft_kernel(c_ref, s_ref, ksum_ref, kdiff_ref, a_ref, b_ref, *, n_fft, fc):
    j = pl.program_id(0)
    f = lax.broadcasted_iota(jnp.int32, (fc, 1), 0) + j * fc
    w = jnp.where(f == 0, 1.0 / n_fft, 2.0 / n_fft)
    a_ref[...] = w * _dot_hi(c_ref[...], ksum_ref[...])
    b_ref[...] = w * _dot_hi(s_ref[...], kdiff_ref[...])


def filter_call(L, hy_filt_w1, hy_filt_b1, hy_filt_w2, hy_filt_b2, hy_filt_w3, hy_filt_freq, hy_decay,
                cos32, sin32):
    t = jnp.linspace(0.0, 1.0, L, dtype=F32)[:, None]
    w = (2.0 * np.pi / L) * jnp.arange(L, dtype=F32)[:, None]
    f = jnp.linspace(1e-4, HY_BANDS - 1, HY_BANDS, dtype=F32)[None, :]
    feats = jnp.concatenate([t, jnp.cos(f * w), -jnp.sin(f * w)], axis=-1)
    feats = jnp.pad(feats, ((0, 0), (0, LANES - HY_EMB)))
    w1 = jnp.pad(hy_filt_w1, ((0, LANES - HY_EMB), (0, 0)))
    hid = HY_FILT_HIDDEN
    ksum, kdiff, knyq = pl.pallas_call(
        _filt_mlp_kernel,
        out_shape=(jax.ShapeDtypeStruct((L, HY_WIDTH), F32),
                   jax.ShapeDtypeStruct((L, HY_WIDTH), F32),
                   jax.ShapeDtypeStruct((1, HY_WIDTH), F32)),
        compiler_params=pltpu.CompilerParams(vmem_limit_bytes=VMEM_LIMIT),
        name="filt_mlp",
    )(feats, t, w1, hy_filt_b1.reshape(1, hid), hy_filt_w2, hy_filt_b2.reshape(1, hid), hy_filt_w3,
      hy_filt_freq.reshape(1, hid), hy_decay.reshape(1, 2 * HY_WIDTH))
    fc = 256
    a_co, b_co = pl.pallas_call(
        functools.partial(_filt_dft_kernel, n_fft=2 * L, fc=fc),
        grid=(L // fc,),
        in_specs=[pl.BlockSpec((fc, L), lambda j: (j, 0)),
                  pl.BlockSpec((fc, L), lambda j: (j, 0)),
                  pl.BlockSpec((L, HY_WIDTH), lambda j: (0, 0)),
                  pl.BlockSpec((L, HY_WIDTH), lambda j: (0, 0))],
        out_specs=(pl.BlockSpec((fc, HY_WIDTH), lambda j: (j, 0)),
                   pl.BlockSpec((fc, HY_WIDTH), lambda j: (j, 0))),
        out_shape=(jax.ShapeDtypeStruct((L, HY_WIDTH), F32),
                   jax.ShapeDtypeStruct((L, HY_WIDTH), F32)),
        compiler_params=_cp(("arbitrary",)),
        name="filt_dft",
    )(cos32, sin32, ksum, kdiff)
    return a_co, b_co, knyq


def dft_tables(L):
    n_fft = 2 * L
    idx = (jnp.arange(L, dtype=jnp.int32)[:, None] * jnp.arange(L, dtype=jnp.int32)[None, :]) % n_fft
    ang = idx.astype(F32) * (2.0 * np.pi / n_fft)
    return jnp.cos(ang), jnp.sin(ang)


def _modulated_norm(x_ref, nw_ref, sc_ref, sh_ref):
    y = _rms_scale(x_ref[0].astype(F32)) * nw_ref[...]
    return (y * (1.0 + sc_ref[0]) + sh_ref[0]).astype(BF16)


def _inproj_kernel(x_ref, nw_ref, sc_ref, sh_ref, w_ref, wkt_ref, wa_ref, wat_ref,
                   hy_ref, qvr_ref, gate_ref, kt_ref, a_ref, at_ref, *, tm):
    h = _modulated_norm(x_ref, nw_ref, sc_ref, sh_ref)
    col = 0
    for o_ref in (hy_ref, qvr_ref, gate_ref):
        width = o_ref.shape[-1]
        done = 0
        while done < width:
            step = min(512, width - done)
            o_ref[0, :, done:done + step] = _dot(h, w_ref[:, col:col + step]).astype(o_ref.dtype)
            done += step
            col += step
    nt = (((1,), (1,)), ((), ()))
    kt = lax.dot_general(wkt_ref[...], h, nt, preferred_element_type=F32).astype(BF16)
    for p in range(tm // PAIR):
        kt_ref[0, p] = kt[:, p * PAIR:(p + 1) * PAIR]
    a_ref[0] = _dot(h, wa_ref[...])
    at_ref[0] = lax.dot_general(wat_ref[...], h, nt, preferred_element_type=F32)


def inproj_call(x, norm_w, sc, sh, w_main, w_kt, w_a, w_at, *, tm=512):
    B, L, D = x.shape
    n_main = w_main.shape[1]
    const = lambda b, i: (0, 0)
    tok = lambda b, i: (b, i, 0)
    bvec = lambda b, i: (b, 0, 0)
    return pl.pallas_call(
        functools.partial(_inproj_kernel, tm=tm),
        grid=(B, L // tm),
        in_specs=[pl.BlockSpec((1, tm, D), tok),
                  pl.BlockSpec((1, D), const),
                  pl.BlockSpec((1, 1, D), bvec),
                  pl.BlockSpec((1, 1, D), bvec),
                  pl.BlockSpec((D, n_main), const, pipeline_mode=pl.Buffered(1)),
                  pl.BlockSpec((GLA_QK, D), const),
                  pl.BlockSpec((D, LANES), const),
                  pl.BlockSpec((LANES, D), const)],
        out_specs=(pl.BlockSpec((1, tm, HY_COLS), tok),
                   pl.BlockSpec((1, tm, GLA_QK + 2 * GLA_VW), tok),
                   pl.BlockSpec((1, tm, 2 * D), tok),
                   pl.BlockSpec((1, tm // PAIR, GLA_QK, PAIR), lambda b, i: (b, i, 0, 0)),
                   pl.BlockSpec((1, tm, LANES), tok),
                   pl.BlockSpec((1, LANES, tm), lambda b, i: (b, 0, i))),
        out_shape=(jax.ShapeDtypeStruct((B, L, HY_COLS), BF16),
                   jax.ShapeDtypeStruct((B, L, GLA_QK + 2 * GLA_VW), BF16),
                   jax.ShapeDtypeStruct((B, L, 2 * D), BF16),
                   jax.ShapeDtypeStruct((B, L // PAIR, GLA_QK, PAIR), BF16),
                   jax.ShapeDtypeStruct((B, L, LANES), F32),
                   jax.ShapeDtypeStruct((B, LANES, L), F32)),
        compiler_params=_cp(("arbitrary", "arbitrary")),
        name="inproj",
    )(x, norm_w.reshape(1, D), sc, sh, w_main, w_kt, w_a, w_at)


def _short_conv_rows(u_ref, w_ref, t0, rows, col0, ncol, L):
    halo = 16
    lo = max(t0 - halo, 0)
    hi = min(t0 + rows + halo, L)
    n = hi - lo
    blk = u_ref[0, lo:hi, col0:col0 + ncol].astype(F32)
    ridx = lax.broadcasted_iota(jnp.int32, (n, 1), 0)
    prev = pltpu.roll(blk, 1, axis=0)
    nxt = pltpu.roll(blk, n - 1, axis=0)
    if lo == 0:
        prev = jnp.where(ridx == 0, 0.0, prev)
    if hi == L:
        nxt = jnp.where(ridx == n - 1, 0.0, nxt)
    w = w_ref[:, col0:col0 + ncol]
    out = prev * w[0:1] + blk * w[1:2] + nxt * w[2:3]
    return out[t0 - lo:t0 - lo + rows]


def _hyena_kernel(u_ref, cw_ref, bias_ref, knyq_ref, crow_ref, srow_ref, ccol_ref, scol_ref,
                  a_ref, b_ref, o_ref, zb_ref, acc_ref, znyq_ref, *, L, rb):
    j = pl.program_id(1)
    W = HY_WIDTH

    def z_rows(t0):
        x1 = _short_conv_rows(u_ref, cw_ref, t0, rb, W, W, L)
        v = _short_conv_rows(u_ref, cw_ref, t0, rb, 2 * W, W, L)
        return x1 * v

    def sign_rows(t0):
        t = lax.broadcasted_iota(jnp.int32, (rb, 1), 0) + t0
        return jnp.where((t & 1) == 0, 1.0, -1.0)

    @pl.when(j == 0)
    def _():
        znyq = jnp.zeros((1, W), F32)
        for t0 in range(0, L, rb):
            z = z_rows(t0)
            zb_ref[t0:t0 + rb, :] = z.astype(BF16)
            znyq = znyq + jnp.sum(sign_rows(t0) * z, axis=0, keepdims=True)
        znyq_ref[...] = jnp.broadcast_to(znyq, znyq_ref.shape)
        acc_ref[...] = jnp.zeros_like(acc_ref)

    zb = zb_ref[...]
    zre = _dot(crow_ref[...], zb)
    p = _dot(srow_ref[...], zb)
    a = a_ref[...]
    b = b_ref[...]
    u = (zre * a + p * b).astype(BF16)
    v = (p * a - zre * b).astype(BF16)
    acc_ref[...] += _dot(ccol_ref[...], u) + _dot(scol_ref[...], v)

    @pl.when(j == pl.num_programs(1) - 1)
    def _():
        ynyq = znyq_ref[0:1, :] * knyq_ref[...]
        bias = bias_ref[...]
        for t0 in range(0, L, rb):
            z = z_rows(t0)
            x0 = _short_conv_rows(u_ref, cw_ref, t0, rb, 0, W, L)
            y = acc_ref[t0:t0 + rb, :] + sign_rows(t0) * ynyq + bias * z
            o_ref[0, t0:t0 + rb, :] = (x0 * y).astype(o_ref.dtype)


def hyena_call(u_hy, conv_w, hy_bias, knyq, cos_b, sin_b, a_co, b_co, *, fc=256, rb=256):
    B, L, _ = u_hy.shape
    W = HY_WIDTH
    const = lambda b, j: (0, 0)
    rowblk = lambda b, j: (j, 0)
    colblk = lambda b, j: (0, j)
    return pl.pallas_call(
        functools.partial(_hyena_kernel, L=L, rb=rb),
        grid=(B, L // fc),
        in_specs=[pl.BlockSpec((1, L, HY_COLS), lambda b, j: (b, 0, 0)),
                  pl.BlockSpec((3, HY_COLS), const),
                  pl.BlockSpec((1, W), const),
                  pl.BlockSpec((1, W), const),
                  pl.BlockSpec((fc, L), rowblk),
                  pl.BlockSpec((fc, L), rowblk),
                  pl.BlockSpec((L, fc), colblk),
                  pl.BlockSpec((L, fc), colblk),
                  pl.BlockSpec((fc, W), rowblk),
                  pl.BlockSpec((fc, W), rowblk)],
        out_specs=pl.BlockSpec((1, L, W), lambda b, j: (b, 0, 0)),
        out_shape=jax.ShapeDtypeStruct((B, L, W), BF16),
        scratch_shapes=[pltpu.VMEM((L, W), BF16),
                        pltpu.VMEM((L, W), F32),
                        pltpu.VMEM((8, W), F32)],
        compiler_params=_cp(("arbitrary", "arbitrary")),
        name="hyena",
    )(u_hy, conv_w, hy_bias.reshape(1, W), knyq, cos_b, sin_b, cos_b, sin_b, a_co, b_co)


def _split3(x):
    x1 = x.astype(BF16)
    r = x - x1.astype(F32)
    x2 = r.astype(BF16)
    x3 = (r - x2.astype(F32)).astype(BF16)
    return x1, x2, x3


def _tri_left(m, x):
    x1, x2, x3 = _split3(x)
    return _dot(m, x1) + _dot(m, x2) + _dot(m, x3)


def _tri_right(x, m):
    x1, x2, x3 = _split3(x)
    return _dot(x1, m) + _dot(x2, m) + _dot(x3, m)


def _tri_mats(n, block):
    r = np.arange(n)
    same = (r[:, None] // block) == (r[None, :] // block)
    lower = same & (r[:, None] >= r[None, :])
    upper = same & (r[:, None] <= r[None, :])
    return jnp.asarray(lower, BF16), jnp.asarray(upper, BF16)


def _log_decay_tok(a, wup_ref, bias_ref):
    return _log_sigmoid(_dot_hi(a, wup_ref[...]) + bias_ref[...]) * (1.0 / GLA_TAU)


def _log_decay_feat(at, wupt_ref, biast_ref):
    return _log_sigmoid(_dot_hi(wupt_ref[...], at) + biast_ref[...]) * (1.0 / GLA_TAU)


def _ctx_kernel(x_ref, nw_ref, sc_ref, sh_ref, wkt_ref, wv_ref, wat_ref,
                wuptf_ref, biastf_ref, wuptb_ref, biastb_ref, lo_ref, up_ref, sf_ref, sb_ref):
    h = _modulated_norm(x_ref, nw_ref, sc_ref, sh_ref)
    nt = (((1,), (1,)), ((), ()))
    kt = lax.dot_general(wkt_ref[...], h, nt, preferred_element_type=F32)
    v = _dot(h, wv_ref[...]).astype(BF16)
    at = lax.dot_general(wat_ref[...], h, nt, preferred_element_type=F32)
    la_f = _log_decay_feat(at, wuptf_ref, biastf_ref)
    la_b = _log_decay_feat(at, wuptb_ref, biastb_ref)
    lo = lo_ref[...]
    up = up_ref[...]
    w_f = jnp.exp(_tri_right(la_f, lo) - la_f)
    w_b = jnp.exp(_tri_right(la_b, up) - la_b)
    kf = (kt * w_f).astype(BF16)
    kb = (kt * w_b).astype(BF16)
    for hd in range(GLA_HEADS):
        r0, c0 = hd * GLA_DK, hd * GLA_DV
        sf_ref[0, hd] = _dot(kf[r0:r0 + GLA_DK], v[:, c0:c0 + GLA_DV])
        sb_ref[0, hd] = _dot(kb[r0:r0 + GLA_DK], v[:, c0:c0 + GLA_DV])


def ctx_call(ctx, norm_w, csc, csh, w_kt, w_v, w_at, wupt_f, biast_f, wupt_b, biast_b):
    B, Lc, D = ctx.shape
    lo, up = _tri_mats(Lc, Lc)
    const2 = lambda b: (0, 0)
    const3 = lambda b: (0, 0, 0)
    st = jax.ShapeDtypeStruct((B, GLA_HEADS, GLA_DK, GLA_DV), F32)
    st_spec = pl.BlockSpec((1, GLA_HEADS, GLA_DK, GLA_DV), lambda b: (b, 0, 0, 0))
    return pl.pallas_call(
        _ctx_kernel,
        grid=(B,),
        in_specs=[pl.BlockSpec((1, Lc, D), lambda b: (b, 0, 0)),
                  pl.BlockSpec((1, D), const2),
                  pl.BlockSpec((1, 1, D), const3),
                  pl.BlockSpec((1, 1, D), const3),
                  pl.BlockSpec((GLA_QK, D), const2),
                  pl.BlockSpec((D, GLA_VW), const2),
                  pl.BlockSpec((LANES, D), const2),
                  pl.BlockSpec((GLA_QK, LANES), const2),
                  pl.BlockSpec((GLA_QK, 1), const2),
                  pl.BlockSpec((GLA_QK, LANES), const2),
                  pl.BlockSpec((GLA_QK, 1), const2),
                  pl.BlockSpec((Lc, Lc), const2),
                  pl.BlockSpec((Lc, Lc), const2)],
        out_specs=(st_spec, st_spec),
        out_shape=(st, st),
        compiler_params=_cp(("arbitrary",)),
        name="ctx",
    )(ctx, norm_w.reshape(1, D), csc, csh, w_kt, w_v, w_at, wupt_f, biast_f, wupt_b, biast_b, lo, up)


def _gla_kernel(qvr_ref, kt_ref, a_ref, at_ref, sf_ref, sb_ref,
                wupf_ref, biasf_ref, wupb_ref, biasb_ref,
                wuptf_ref, biastf_ref, wuptb_ref, biastb_ref,
                lo_ref, up_ref, nw_ref, o_ref,
                laf_ref, lab_ref, latf_ref, latb_ref, acc_ref, st_ref, *, L, rb):
    n_pairs = L // PAIR
    C = GLA_CHUNK
    H, DK, DV = GLA_HEADS, GLA_DK, GLA_DV
    scale = DK ** -0.5

    for t0 in range(0, L, rb):
        a = a_ref[0, t0:t0 + rb, :]
        laf_ref[t0:t0 + rb, :] = _log_decay_tok(a, wupf_ref, biasf_ref)
        lab_ref[t0:t0 + rb, :] = _log_decay_tok(a, wupb_ref, biasb_ref)
    for p in range(n_pairs):
        at = at_ref[0, :, p * PAIR:(p + 1) * PAIR]
        latf_ref[p] = _log_decay_feat(at, wuptf_ref, biastf_ref)
        latb_ref[p] = _log_decay_feat(at, wuptb_ref, biastb_ref)

    lo = lo_ref[...]
    up = up_ref[...]
    rr = lax.broadcasted_iota(jnp.int32, (C, C), 0)
    cc = lax.broadcasted_iota(jnp.int32, (C, C), 1)

    def direction(la_ref, lat_ref, s0_ref, forward):
        m_tok = lo if forward else up
        m_feat = up if forward else lo
        keep = (rr >= cc) if forward else (rr <= cc)
        last = C - 1 if forward else 0
        for hd in range(H):
            st_ref[hd] = s0_ref[0, hd]

        def pair_body(i, carry):
            p = i if forward else n_pairs - 1 - i
            t0 = pl.multiple_of(p * PAIR, PAIR)
            b_tok = _tri_left(m_tok, la_ref[pl.ds(t0, PAIR), :])
            b_feat = _tri_right(lat_ref[p], m_feat)
            q = qvr_ref[0, pl.ds(t0, PAIR), 0:GLA_QK].astype(F32)
            qe = (q * scale * jnp.exp(b_tok)).astype(BF16)
            kt = kt_ref[0, p].astype(F32)
            ket = (kt * jnp.exp(-b_feat)).astype(BF16)
            v = qvr_ref[0, pl.ds(t0, PAIR), GLA_QK:GLA_QK + GLA_VW]
            for half in ((0, 1) if forward else (1, 0)):
                r0 = half * C
                b_last = b_feat[:, r0 + last:r0 + last + 1]
                kdt = (kt[:, r0:r0 + C] * jnp.exp(b_last - b_feat[:, r0:r0 + C])).astype(BF16)
                g = jnp.exp(b_last)
                for hd in range(H):
                    k0, v0 = hd * DK, hd * DV
                    qe_h = qe[r0:r0 + C, k0:k0 + DK]
                    v_h = v[r0:r0 + C, v0:v0 + DV]
                    s_prev = st_ref[hd]
                    sc = _dot(qe_h, ket[k0:k0 + DK, r0:r0 + C])
                    sc = jnp.where(keep, sc, 0.0).astype(BF16)
                    out = _dot(sc, v_h) + _dot(qe_h, s_prev.astype(BF16))
                    rows = pl.ds(t0 + r0, C)
                    if forward:
                        acc_ref[rows, v0:v0 + DV] = out
                    else:
                        acc_ref[rows, v0:v0 + DV] += out
                    st_ref[hd] = g[k0:k0 + DK] * s_prev + _dot(kdt[k0:k0 + DK], v_h)
            return carry

        lax.fori_loop(0, n_pairs, pair_body, 0)

    direction(laf_ref, latf_ref, sf_ref, True)
    direction(lab_ref, latb_ref, sb_ref, False)

    for t0 in range(0, L, rb):
        r = qvr_ref[0, t0:t0 + rb, GLA_QK + GLA_VW:GLA_QK + 2 * GLA_VW].astype(F32)
        for hd in range(H):
            v0 = hd * DV
            o = _rms_scale(acc_ref[t0:t0 + rb, v0:v0 + DV])
            y = o * nw_ref[:, v0:v0 + DV] * _silu(r[:, v0:v0 + DV])
            o_ref[0, t0:t0 + rb, v0:v0 + DV] = y.astype(o_ref.dtype)


def gla_call(qvr, kt, a, at, s_f, s_b, wup_f, bias_f, wup_b, bias_b,
             wupt_f, biast_f, wupt_b, biast_b, norm_w, *, rb=256):
    B, L, _ = qvr.shape
    lo, up = _tri_mats(PAIR, GLA_CHUNK)
    const2 = lambda b: (0, 0)
    st_spec = pl.BlockSpec((1, GLA_HEADS, GLA_DK, GLA_DV), lambda b: (b, 0, 0, 0))
    return pl.pallas_call(
        functools.partial(_gla_kernel, L=L, rb=rb),
        grid=(B,),
        in_specs=[pl.BlockSpec((1, L, GLA_QK + 2 * GLA_VW), lambda b: (b, 0, 0)),
                  pl.BlockSpec((1, L // PAIR, GLA_QK, PAIR), lambda b: (b, 0, 0, 0)),
                  pl.BlockSpec((1, L, LANES), lambda b: (b, 0, 0)),
                  pl.BlockSpec((1, LANES, L), lambda b: (b, 0, 0)),
                  st_spec, st_spec,
                  pl.BlockSpec((LANES, GLA_QK), const2),
                  pl.BlockSpec((1, GLA_QK), const2),
                  pl.BlockSpec((LANES, GLA_QK), const2),
                  pl.BlockSpec((1, GLA_QK), const2),
                  pl.BlockSpec((GLA_QK, LANES), const2),
                  pl.BlockSpec((GLA_QK, 1), const2),
                  pl.BlockSpec((GLA_QK, LANES), const2),
                  pl.BlockSpec((GLA_QK, 1), const2),
                  pl.BlockSpec((PAIR, PAIR), const2),
                  pl.BlockSpec((PAIR, PAIR), const2),
                  pl.BlockSpec((1, GLA_VW), const2)],
        out_specs=pl.BlockSpec((1, L, GLA_VW), lambda b: (b, 0, 0)),
        out_shape=jax.ShapeDtypeStruct((B, L, GLA_VW), BF16),
        scratch_shapes=[pltpu.VMEM((L, GLA_QK), F32),
                        pltpu.VMEM((L, GLA_QK), F32),
                        pltpu.VMEM((L // PAIR, GLA_QK, PAIR), F32),
                        pltpu.VMEM((L // PAIR, GLA_QK, PAIR), F32),
                        pltpu.VMEM((L, GLA_VW), F32),
                        pltpu.VMEM((GLA_HEADS, GLA_DK, GLA_DV), F32)],
        compiler_params=_cp(("arbitrary",)),
        name="gla",
    )(qvr, kt, a, at, s_f, s_b, wup_f, bias_f, wup_b, bias_b,
      wupt_f, biast_f, wupt_b, biast_b, lo, up, norm_w.reshape(1, GLA_VW))


def _merge_kernel(x_ref, yhy_ref, ygla_ref, gate_ref, g1_ref, whp_ref, wgp_ref, wo_ref,
                  nw_ref, sc_ref, sh_ref, xn_ref, h2_ref):
    D = D_MODEL
    g_hy = _sigmoid(gate_ref[0, :, 0:D].astype(F32))
    g_gla = _sigmoid(gate_ref[0, :, D:2 * D].astype(F32))
    merged = g_hy * _dot(yhy_ref[0], whp_ref[...]) + g_gla * _dot(ygla_ref[0], wgp_ref[...])
    mix = _dot(merged.astype(BF16), wo_ref[...])
    xn = x_ref[0] + g1_ref[0] * mix
    xn_ref[0] = xn
    y = _rms_scale(xn) * nw_ref[...]
    h2_ref[0] = (y * (1.0 + sc_ref[0]) + sh_ref[0]).astype(h2_ref.dtype)


def merge_call(x, y_hy, y_gla, gates, g1, w_hy_proj, w_gla_proj, w_out, norm2_w, sc2, sh2, *, tm=512):
    B, L, D = x.shape
    tok = lambda b, i: (b, i, 0)
    bvec = lambda b, i: (b, 0, 0)
    const = lambda b, i: (0, 0)
    return pl.pallas_call(
        _merge_kernel,
        grid=(B, L // tm),
        in_specs=[pl.BlockSpec((1, tm, D), tok),
                  pl.BlockSpec((1, tm, HY_WIDTH), tok),
                  pl.BlockSpec((1, tm, GLA_VW), tok),
                  pl.BlockSpec((1, tm, 2 * D), tok),
                  pl.BlockSpec((1, 1, D), bvec),
                  pl.BlockSpec((HY_WIDTH, D), const),
                  pl.BlockSpec((GLA_VW, D), const),
                  pl.BlockSpec((D, D), const),
                  pl.BlockSpec((1, D), const),
                  pl.BlockSpec((1, 1, D), bvec),
                  pl.BlockSpec((1, 1, D), bvec)],
        out_specs=(pl.BlockSpec((1, tm, D), tok), pl.BlockSpec((1, tm, D), tok)),
        out_shape=(jax.ShapeDtypeStruct((B, L, D), F32), jax.ShapeDtypeStruct((B, L, D), BF16)),
        compiler_params=_cp(("arbitrary", "arbitrary")),
        name="merge",
    )(x, y_hy, y_gla, gates, g1, w_hy_proj, w_gla_proj, w_out, norm2_w.reshape(1, D), sc2, sh2)


FFN_PAD = GRID_W


def _ffn_act_kernel(h_ref, wg_ref, wup_ref, dw_ref, dwb_ref, o_ref,
                    g0_ref, l0_ref, r0_ref, g1_ref, l1_ref, r1_ref, s_ref, *, L, rb, n_chunks):
    j = pl.program_id(1)
    hc = wup_ref.shape[1]
    mb = 512
    sub = lax.broadcasted_iota(jnp.int32, (8, hc), 0)
    sets = ((g0_ref, l0_ref, r0_ref), (g1_ref, l1_ref, r1_ref))

    def project(dst, chunk, t0):
        gate_ref, gl_ref, gr_ref = dst
        gate_ref[FFN_PAD + t0:FFN_PAD + t0 + mb, :] = _dot(h_ref[0, t0:t0 + mb, :], wg_ref[chunk])
        for r0 in range(FFN_PAD + t0, FFN_PAD + t0 + mb, GRID_W):
            g = gate_ref[r0:r0 + GRID_W, :]
            left = pltpu.roll(g, 1, axis=0)
            right = pltpu.roll(g, GRID_W - 1, axis=0)
            gl_ref[r0:r0 + 8, :] = jnp.where(sub == 0, 0.0, left[0:8])
            gl_ref[r0 + 8:r0 + GRID_W, :] = left[8:]
            gr_ref[r0:r0 + GRID_W - 8, :] = right[:GRID_W - 8]
            gr_ref[r0 + GRID_W - 8:r0 + GRID_W, :] = jnp.where(sub == 7, 0.0, right[GRID_W - 8:])

    @pl.when(j == 0)
    def _():
        zeros = jnp.zeros((FFN_PAD, hc), F32)
        for ref in sets[0] + sets[1]:
            ref[0:FFN_PAD, :] = zeros
            ref[FFN_PAD + L:FFN_PAD + L + FFN_PAD, :] = zeros
        for t0 in range(0, L, mb):
            project(sets[0], 0, t0)

    def step(cur, nxt):
        gate_ref, gl_ref, gr_ref = cur
        nxt_chunk = jnp.minimum(j + 1, n_chunks - 1)
        dw = dw_ref[...]
        for t0 in range(0, L, mb):
            project(nxt, nxt_chunk, t0)
            for r0 in range(t0, t0 + mb, rb):
                conv = jnp.broadcast_to(dwb_ref[...], (rb, hc))
                for dy in range(3):
                    rows = pl.ds(FFN_PAD + r0 + (dy - 1) * GRID_W, rb)
                    conv = (conv + gl_ref[rows, :] * dw[3 * dy:3 * dy + 1]
                            + gate_ref[rows, :] * dw[3 * dy + 1:3 * dy + 2]
                            + gr_ref[rows, :] * dw[3 * dy + 2:3 * dy + 3])
                s_ref[r0:r0 + rb, :] = _silu(conv)
            up = _dot(h_ref[0, t0:t0 + mb, :], wup_ref[...])
            o_ref[0, t0:t0 + mb, :] = (s_ref[t0:t0 + mb, :] * up).astype(o_ref.dtype)

    parity = lax.rem(j, 2)
    pl.when(parity == 0)(lambda: step(sets[0], sets[1]))
    pl.when(parity == 1)(lambda: step(sets[1], sets[0]))


def ffn_act_call(h2, w_gate, w_up, dw, dw_bias, *, hc=256, rb=64):
    B, L, D = h2.shape
    F = w_up.shape[1]
    n_chunks = F // hc
    wg = w_gate.reshape(D, n_chunks, hc).transpose(1, 0, 2)
    return pl.pallas_call(
        functools.partial(_ffn_act_kernel, L=L, rb=rb, n_chunks=n_chunks),
        grid=(B, n_chunks),
        in_specs=[pl.BlockSpec((1, L, D), lambda b, j: (b, 0, 0)),
                  pl.BlockSpec((n_chunks, D, hc), lambda b, j: (0, 0, 0), pipeline_mode=pl.Buffered(1)),
                  pl.BlockSpec((D, hc), lambda b, j: (0, j)),
                  pl.BlockSpec((9, hc), lambda b, j: (0, j)),
                  pl.BlockSpec((1, hc), lambda b, j: (0, j))],
        out_specs=pl.BlockSpec((1, L, hc), lambda b, j: (b, 0, j)),
        out_shape=jax.ShapeDtypeStruct((B, L, F), BF16),
        scratch_shapes=[pltpu.VMEM((L + 2 * FFN_PAD, hc), F32)] * 6 + [pltpu.VMEM((L, hc), F32)],
        compiler_params=_cp(("arbitrary", "arbitrary")),
        name="ffn_act",
    )(h2, wg, w_up, dw, dw_bias.reshape(1, F))


def _ffn_out_kernel(act_ref, xn_ref, g2_ref, wout_ref, fw_ref, o_ref):
    xo = xn_ref[0] + g2_ref[0] * _dot(act_ref[0], wout_ref[...])
    o_ref[0] = _rms_scale(xo) * fw_ref[...]


def ffn_out_call(act, x_new, g2, w_out, final_w, *, tm=512):
    B, L, F = act.shape
    D = w_out.shape[1]
    tok = lambda b, i: (b, i, 0)
    const = lambda b, i: (0, 0)
    return pl.pallas_call(
        _ffn_out_kernel,
        grid=(B, L // tm),
        in_specs=[pl.BlockSpec((1, tm, F), tok),
                  pl.BlockSpec((1, tm, D), tok),
                  pl.BlockSpec((1, 1, D), lambda b, i: (b, 0, 0)),
                  pl.BlockSpec((F, D), const, pipeline_mode=pl.Buffered(1)),
                  pl.BlockSpec((1, D), const)],
        out_specs=pl.BlockSpec((1, tm, D), tok),
        out_shape=jax.ShapeDtypeStruct((B, L, D), F32),
        compiler_params=_cp(("arbitrary", "arbitrary")),
        name="ffn_out",
    )(act, x_new, g2, w_out, final_w.reshape(1, D))


def _pad_rows(w, rows, at=0):
    out = jnp.zeros((rows, w.shape[1]), w.dtype)
    return lax.dynamic_update_slice(out, w, (at, 0))


def kernel(x, c, ctx, c_ctx, w_ada, b_ada, norm1_w, norm2_w, w_in, hy_conv_w, hy_filt_w1, hy_filt_b1,
           hy_filt_w2, hy_filt_b2, hy_filt_w3, hy_filt_freq, hy_decay, hy_bias, gla_a_up_f, gla_a_bias_f,
           gla_a_up_b, gla_a_bias_b, gla_norm_w, w_hy_proj, w_gla_proj, w_out, ffn_w_in, ffn_dw,
           ffn_dw_bias, ffn_w_out, final_norm_w):
    assert w_ada.shape[0] == 1, "single-layer block"
    B, L, D = x.shape
    R = GLA_LOWRANK

    n_rows = -(-(B + 1) // 8) * 8
    c_rows = jnp.zeros((n_rows, D), F32).at[:B].set(c).at[B].set(c_ctx)
    mod = ada_call(c_rows, w_ada[0], b_ada[0])
    sh1, sc1, g1, sh2, sc2, g2 = [m[:B, None, :] for m in jnp.split(mod, 6, axis=-1)]
    csh1, csc1 = mod[B:B + 1, None, 0:D], mod[B:B + 1, None, D:2 * D]

    w = w_in[0]
    o_q = HY_COLS
    o_k = o_q + GLA_QK
    o_v = o_k + GLA_QK
    o_r = o_v + GLA_VW
    o_af = o_r + GLA_VW
    o_ab = o_af + R
    o_g = o_ab + R
    w_main = jnp.concatenate([w[:, :o_k], w[:, o_v:o_af], w[:, o_g:]], axis=1).astype(BF16)
    w_kt = w[:, o_k:o_v].T.astype(BF16)
    w_v = w[:, o_v:o_r].astype(BF16)
    w_a = jnp.pad(w[:, o_af:o_g], ((0, 0), (0, LANES - 2 * R))).astype(BF16)
    w_at = w_a.T
    wup_f = _pad_rows(gla_a_up_f[0], LANES, 0)
    wup_b = _pad_rows(gla_a_up_b[0], LANES, R)
    wupt_f, wupt_b = wup_f.T, wup_b.T
    bias_f, bias_b = gla_a_bias_f[0].reshape(1, GLA_QK), gla_a_bias_b[0].reshape(1, GLA_QK)
    biast_f, biast_b = bias_f.reshape(GLA_QK, 1), bias_b.reshape(GLA_QK, 1)

    cos32, sin32 = dft_tables(L)
    a_co, b_co, knyq = filter_call(L, hy_filt_w1[0], hy_filt_b1[0], hy_filt_w2[0], hy_filt_b2[0],
                                   hy_filt_w3[0], hy_filt_freq[0], hy_decay[0], cos32, sin32)

    s_f, s_b = ctx_call(ctx, norm1_w[0], csc1, csh1, w_kt, w_v, w_at, wupt_f, biast_f, wupt_b, biast_b)

    u_hy, qvr, gates, kt, a, at = inproj_call(x, norm1_w[0], sc1, sh1, w_main, w_kt, w_a, w_at)
    y_hy = hyena_call(u_hy, hy_conv_w[0], hy_bias[0], knyq, cos32.astype(BF16), sin32.astype(BF16), a_co, b_co)
    y_gla = gla_call(qvr, kt, a, at, s_f, s_b, wup_f, bias_f, wup_b, bias_b,
                     wupt_f, biast_f, wupt_b, biast_b, gla_norm_w[0])
    x_new, h2 = merge_call(x, y_hy, y_gla, gates, g1, w_hy_proj[0].astype(BF16), w_gla_proj[0].astype(BF16),
                           w_out[0].astype(BF16), norm2_w[0], sc2, sh2)
    w_ffn = ffn_w_in[0]
    act = ffn_act_call(h2, w_ffn[:, FFN_HIDDEN:].astype(BF16), w_ffn[:, :FFN_HIDDEN].astype(BF16),
                       ffn_dw[0].reshape(9, FFN_HIDDEN), ffn_dw_bias[0])
    return ffn_out_call(act, x_new, g2, ffn_w_out[0].astype(BF16), final_norm_w)
```

```python
import functools
import math

import jax
import jax.numpy as jnp
import numpy as np
from jax import lax
from jax.experimental import pallas as pl
from jax.experimental.pallas import tpu as pltpu

F32 = jnp.float32
BF16 = jnp.bfloat16
HIGHEST = lax.Precision.HIGHEST

D_MODEL = 1024
GRID_W = 64
HY_WIDTH = 512
HY_BANDS = 16
HY_EMB = 1 + 2 * HY_BANDS
HY_FILT_HIDDEN = 64
GLA_HEADS = 4
GLA_DK = 64
GLA_DV = 128
GLA_LOWRANK = 16
GLA_TAU = 16.0
GLA_CHUNK = 64
FFN_HIDDEN = 2816
EPS = 1e-6
HY_COLS = 3 * HY_WIDTH
GLA_QK = GLA_HEADS * GLA_DK
GLA_VW = GLA_HEADS * GLA_DV

LANES = 128
PAIR = 2 * GLA_CHUNK
VMEM_LIMIT = 56 * 1024 * 1024


def _cp(sem):
    return pltpu.CompilerParams(dimension_semantics=sem, vmem_limit_bytes=VMEM_LIMIT)


def _dot(a, b):
    return jnp.dot(a, b, preferred_element_type=F32)


def _dot_hi(a, b):
    return jnp.dot(a, b, preferred_element_type=F32, precision=HIGHEST)


def _sigmoid(x):
    return 1.0 / (1.0 + jnp.exp(-x))


def _silu(x):
    return x * _sigmoid(x)


def _log_sigmoid(x):
    return jnp.minimum(x, 0.0) - jnp.log(1.0 + jnp.exp(-jnp.abs(x)))


def _rms_scale(xf):
    return xf * lax.rsqrt(jnp.mean(xf * xf, axis=-1, keepdims=True) + EPS)


def _ada_kernel(c_ref, w_ref, b_ref, o_ref):
    o_ref[...] = _dot_hi(_silu(c_ref[...]), w_ref[...]) + b_ref[...]


def ada_call(c_rows, w_ada, b_ada):
    rows, d = c_rows.shape
    n = w_ada.shape[1]
    bn = 512
    return pl.pallas_call(
        _ada_kernel,
        grid=(n // bn,),
        in_specs=[pl.BlockSpec((rows, d), lambda j: (0, 0)),
                  pl.BlockSpec((d, bn), lambda j: (0, j)),
                  pl.BlockSpec((1, bn), lambda j: (0, j))],
        out_specs=pl.BlockSpec((rows, bn), lambda j: (0, j)),
        out_shape=jax.ShapeDtypeStruct((rows, n), F32),
        compiler_params=_cp(("arbitrary",)),
        name="ada",
    )(c_rows, w_ada, b_ada.reshape(1, n))


HY_BLOCK = 512


def _filt_mlp_kernel(feats_ref, t_ref, w1_ref, b1_ref, w2_ref, b2_ref, w3_ref, freq_ref, decay_ref,
                     kf_ref, kb_ref):
    freq = freq_ref[...]
    h = jnp.sin(freq * (_dot_hi(feats_ref[...], w1_ref[...]) + b1_ref[...]))
    h = jnp.sin(freq * (_dot_hi(h, w2_ref[...]) + b2_ref[...]))
    k = _dot_hi(h, w3_ref[...]) * jnp.exp(-t_ref[...] * decay_ref[...])
    L = k.shape[0]
    row = lax.broadcasted_iota(jnp.int32, (L, HY_WIDTH), 0)
    k_f = k[:, :HY_WIDTH]
    k_b = jnp.where(row == 0, 0.0, k[:, HY_WIDTH:])
    norm = jnp.sum(jnp.abs(k_f), axis=0, keepdims=True) + jnp.sum(jnp.abs(k_b), axis=0, keepdims=True)
    kf_ref[...] = k_f / norm
    kb_ref[...] = k_b / norm


def _filt_dft_kernel(c_ref, s_ref, hp_ref, hm_ref, a_ref, b_ref, gn_ref, *, M):
    row = lax.broadcasted_iota(jnp.int32, (M, HY_WIDTH), 0)
    hp = hp_ref[...]
    hm = jnp.where(row == 0, 0.0, hm_ref[...])
    f = lax.broadcasted_iota(jnp.int32, (M, 1), 0)
    w = jnp.where(f == 0, 0.5 / M, 1.0 / M)
    a_ref[0] = w * _dot_hi(c_ref[...], hp + hm)
    b_ref[0] = w * _dot_hi(s_ref[...], hm - hp)
    sign = jnp.where((row & 1) == 0, 1.0, -1.0)
    gn = jnp.sum(sign * (hp + hm), axis=0, keepdims=True) * (0.5 / M)
    gn_ref[0] = jnp.broadcast_to(gn, gn_ref.shape[1:])


def filter_call(L, hy_filt_w1, hy_filt_b1, hy_filt_w2, hy_filt_b2, hy_filt_w3, hy_filt_freq, hy_decay,
                cos32, sin32):
    M = HY_BLOCK
    P = L // M
    t = jnp.linspace(0.0, 1.0, L, dtype=F32)[:, None]
    w = (2.0 * np.pi / L) * jnp.arange(L, dtype=F32)[:, None]
    f = jnp.linspace(1e-4, HY_BANDS - 1, HY_BANDS, dtype=F32)[None, :]
    feats = jnp.concatenate([t, jnp.cos(f * w), -jnp.sin(f * w)], axis=-1)
    feats = jnp.pad(feats, ((0, 0), (0, LANES - HY_EMB)))
    w1 = jnp.pad(hy_filt_w1, ((0, LANES - HY_EMB), (0, 0)))
    hid = HY_FILT_HIDDEN
    k_f, k_b = pl.pallas_call(
        _filt_mlp_kernel,
        out_shape=(jax.ShapeDtypeStruct((L, HY_WIDTH), F32),
                   jax.ShapeDtypeStruct((L, HY_WIDTH), F32)),
        compiler_params=pltpu.CompilerParams(vmem_limit_bytes=VMEM_LIMIT),
        name="filt_mlp",
    )(feats, t, w1, hy_filt_b1.reshape(1, hid), hy_filt_w2, hy_filt_b2.reshape(1, hid), hy_filt_w3,
      hy_filt_freq.reshape(1, hid), hy_decay.reshape(1, 2 * HY_WIDTH))
    zero = jnp.zeros((1, HY_WIDTH), F32)
    h_lag = jnp.concatenate([zero, jnp.flip(k_b[1:], axis=0), k_f], axis=0)
    h_neg = jnp.concatenate([zero, jnp.flip(h_lag[1:], axis=0)], axis=0)
    n_seg = 2 * P - 1
    seg = jax.ShapeDtypeStruct((n_seg, M, HY_WIDTH), F32)
    seg_spec = pl.BlockSpec((1, M, HY_WIDTH), lambda g: (g, 0, 0))
    a_co, b_co, g_nyq = pl.pallas_call(
        functools.partial(_filt_dft_kernel, M=M),
        grid=(n_seg,),
        in_specs=[pl.BlockSpec((M, M), lambda g: (0, 0)),
                  pl.BlockSpec((M, M), lambda g: (0, 0)),
                  pl.BlockSpec((M, HY_WIDTH), lambda g: (g + 1, 0)),
                  pl.BlockSpec((M, HY_WIDTH), lambda g: (2 * P - 1 - g, 0))],
        out_specs=(seg_spec, seg_spec, pl.BlockSpec((1, 8, HY_WIDTH), lambda g: (g, 0, 0))),
        out_shape=(seg, seg, jax.ShapeDtypeStruct((n_seg, 8, HY_WIDTH), F32)),
        compiler_params=_cp(("arbitrary",)),
        name="filt_dft",
    )(cos32, sin32, h_lag, h_neg)
    return a_co, b_co, g_nyq


def dft_tables(M):
    n_fft = 2 * M
    idx = (jnp.arange(M, dtype=jnp.int32)[:, None] * jnp.arange(M, dtype=jnp.int32)[None, :]) % n_fft
    ang = idx.astype(F32) * (2.0 * np.pi / n_fft)
    return jnp.cos(ang), jnp.sin(ang)


def _modulated_norm(x_ref, nw_ref, sc_ref, sh_ref):
    y = _rms_scale(x_ref[0].astype(F32)) * nw_ref[...]
    return (y * (1.0 + sc_ref[0]) + sh_ref[0]).astype(BF16)


def _inproj_kernel(x_ref, nw_ref, sc_ref, sh_ref, w_ref, wkt_ref, wa_ref, wat_ref,
                   hy_ref, qvr_ref, gate_ref, kt_ref, a_ref, at_ref, *, tm):
    h = _modulated_norm(x_ref, nw_ref, sc_ref, sh_ref)
    col = 0
    for o_ref in (hy_ref, qvr_ref, gate_ref):
        width = o_ref.shape[-1]
        done = 0
        while done < width:
            step = min(512, width - done)
            o_ref[0, :, done:done + step] = _dot(h, w_ref[:, col:col + step]).astype(o_ref.dtype)
            done += step
            col += step
    nt = (((1,), (1,)), ((), ()))
    kt = lax.dot_general(wkt_ref[...], h, nt, preferred_element_type=F32).astype(BF16)
    for p in range(tm // PAIR):
        kt_ref[0, p] = kt[:, p * PAIR:(p + 1) * PAIR]
    a_ref[0] = _dot(h, wa_ref[...])
    at_ref[0] = lax.dot_general(wat_ref[...], h, nt, preferred_element_type=F32)


def inproj_call(x, norm_w, sc, sh, w_main, w_kt, w_a, w_at, *, tm=512):
    B, L, D = x.shape
    n_main = w_main.shape[1]
    const = lambda b, i: (0, 0)
    tok = lambda b, i: (b, i, 0)
    bvec = lambda b, i: (b, 0, 0)
    return pl.pallas_call(
        functools.partial(_inproj_kernel, tm=tm),
        grid=(B, L // tm),
        in_specs=[pl.BlockSpec((1, tm, D), tok),
                  pl.BlockSpec((1, D), const),
                  pl.BlockSpec((1, 1, D), bvec),
                  pl.BlockSpec((1, 1, D), bvec),
                  pl.BlockSpec((D, n_main), const, pipeline_mode=pl.Buffered(1)),
                  pl.BlockSpec((GLA_QK, D), const),
                  pl.BlockSpec((D, LANES), const),
                  pl.BlockSpec((LANES, D), const)],
        out_specs=(pl.BlockSpec((1, tm, HY_COLS), tok),
                   pl.BlockSpec((1, tm, GLA_QK + 2 * GLA_VW), tok),
                   pl.BlockSpec((1, tm, 2 * D), tok),
                   pl.BlockSpec((1, tm // PAIR, GLA_QK, PAIR), lambda b, i: (b, i, 0, 0)),
                   pl.BlockSpec((1, tm, LANES), tok),
                   pl.BlockSpec((1, LANES, tm), lambda b, i: (b, 0, i))),
        out_shape=(jax.ShapeDtypeStruct((B, L, HY_COLS), BF16),
                   jax.ShapeDtypeStruct((B, L, GLA_QK + 2 * GLA_VW), BF16),
                   jax.ShapeDtypeStruct((B, L, 2 * D), BF16),
                   jax.ShapeDtypeStruct((B, L // PAIR, GLA_QK, PAIR), BF16),
                   jax.ShapeDtypeStruct((B, L, LANES), F32),
                   jax.ShapeDtypeStruct((B, LANES, L), F32)),
        compiler_params=_cp(("arbitrary", "arbitrary")),
        name="inproj",
    )(x, norm_w.reshape(1, D), sc, sh, w_main, w_kt, w_a, w_at)


def _short_conv_rows(u_ref, w_ref, t0, rows, col0, ncol, L):
    halo = 16
    lo = max(t0 - halo, 0)
    hi = min(t0 + rows + halo, L)
    n = hi - lo
    blk = u_ref[0, lo:hi, col0:col0 + ncol].astype(F32)
    ridx = lax.broadcasted_iota(jnp.int32, (n, 1), 0)
    prev = pltpu.roll(blk, 1, axis=0)
    nxt = pltpu.roll(blk, n - 1, axis=0)
    if lo == 0:
        prev = jnp.where(ridx == 0, 0.0, prev)
    if hi == L:
        nxt = jnp.where(ridx == n - 1, 0.0, nxt)
    w = w_ref[:, col0:col0 + ncol]
    out = prev * w[0:1] + blk * w[1:2] + nxt * w[2:3]
    return out[t0 - lo:t0 - lo + rows]


def _hyena_kernel(u_ref, cw_ref, bias_ref, fwd_ref, inv_ref, a_ref, b_ref, gn_ref, o_ref,
                  zb_ref, x0_ref, zp_ref, uv_ref, y_ref, *, L, M, rb, fb):
    W = HY_WIDTH
    P = L // M

    def sign_rows(n):
        t = lax.broadcasted_iota(jnp.int32, (n, 1), 0)
        return jnp.where((t & 1) == 0, 1.0, -1.0)

    znyq = [jnp.zeros((1, W), F32) for _ in range(P)]
    for t0 in range(0, L, rb):
        x1 = _short_conv_rows(u_ref, cw_ref, t0, rb, W, W, L)
        v = _short_conv_rows(u_ref, cw_ref, t0, rb, 2 * W, W, L)
        z = x1 * v
        zb_ref[t0:t0 + rb, :] = z.astype(BF16)
        x0_ref[t0:t0 + rb, :] = _short_conv_rows(u_ref, cw_ref, t0, rb, 0, W, L).astype(BF16)
        znyq[t0 // M] = znyq[t0 // M] + jnp.sum(sign_rows(rb) * z, axis=0, keepdims=True)

    for j in range(P):
        zp_ref[j] = _dot(fwd_ref[...], zb_ref[j * M:(j + 1) * M, :])

    bias = bias_ref[...]
    for i in range(P):
        for f0 in range(0, M, fb):
            u = jnp.zeros((fb, W), F32)
            v = jnp.zeros((fb, W), F32)
            for j in range(P):
                d = i - j + P - 1
                zre = zp_ref[j, f0:f0 + fb, :]
                p = zp_ref[j, M + f0:M + f0 + fb, :]
                a = a_ref[d, f0:f0 + fb, :]
                b = b_ref[d, f0:f0 + fb, :]
                u = u + zre * a + p * b
                v = v + p * a - zre * b
            uv_ref[f0:f0 + fb, :] = u.astype(BF16)
            uv_ref[M + f0:M + f0 + fb, :] = v.astype(BF16)
        y_ref[...] = _dot(inv_ref[...], uv_ref[...])
        ynyq = jnp.zeros((1, W), F32)
        for j in range(P):
            ynyq = ynyq + znyq[j] * gn_ref[i - j + P - 1, 0:1, :]
        for r0 in range(0, M, rb):
            rows = slice(i * M + r0, i * M + r0 + rb)
            z = zb_ref[rows, :].astype(F32)
            y = y_ref[r0:r0 + rb, :] + sign_rows(rb) * ynyq + bias * z
            o_ref[0, rows, :] = (x0_ref[rows, :].astype(F32) * y).astype(o_ref.dtype)


def hyena_call(u_hy, conv_w, hy_bias, fwd_tab, inv_tab, a_co, b_co, g_nyq, *, rb=256, fb=16):
    B, L, _ = u_hy.shape
    W = HY_WIDTH
    M = HY_BLOCK
    n_seg = a_co.shape[0]
    const2 = lambda b: (0, 0)
    const3 = lambda b: (0, 0, 0)
    return pl.pallas_call(
        functools.partial(_hyena_kernel, L=L, M=M, rb=rb, fb=fb),
        grid=(B,),
        in_specs=[pl.BlockSpec((1, L, HY_COLS), lambda b: (b, 0, 0)),
                  pl.BlockSpec((3, HY_COLS), const2),
                  pl.BlockSpec((1, W), const2),
                  pl.BlockSpec((2 * M, M), const2),
                  pl.BlockSpec((M, 2 * M), const2),
                  pl.BlockSpec((n_seg, M, W), const3, pipeline_mode=pl.Buffered(1)),
                  pl.BlockSpec((n_seg, M, W), const3, pipeline_mode=pl.Buffered(1)),
                  pl.BlockSpec((n_seg, 8, W), const3)],
        out_specs=pl.BlockSpec((1, L, W), lambda b: (b, 0, 0)),
        out_shape=jax.ShapeDtypeStruct((B, L, W), BF16),
        scratch_shapes=[pltpu.VMEM((L, W), BF16),
                        pltpu.VMEM((L, W), BF16),
                        pltpu.VMEM((L // M, 2 * M, W), F32),
                        pltpu.VMEM((2 * M, W), BF16),
                        pltpu.VMEM((M, W), F32)],
        compiler_params=_cp(("arbitrary",)),
        name="hyena",
    )(u_hy, conv_w, hy_bias.reshape(1, W), fwd_tab, inv_tab, a_co, b_co, g_nyq)


def _split3(x):
    x1 = x.astype(BF16)
    r = x - x1.astype(F32)
    x2 = r.astype(BF16)
    x3 = (r - x2.astype(F32)).astype(BF16)
    return x1, x2, x3


def _tri_left(m, x):
    x1, x2, x3 = _split3(x)
    return _dot(m, x1) + _dot(m, x2) + _dot(m, x3)


def _tri_right(x, m):
    x1, x2, x3 = _split3(x)
    return _dot(x1, m) + _dot(x2, m) + _dot(x3, m)


def _tri_mats(n, block):
    r = np.arange(n)
    same = (r[:, None] // block) == (r[None, :] // block)
    lower = same & (r[:, None] >= r[None, :])
    upper = same & (r[:, None] <= r[None, :])
    return jnp.asarray(lower, BF16), jnp.asarray(upper, BF16)


def _log_decay_tok(a, wup_ref, bias_ref):
    return _log_sigmoid(_dot_hi(a, wup_ref[...]) + bias_ref[...]) * (1.0 / GLA_TAU)


def _log_decay_feat(at, wupt_ref, biast_ref):
    return _log_sigmoid(_dot_hi(wupt_ref[...], at) + biast_ref[...]) * (1.0 / GLA_TAU)


def _ctx_kernel(x_ref, nw_ref, sc_ref, sh_ref, wkt_ref, wv_ref, wat_ref,
                wuptf_ref, biastf_ref, wuptb_ref, biastb_ref, lo_ref, up_ref, sf_ref, sb_ref):
    h = _modulated_norm(x_ref, nw_ref, sc_ref, sh_ref)
    nt = (((1,), (1,)), ((), ()))
    kt = lax.dot_general(wkt_ref[...], h, nt, preferred_element_type=F32)
    v = _dot(h, wv_ref[...]).astype(BF16)
    at = lax.dot_general(wat_ref[...], h, nt, preferred_element_type=F32)
    la_f = _log_decay_feat(at, wuptf_ref, biastf_ref)
    la_b = _log_decay_feat(at, wuptb_ref, biastb_ref)
    lo = lo_ref[...]
    up = up_ref[...]
    w_f = jnp.exp(_tri_right(la_f, lo) - la_f)
    w_b = jnp.exp(_tri_right(la_b, up) - la_b)
    kf = (kt * w_f).astype(BF16)
    kb = (kt * w_b).astype(BF16)
    for hd in range(GLA_HEADS):
        r0, c0 = hd * GLA_DK, hd * GLA_DV
        sf_ref[0, hd] = _dot(kf[r0:r0 + GLA_DK], v[:, c0:c0 + GLA_DV])
        sb_ref[0, hd] = _dot(kb[r0:r0 + GLA_DK], v[:, c0:c0 + GLA_DV])


def ctx_call(ctx, norm_w, csc, csh, w_kt, w_v, w_at, wupt_f, biast_f, wupt_b, biast_b):
    B, Lc, D = ctx.shape
    lo, up = _tri_mats(Lc, Lc)
    const2 = lambda b: (0, 0)
    const3 = lambda b: (0, 0, 0)
    st = jax.ShapeDtypeStruct((B, GLA_HEADS, GLA_DK, GLA_DV), F32)
    st_spec = pl.BlockSpec((1, GLA_HEADS, GLA_DK, GLA_DV), lambda b: (b, 0, 0, 0))
    return pl.pallas_call(
        _ctx_kernel,
        grid=(B,),
        in_specs=[pl.BlockSpec((1, Lc, D), lambda b: (b, 0, 0)),
                  pl.BlockSpec((1, D), const2),
                  pl.BlockSpec((1, 1, D), const3),
                  pl.BlockSpec((1, 1, D), const3),
                  pl.BlockSpec((GLA_QK, D), const2),
                  pl.BlockSpec((D, GLA_VW), const2),
                  pl.BlockSpec((LANES, D), const2),
                  pl.BlockSpec((GLA_QK, LANES), const2),
                  pl.BlockSpec((GLA_QK, 1), const2),
                  pl.BlockSpec((GLA_QK, LANES), const2),
                  pl.BlockSpec((GLA_QK, 1), const2),
                  pl.BlockSpec((Lc, Lc), const2),
                  pl.BlockSpec((Lc, Lc), const2)],
        out_specs=(st_spec, st_spec),
        out_shape=(st, st),
        compiler_params=_cp(("arbitrary",)),
        name="ctx",
    )(ctx, norm_w.reshape(1, D), csc, csh, w_kt, w_v, w_at, wupt_f, biast_f, wupt_b, biast_b, lo, up)


def _gla_kernel(qvr_ref, kt_ref, a_ref, at_ref, sf_ref, sb_ref,
                wupf_ref, biasf_ref, wupb_ref, biasb_ref,
                wuptf_ref, biastf_ref, wuptb_ref, biastb_ref,
                lo_ref, up_ref, nw_ref, o_ref,
                laf_ref, lab_ref, latf_ref, latb_ref, acc_ref, st_ref, *, L, rb):
    n_pairs = L // PAIR
    C = GLA_CHUNK
    H, DK, DV = GLA_HEADS, GLA_DK, GLA_DV
    scale = DK ** -0.5

    for t0 in range(0, L, rb):
        a = a_ref[0, t0:t0 + rb, :]
        laf_ref[t0:t0 + rb, :] = _log_decay_tok(a, wupf_ref, biasf_ref)
        lab_ref[t0:t0 + rb, :] = _log_decay_tok(a, wupb_ref, biasb_ref)
    for p in range(n_pairs):
        at = at_ref[0, :, p * PAIR:(p + 1) * PAIR]
        latf_ref[p] = _log_decay_feat(at, wuptf_ref, biastf_ref)
        latb_ref[p] = _log_decay_feat(at, wuptb_ref, biastb_ref)

    lo = lo_ref[...]
    up = up_ref[...]
    rr = lax.broadcasted_iota(jnp.int32, (C, C), 0)
    cc = lax.broadcasted_iota(jnp.int32, (C, C), 1)

    def direction(la_ref, lat_ref, s0_ref, forward):
        m_tok = lo if forward else up
        m_feat = up if forward else lo
        keep = (rr >= cc) if forward else (rr <= cc)
        last = C - 1 if forward else 0
        for hd in range(H):
            st_ref[hd] = s0_ref[0, hd]

        def pair_body(i, carry):
            p = i if forward else n_pairs - 1 - i
            t0 = pl.multiple_of(p * PAIR, PAIR)
            b_tok = _tri_left(m_tok, la_ref[pl.ds(t0, PAIR), :])
            b_feat = _tri_right(lat_ref[p], m_feat)
            q = qvr_ref[0, pl.ds(t0, PAIR), 0:GLA_QK].astype(F32)
            qe = (q * scale * jnp.exp(b_tok)).astype(BF16)
            kt = kt_ref[0, p].astype(F32)
            ket = (kt * jnp.exp(-b_feat)).astype(BF16)
            v = qvr_ref[0, pl.ds(t0, PAIR), GLA_QK:GLA_QK + GLA_VW]
            for half in ((0, 1) if forward else (1, 0)):
                r0 = half * C
                b_last = b_feat[:, r0 + last:r0 + last + 1]
                kdt = (kt[:, r0:r0 + C] * jnp.exp(b_last - b_feat[:, r0:r0 + C])).astype(BF16)
                g = jnp.exp(b_last)
                for hd in range(H):
                    k0, v0 = hd * DK, hd * DV
                    qe_h = qe[r0:r0 + C, k0:k0 + DK]
                    v_h = v[r0:r0 + C, v0:v0 + DV]
                    s_prev = st_ref[hd]
                    sc = _dot(qe_h, ket[k0:k0 + DK, r0:r0 + C])
                    sc = jnp.where(keep, sc, 0.0).astype(BF16)
                    out = _dot(sc, v_h) + _dot(qe_h, s_prev.astype(BF16))
                    rows = pl.ds(t0 + r0, C)
                    if forward:
                        acc_ref[rows, v0:v0 + DV] = out
                    else:
                        acc_ref[rows, v0:v0 + DV] += out
                    st_ref[hd] = g[k0:k0 + DK] * s_prev + _dot(kdt[k0:k0 + DK], v_h)
            return carry

        lax.fori_loop(0, n_pairs, pair_body, 0)

    direction(laf_ref, latf_ref, sf_ref, True)
    direction(lab_ref, latb_ref, sb_ref, False)

    for t0 in range(0, L, rb):
        r = qvr_ref[0, t0:t0 + rb, GLA_QK + GLA_VW:GLA_QK + 2 * GLA_VW].astype(F32)
        for hd in range(H):
            v0 = hd * DV
            o = _rms_scale(acc_ref[t0:t0 + rb, v0:v0 + DV])
            y = o * nw_ref[:, v0:v0 + DV] * _silu(r[:, v0:v0 + DV])
            o_ref[0, t0:t0 + rb, v0:v0 + DV] = y.astype(o_ref.dtype)


def gla_call(qvr, kt, a, at, s_f, s_b, wup_f, bias_f, wup_b, bias_b,
             wupt_f, biast_f, wupt_b, biast_b, norm_w, *, rb=256):
    B, L, _ = qvr.shape
    lo, up = _tri_mats(PAIR, GLA_CHUNK)
    const2 = lambda b: (0, 0)
    st_spec = pl.BlockSpec((1, GLA_HEADS, GLA_DK, GLA_DV), lambda b: (b, 0, 0, 0))
    return pl.pallas_call(
        functools.partial(_gla_kernel, L=L, rb=rb),
        grid=(B,),
        in_specs=[pl.BlockSpec((1, L, GLA_QK + 2 * GLA_VW), lambda b: (b, 0, 0)),
                  pl.BlockSpec((1, L // PAIR, GLA_QK, PAIR), lambda b: (b, 0, 0, 0)),
                  pl.BlockSpec((1, L, LANES), lambda b: (b, 0, 0)),
                  pl.BlockSpec((1, LANES, L), lambda b: (b, 0, 0)),
                  st_spec, st_spec,
                  pl.BlockSpec((LANES, GLA_QK), const2),
                  pl.BlockSpec((1, GLA_QK), const2),
                  pl.BlockSpec((LANES, GLA_QK), const2),
                  pl.BlockSpec((1, GLA_QK), const2),
                  pl.BlockSpec((GLA_QK, LANES), const2),
                  pl.BlockSpec((GLA_QK, 1), const2),
                  pl.BlockSpec((GLA_QK, LANES), const2),
                  pl.BlockSpec((GLA_QK, 1), const2),
                  pl.BlockSpec((PAIR, PAIR), const2),
                  pl.BlockSpec((PAIR, PAIR), const2),
                  pl.BlockSpec((1, GLA_VW), const2)],
        out_specs=pl.BlockSpec((1, L, GLA_VW), lambda b: (b, 0, 0)),
        out_shape=jax.ShapeDtypeStruct((B, L, GLA_VW), BF16),
        scratch_shapes=[pltpu.VMEM((L, GLA_QK), F32),
                        pltpu.VMEM((L, GLA_QK), F32),
                        pltpu.VMEM((L // PAIR, GLA_QK, PAIR), F32),
                        pltpu.VMEM((L // PAIR, GLA_QK, PAIR), F32),
                        pltpu.VMEM((L, GLA_VW), F32),
                        pltpu.VMEM((GLA_HEADS, GLA_DK, GLA_DV), F32)],
        compiler_params=_cp(("arbitrary",)),
        name="gla",
    )(qvr, kt, a, at, s_f, s_b, wup_f, bias_f, wup_b, bias_b,
      wupt_f, biast_f, wupt_b, biast_b, lo, up, norm_w.reshape(1, GLA_VW))


def _merge_kernel(x_ref, yhy_ref, ygla_ref, gate_ref, g1_ref, whp_ref, wgp_ref, wo_ref,
                  nw_ref, sc_ref, sh_ref, xn_ref, h2_ref):
    D = D_MODEL
    g_hy = _sigmoid(gate_ref[0, :, 0:D].astype(F32))
    g_gla = _sigmoid(gate_ref[0, :, D:2 * D].astype(F32))
    merged = g_hy * _dot(yhy_ref[0], whp_ref[...]) + g_gla * _dot(ygla_ref[0], wgp_ref[...])
    mix = _dot(merged.astype(BF16), wo_ref[...])
    xn = x_ref[0] + g1_ref[0] * mix
    xn_ref[0] = xn
    y = _rms_scale(xn) * nw_ref[...]
    h2_ref[0] = (y * (1.0 + sc_ref[0]) + sh_ref[0]).astype(h2_ref.dtype)


def merge_call(x, y_hy, y_gla, gates, g1, w_hy_proj, w_gla_proj, w_out, norm2_w, sc2, sh2, *, tm=512):
    B, L, D = x.shape
    tok = lambda b, i: (b, i, 0)
    bvec = lambda b, i: (b, 0, 0)
    const = lambda b, i: (0, 0)
    return pl.pallas_call(
        _merge_kernel,
        grid=(B, L // tm),
        in_specs=[pl.BlockSpec((1, tm, D), tok),
                  pl.BlockSpec((1, tm, HY_WIDTH), tok),
                  pl.BlockSpec((1, tm, GLA_VW), tok),
                  pl.BlockSpec((1, tm, 2 * D), tok),
                  pl.BlockSpec((1, 1, D), bvec),
                  pl.BlockSpec((HY_WIDTH, D), const),
                  pl.BlockSpec((GLA_VW, D), const),
                  pl.BlockSpec((D, D), const),
                  pl.BlockSpec((1, D), const),
                  pl.BlockSpec((1, 1, D), bvec),
                  pl.BlockSpec((1, 1, D), bvec)],
        out_specs=(pl.BlockSpec((1, tm, D), tok), pl.BlockSpec((1, tm, D), tok)),
        out_shape=(jax.ShapeDtypeStruct((B, L, D), F32), jax.ShapeDtypeStruct((B, L, D), BF16)),
        compiler_params=_cp(("arbitrary", "arbitrary")),
        name="merge",
    )(x, y_hy, y_gla, gates, g1, w_hy_proj, w_gla_proj, w_out, norm2_w.reshape(1, D), sc2, sh2)


FFN_PAD = GRID_W


def _ffn_act_kernel(h_ref, wg_ref, wup_ref, dw_ref, dwb_ref, o_ref,
                    g0_ref, l0_ref, r0_ref, g1_ref, l1_ref, r1_ref, s_ref, *, L, rb, n_chunks):
    j = pl.program_id(1)
    hc = wup_ref.shape[1]
    mb = 512
    sub = lax.broadcasted_iota(jnp.int32, (8, hc), 0)
    sets = ((g0_ref, l0_ref, r0_ref), (g1_ref, l1_ref, r1_ref))

    def project(dst, chunk, t0):
        gate_ref, gl_ref, gr_ref = dst
        gate_ref[FFN_PAD + t0:FFN_PAD + t0 + mb, :] = _dot(h_ref[0, t0:t0 + mb, :], wg_ref[chunk])
        for r0 in range(FFN_PAD + t0, FFN_PAD + t0 + mb, GRID_W):
            g = gate_ref[r0:r0 + GRID_W, :]
            left = pltpu.roll(g, 1, axis=0)
            right = pltpu.roll(g, GRID_W - 1, axis=0)
            gl_ref[r0:r0 + 8, :] = jnp.where(sub == 0, 0.0, left[0:8])
            gl_ref[r0 + 8:r0 + GRID_W, :] = left[8:]
            gr_ref[r0:r0 + GRID_W - 8, :] = right[:GRID_W - 8]
            gr_ref[r0 + GRID_W - 8:r0 + GRID_W, :] = jnp.where(sub == 7, 0.0, right[GRID_W - 8:])

    @pl.when(j == 0)
    def _():
        zeros = jnp.zeros((FFN_PAD, hc), F32)
        for ref in sets[0] + sets[1]:
            ref[0:FFN_PAD, :] = zeros
            ref[FFN_PAD + L:FFN_PAD + L + FFN_PAD, :] = zeros
        for t0 in range(0, L, mb):
            project(sets[0], 0, t0)

    def step(cur, nxt):
        gate_ref, gl_ref, gr_ref = cur
        nxt_chunk = jnp.minimum(j + 1, n_chunks - 1)
        dw = dw_ref[...]
        for t0 in range(0, L, mb):
            project(nxt, nxt_chunk, t0)
            for r0 in range(t0, t0 + mb, rb):
                conv = jnp.broadcast_to(dwb_ref[...], (rb, hc))
                for dy in range(3):
                    rows = pl.ds(FFN_PAD + r0 + (dy - 1) * GRID_W, rb)
                    conv = (conv + gl_ref[rows, :] * dw[3 * dy:3 * dy + 1]
                            + gate_ref[rows, :] * dw[3 * dy + 1:3 * dy + 2]
                            + gr_ref[rows, :] * dw[3 * dy + 2:3 * dy + 3])
                s_ref[r0:r0 + rb, :] = _silu(conv)
            up = _dot(h_ref[0, t0:t0 + mb, :], wup_ref[...])
            o_ref[0, t0:t0 + mb, :] = (s_ref[t0:t0 + mb, :] * up).astype(o_ref.dtype)

    parity = lax.rem(j, 2)
    pl.when(parity == 0)(lambda: step(sets[0], sets[1]))
    pl.when(parity == 1)(lambda: step(sets[1], sets[0]))


def ffn_act_call(h2, w_gate, w_up, dw, dw_bias, *, hc=256, rb=64):
    B, L, D = h2.shape
    F = w_up.shape[1]
    n_chunks = F // hc
    wg = w_gate.reshape(D, n_chunks, hc).transpose(1, 0, 2)
    return pl.pallas_call(
        functools.partial(_ffn_act_kernel, L=L, rb=rb, n_chunks=n_chunks),
        grid=(B, n_chunks),
        in_specs=[pl.BlockSpec((1, L, D), lambda b, j: (b, 0, 0)),
                  pl.BlockSpec((n_chunks, D, hc), lambda b, j: (0, 0, 0), pipeline_mode=pl.Buffered(1)),
                  pl.BlockSpec((D, hc), lambda b, j: (0, j)),
                  pl.BlockSpec((9, hc), lambda b, j: (0, j)),
                  pl.BlockSpec((1, hc), lambda b, j: (0, j))],
        out_specs=pl.BlockSpec((1, L, hc), lambda b, j: (b, 0, j)),
        out_shape=jax.ShapeDtypeStruct((B, L, F), BF16),
        scratch_shapes=[pltpu.VMEM((L + 2 * FFN_PAD, hc), F32)] * 6 + [pltpu.VMEM((L, hc), F32)],
        compiler_params=_cp(("arbitrary", "arbitrary")),
        name="ffn_act",
    )(h2, wg, w_up, dw, dw_bias.reshape(1, F))


def _ffn_out_kernel(act_ref, xn_ref, g2_ref, wout_ref, fw_ref, o_ref):
    xo = xn_ref[0] + g2_ref[0] * _dot(act_ref[0], wout_ref[...])
    o_ref[0] = _rms_scale(xo) * fw_ref[...]


def ffn_out_call(act, x_new, g2, w_out, final_w, *, tm=512):
    B, L, F = act.shape
    D = w_out.shape[1]
    tok = lambda b, i: (b, i, 0)
    const = lambda b, i: (0, 0)
    return pl.pallas_call(
        _ffn_out_kernel,
        grid=(B, L // tm),
        in_specs=[pl.BlockSpec((1, tm, F), tok),
                  pl.BlockSpec((1, tm, D), tok),
                  pl.BlockSpec((1, 1, D), lambda b, i: (b, 0, 0)),
                  pl.BlockSpec((F, D), const, pipeline_mode=pl.Buffered(1)),
                  pl.BlockSpec((1, D), const)],
        out_specs=pl.BlockSpec((1, tm, D), tok),
        out_shape=jax.ShapeDtypeStruct((B, L, D), F32),
        compiler_params=_cp(("arbitrary", "arbitrary")),
        name="ffn_out",
    )(act, x_new, g2, w_out, final_w.reshape(1, D))


def _pad_rows(w, rows, at=0):
    out = jnp.zeros((rows, w.shape[1]), w.dtype)
    return lax.dynamic_update_slice(out, w, (at, 0))


def kernel(x, c, ctx, c_ctx, w_ada, b_ada, norm1_w, norm2_w, w_in, hy_conv_w, hy_filt_w1, hy_filt_b1,
           hy_filt_w2, hy_filt_b2, hy_filt_w3, hy_filt_freq, hy_decay, hy_bias, gla_a_up_f, gla_a_bias_f,
           gla_a_up_b, gla_a_bias_b, gla_norm_w, w_hy_proj, w_gla_proj, w_out, ffn_w_in, ffn_dw,
           ffn_dw_bias, ffn_w_out, final_norm_w):
    assert w_ada.shape[0] == 1, "single-layer block"
    B, L, D = x.shape
    R = GLA_LOWRANK

    n_rows = -(-(B + 1) // 8) * 8
    c_rows = jnp.zeros((n_rows, D), F32).at[:B].set(c).at[B].set(c_ctx)
    mod = ada_call(c_rows, w_ada[0], b_ada[0])
    sh1, sc1, g1, sh2, sc2, g2 = [m[:B, None, :] for m in jnp.split(mod, 6, axis=-1)]
    csh1, csc1 = mod[B:B + 1, None, 0:D], mod[B:B + 1, None, D:2 * D]

    w = w_in[0]
    o_q = HY_COLS
    o_k = o_q + GLA_QK
    o_v = o_k + GLA_QK
    o_r = o_v + GLA_VW
    o_af = o_r + GLA_VW
    o_ab = o_af + R
    o_g = o_ab + R
    w_main = jnp.concatenate([w[:, :o_k], w[:, o_v:o_af], w[:, o_g:]], axis=1).astype(BF16)
    w_kt = w[:, o_k:o_v].T.astype(BF16)
    w_v = w[:, o_v:o_r].astype(BF16)
    w_a = jnp.pad(w[:, o_af:o_g], ((0, 0), (0, LANES - 2 * R))).astype(BF16)
    w_at = w_a.T
    wup_f = _pad_rows(gla_a_up_f[0], LANES, 0)
    wup_b = _pad_rows(gla_a_up_b[0], LANES, R)
    wupt_f, wupt_b = wup_f.T, wup_b.T
    bias_f, bias_b = gla_a_bias_f[0].reshape(1, GLA_QK), gla_a_bias_b[0].reshape(1, GLA_QK)
    biast_f, biast_b = bias_f.reshape(GLA_QK, 1), bias_b.reshape(GLA_QK, 1)

    cos32, sin32 = dft_tables(HY_BLOCK)
    a_co, b_co, g_nyq = filter_call(L, hy_filt_w1[0], hy_filt_b1[0], hy_filt_w2[0], hy_filt_b2[0],
                                    hy_filt_w3[0], hy_filt_freq[0], hy_decay[0], cos32, sin32)
    fwd_tab = jnp.concatenate([cos32, sin32], axis=0).astype(BF16)
    inv_tab = jnp.concatenate([cos32, sin32], axis=1).astype(BF16)

    s_f, s_b = ctx_call(ctx, norm1_w[0], csc1, csh1, w_kt, w_v, w_at, wupt_f, biast_f, wupt_b, biast_b)

    u_hy, qvr, gates, kt, a, at = inproj_call(x, norm1_w[0], sc1, sh1, w_main, w_kt, w_a, w_at)
    y_hy = hyena_call(u_hy, hy_conv_w[0], hy_bias[0], fwd_tab, inv_tab, a_co, b_co, g_nyq)
    y_gla = gla_call(qvr, kt, a, at, s_f, s_b, wup_f, bias_f, wup_b, bias_b,
                     wupt_f, biast_f, wupt_b, biast_b, gla_norm_w[0])
    x_new, h2 = merge_call(x, y_hy, y_gla, gates, g1, w_hy_proj[0].astype(BF16), w_gla_proj[0].astype(BF16),
                           w_out[0].astype(BF16), norm2_w[0], sc2, sh2)
    w_ffn = ffn_w_in[0]
    act = ffn_act_call(h2, w_ffn[:, FFN_HIDDEN:].astype(BF16), w_ffn[:, :FFN_HIDDEN].astype(BF16),
                       ffn_dw[0].reshape(9, FFN_HIDDEN), ffn_dw_bias[0])
    return ffn_out_call(act, x_new, g2, ffn_w_out[0].astype(BF16), final_norm_w)
```

```python
import functools
import math

import jax
import jax.numpy as jnp
import numpy as np
from jax import lax
from jax.experimental import pallas as pl
from jax.experimental.pallas import tpu as pltpu

F32 = jnp.float32
BF16 = jnp.bfloat16
HIGHEST = lax.Precision.HIGHEST

D_MODEL = 1024
GRID_W = 64
HY_WIDTH = 512
HY_BANDS = 16
HY_EMB = 1 + 2 * HY_BANDS
HY_FILT_HIDDEN = 64
GLA_HEADS = 4
GLA_DK = 64
GLA_DV = 128
GLA_LOWRANK = 16
GLA_TAU = 16.0
GLA_CHUNK = 64
FFN_HIDDEN = 2816
EPS = 1e-6
HY_COLS = 3 * HY_WIDTH
GLA_QK = GLA_HEADS * GLA_DK
GLA_VW = GLA_HEADS * GLA_DV

LANES = 128
PAIR = 2 * GLA_CHUNK
VMEM_LIMIT = 56 * 1024 * 1024


def _cp(sem):
    return pltpu.CompilerParams(dimension_semantics=sem, vmem_limit_bytes=VMEM_LIMIT)


def _dot(a, b):
    return jnp.dot(a, b, preferred_element_type=F32)


def _dot_hi(a, b):
    return jnp.dot(a, b, preferred_element_type=F32, precision=HIGHEST)


def _sigmoid(x):
    return 1.0 / (1.0 + jnp.exp(-x))


def _silu(x):
    return x * _sigmoid(x)


def _log_sigmoid(x):
    return jnp.minimum(x, 0.0) - jnp.log(1.0 + jnp.exp(-jnp.abs(x)))


def _rms_scale(xf):
    return xf * lax.rsqrt(jnp.mean(xf * xf, axis=-1, keepdims=True) + EPS)


def _ada_kernel(c_ref, w_ref, b_ref, o_ref):
    o_ref[...] = _dot_hi(_silu(c_ref[...]), w_ref[...]) + b_ref[...]


def ada_call(c_rows, w_ada, b_ada):
    rows, d = c_rows.shape
    n = w_ada.shape[1]
    bn = 512
    return pl.pallas_call(
        _ada_kernel,
        grid=(n // bn,),
        in_specs=[pl.BlockSpec((rows, d), lambda j: (0, 0)),
                  pl.BlockSpec((d, bn), lambda j: (0, j)),
                  pl.BlockSpec((1, bn), lambda j: (0, j))],
        out_specs=pl.BlockSpec((rows, bn), lambda j: (0, j)),
        out_shape=jax.ShapeDtypeStruct((rows, n), F32),
        compiler_params=_cp(("arbitrary",)),
        name="ada",
    )(c_rows, w_ada, b_ada.reshape(1, n))


HY_BLOCK = 512


def _filt_mlp_kernel(feats_ref, t_ref, w1_ref, b1_ref, w2_ref, b2_ref, w3_ref, freq_ref, decay_ref, taps_ref):
    freq = freq_ref[...]
    h = jnp.sin(freq * (_dot_hi(feats_ref[...], w1_ref[...]) + b1_ref[...]))
    h = jnp.sin(freq * (_dot_hi(h, w2_ref[...]) + b2_ref[...]))
    k = _dot_hi(h, w3_ref[...]) * jnp.exp(-t_ref[...] * decay_ref[...])
    L = k.shape[0]
    row = lax.broadcasted_iota(jnp.int32, (L, HY_WIDTH), 0)
    k_f = k[:, :HY_WIDTH]
    k_b = jnp.where(row == 0, 0.0, k[:, HY_WIDTH:])
    norm = jnp.sum(jnp.abs(k_f), axis=0, keepdims=True) + jnp.sum(jnp.abs(k_b), axis=0, keepdims=True)
    taps_ref[0:L, :] = k_f / norm
    taps_ref[L:2 * L, :] = k_b / norm


def _filt_blk_kernel(c_ref, s_ref, x_ref, cx_ref, sx_ref, aux_ref, *, M):
    x = x_ref[...]
    cx_ref[0] = _dot_hi(c_ref[...], x)
    sx_ref[0] = _dot_hi(s_ref[...], x)
    row = lax.broadcasted_iota(jnp.int32, (M, HY_WIDTH), 0)
    alt = jnp.sum(jnp.where((row & 1) == 0, x, -x), axis=0, keepdims=True)
    sub = lax.broadcasted_iota(jnp.int32, (8, HY_WIDTH), 0)
    aux_ref[0] = jnp.where(sub == 0, x[0:8], jnp.where(sub == 1, alt, 0.0))


def _filt_seg_kernel(cx1_ref, sx1_ref, aux1_ref, cx2_ref, sx2_ref, aux2_ref, a_ref, b_ref, gn_ref, *, M, P):
    d = pl.program_id(0) - (P - 1)
    f = lax.broadcasted_iota(jnp.int32, (M, 1), 0)
    sf = jnp.where((f & 1) == 0, 1.0, -1.0)
    s = jnp.where(d == 0, 1.0, sf)
    sg1 = jnp.where(d >= 0, -1.0, 1.0)
    sg2 = jnp.where(d >= 1, -1.0, 1.0)
    w = jnp.where(f == 0, 0.5 / M, 1.0 / M)
    x2_0 = aux2_ref[0, 0:1, :]
    a_ref[0] = w * (cx1_ref[0] + s * (cx2_ref[0] - x2_0))
    b_ref[0] = w * (sg1 * sx1_ref[0] + (sg2 * s) * sx2_ref[0])
    gn = (aux1_ref[0, 1:2, :] + aux2_ref[0, 1:2, :] - x2_0) * (0.5 / M)
    gn_ref[0] = jnp.broadcast_to(gn, gn_ref.shape[1:])


def filter_call(L, hy_filt_w1, hy_filt_b1, hy_filt_w2, hy_filt_b2, hy_filt_w3, hy_filt_freq, hy_decay,
                cos32, sin32):
    M = HY_BLOCK
    P = L // M
    t = jnp.linspace(0.0, 1.0, L, dtype=F32)[:, None]
    w = (2.0 * np.pi / L) * jnp.arange(L, dtype=F32)[:, None]
    f = jnp.linspace(1e-4, HY_BANDS - 1, HY_BANDS, dtype=F32)[None, :]
    feats = jnp.concatenate([t, jnp.cos(f * w), -jnp.sin(f * w)], axis=-1)
    feats = jnp.pad(feats, ((0, 0), (0, LANES - HY_EMB)))
    w1 = jnp.pad(hy_filt_w1, ((0, LANES - HY_EMB), (0, 0)))
    hid = HY_FILT_HIDDEN
    taps = pl.pallas_call(
        _filt_mlp_kernel,
        out_shape=jax.ShapeDtypeStruct((2 * L, HY_WIDTH), F32),
        compiler_params=pltpu.CompilerParams(vmem_limit_bytes=VMEM_LIMIT),
        name="filt_mlp",
    )(feats, t, w1, hy_filt_b1.reshape(1, hid), hy_filt_w2, hy_filt_b2.reshape(1, hid), hy_filt_w3,
      hy_filt_freq.reshape(1, hid), hy_decay.reshape(1, 2 * HY_WIDTH))

    blk = lambda n: jax.ShapeDtypeStruct((n, M, HY_WIDTH), F32)
    aux = lambda n: jax.ShapeDtypeStruct((n, 8, HY_WIDTH), F32)
    blk_spec = lambda imap: pl.BlockSpec((1, M, HY_WIDTH), imap)
    aux_spec = lambda imap: pl.BlockSpec((1, 8, HY_WIDTH), imap)
    own = lambda g: (g, 0, 0)
    cx, sx, ax = pl.pallas_call(
        functools.partial(_filt_blk_kernel, M=M),
        grid=(2 * P,),
        in_specs=[pl.BlockSpec((M, M), lambda g: (0, 0)),
                  pl.BlockSpec((M, M), lambda g: (0, 0)),
                  pl.BlockSpec((M, HY_WIDTH), lambda g: (g, 0))],
        out_specs=(blk_spec(own), blk_spec(own), aux_spec(own)),
        out_shape=(blk(2 * P), blk(2 * P), aux(2 * P)),
        compiler_params=_cp(("arbitrary",)),
        name="filt_blk",
    )(cos32, sin32, taps)

    def first(g):
        d = g - (P - 1)
        return (jnp.where(d >= 0, d, P - d), 0, 0)

    def second(g):
        d = g - (P - 1)
        return (jnp.where(d >= 1, d - 1, jnp.where(d == 0, P, P - d - 1)), 0, 0)

    n_seg = 2 * P - 1
    a_co, b_co, g_nyq = pl.pallas_call(
        functools.partial(_filt_seg_kernel, M=M, P=P),
        grid=(n_seg,),
        in_specs=[blk_spec(first), blk_spec(first), aux_spec(first),
                  blk_spec(second), blk_spec(second), aux_spec(second)],
        out_specs=(blk_spec(own), blk_spec(own), aux_spec(own)),
        out_shape=(blk(n_seg), blk(n_seg), aux(n_seg)),
        compiler_params=_cp(("arbitrary",)),
        name="filt_seg",
    )(cx, sx, ax, cx, sx, ax)
    return a_co, b_co, g_nyq


def dft_tables(M):
    n_fft = 2 * M
    idx = (jnp.arange(M, dtype=jnp.int32)[:, None] * jnp.arange(M, dtype=jnp.int32)[None, :]) % n_fft
    ang = idx.astype(F32) * (2.0 * np.pi / n_fft)
    return jnp.cos(ang), jnp.sin(ang)


def _modulated_norm(x_ref, nw_ref, sc_ref, sh_ref):
    y = _rms_scale(x_ref[0].astype(F32)) * nw_ref[...]
    return (y * (1.0 + sc_ref[0]) + sh_ref[0]).astype(BF16)


def _inproj_kernel(x_ref, nw_ref, sc_ref, sh_ref, w_ref, wkt_ref, wa_ref, wat_ref,
                   hy_ref, qvr_ref, gate_ref, kt_ref, a_ref, at_ref, *, tm):
    h = _modulated_norm(x_ref, nw_ref, sc_ref, sh_ref)
    col = 0
    for o_ref in (hy_ref, qvr_ref, gate_ref):
        width = o_ref.shape[-1]
        done = 0
        while done < width:
            step = min(512, width - done)
            o_ref[0, :, done:done + step] = _dot(h, w_ref[:, col:col + step]).astype(o_ref.dtype)
            done += step
            col += step
    nt = (((1,), (1,)), ((), ()))
    kt = lax.dot_general(wkt_ref[...], h, nt, preferred_element_type=F32).astype(BF16)
    for p in range(tm // PAIR):
        kt_ref[0, p] = kt[:, p * PAIR:(p + 1) * PAIR]
    a_ref[0] = _dot(h, wa_ref[...])
    at_ref[0] = lax.dot_general(wat_ref[...], h, nt, preferred_element_type=F32)


def inproj_call(x, norm_w, sc, sh, w_main, w_kt, w_a, w_at, *, tm=512):
    B, L, D = x.shape
    n_main = w_main.shape[1]
    const = lambda b, i: (0, 0)
    tok = lambda b, i: (b, i, 0)
    bvec = lambda b, i: (b, 0, 0)
    return pl.pallas_call(
        functools.partial(_inproj_kernel, tm=tm),
        grid=(B, L // tm),
        in_specs=[pl.BlockSpec((1, tm, D), tok),
                  pl.BlockSpec((1, D), const),
                  pl.BlockSpec((1, 1, D), bvec),
                  pl.BlockSpec((1, 1, D), bvec),
                  pl.BlockSpec((D, n_main), const, pipeline_mode=pl.Buffered(1)),
                  pl.BlockSpec((GLA_QK, D), const),
                  pl.BlockSpec((D, LANES), const),
                  pl.BlockSpec((LANES, D), const)],
        out_specs=(pl.BlockSpec((1, tm, HY_COLS), tok),
                   pl.BlockSpec((1, tm, GLA_QK + 2 * GLA_VW), tok),
                   pl.BlockSpec((1, tm, 2 * D), tok),
                   pl.BlockSpec((1, tm // PAIR, GLA_QK, PAIR), lambda b, i: (b, i, 0, 0)),
                   pl.BlockSpec((1, tm, LANES), tok),
                   pl.BlockSpec((1, LANES, tm), lambda b, i: (b, 0, i))),
        out_shape=(jax.ShapeDtypeStruct((B, L, HY_COLS), BF16),
                   jax.ShapeDtypeStruct((B, L, GLA_QK + 2 * GLA_VW), BF16),
                   jax.ShapeDtypeStruct((B, L, 2 * D), BF16),
                   jax.ShapeDtypeStruct((B, L // PAIR, GLA_QK, PAIR), BF16),
                   jax.ShapeDtypeStruct((B, L, LANES), F32),
                   jax.ShapeDtypeStruct((B, LANES, L), F32)),
        compiler_params=_cp(("arbitrary", "arbitrary")),
        name="inproj",
    )(x, norm_w.reshape(1, D), sc, sh, w_main, w_kt, w_a, w_at)


def _short_conv_rows(u_ref, w_ref, t0, rows, col0, ncol, L):
    halo = 16
    lo = max(t0 - halo, 0)
    hi = min(t0 + rows + halo, L)
    n = hi - lo
    blk = u_ref[0, lo:hi, col0:col0 + ncol].astype(F32)
    ridx = lax.broadcasted_iota(jnp.int32, (n, 1), 0)
    prev = pltpu.roll(blk, 1, axis=0)
    nxt = pltpu.roll(blk, n - 1, axis=0)
    if lo == 0:
        prev = jnp.where(ridx == 0, 0.0, prev)
    if hi == L:
        nxt = jnp.where(ridx == n - 1, 0.0, nxt)
    w = w_ref[:, col0:col0 + ncol]
    out = prev * w[0:1] + blk * w[1:2] + nxt * w[2:3]
    return out[t0 - lo:t0 - lo + rows]


def _hyena_kernel(u_ref, cw_ref, bias_ref, fwd_ref, inv_ref, a_ref, b_ref, gn_ref, o_ref,
                  zb_ref, x0_ref, zp_ref, uv_ref, y_ref, *, L, M, rb, fb):
    W = HY_WIDTH
    P = L // M

    def sign_rows(n):
        t = lax.broadcasted_iota(jnp.int32, (n, 1), 0)
        return jnp.where((t & 1) == 0, 1.0, -1.0)

    znyq = [jnp.zeros((1, W), F32) for _ in range(P)]
    for t0 in range(0, L, rb):
        x1 = _short_conv_rows(u_ref, cw_ref, t0, rb, W, W, L)
        v = _short_conv_rows(u_ref, cw_ref, t0, rb, 2 * W, W, L)
        z = x1 * v
        zb_ref[t0:t0 + rb, :] = z.astype(BF16)
        x0_ref[t0:t0 + rb, :] = _short_conv_rows(u_ref, cw_ref, t0, rb, 0, W, L).astype(BF16)
        znyq[t0 // M] = znyq[t0 // M] + jnp.sum(sign_rows(rb) * z, axis=0, keepdims=True)

    for j in range(P):
        zp_ref[j] = _dot(fwd_ref[...], zb_ref[j * M:(j + 1) * M, :])

    bias = bias_ref[...]
    for i in range(P):
        for f0 in range(0, M, fb):
            u = jnp.zeros((fb, W), F32)
            v = jnp.zeros((fb, W), F32)
            for j in range(P):
                d = i - j + P - 1
                zre = zp_ref[j, f0:f0 + fb, :]
                p = zp_ref[j, M + f0:M + f0 + fb, :]
                a = a_ref[d, f0:f0 + fb, :]
                b = b_ref[d, f0:f0 + fb, :]
                u = u + zre * a + p * b
                v = v + p * a - zre * b
            uv_ref[f0:f0 + fb, :] = u.astype(BF16)
            uv_ref[M + f0:M + f0 + fb, :] = v.astype(BF16)
        y_ref[...] = _dot(inv_ref[...], uv_ref[...])
        ynyq = jnp.zeros((1, W), F32)
        for j in range(P):
            ynyq = ynyq + znyq[j] * gn_ref[i - j + P - 1, 0:1, :]
        for r0 in range(0, M, rb):
            rows = slice(i * M + r0, i * M + r0 + rb)
            z = zb_ref[rows, :].astype(F32)
            y = y_ref[r0:r0 + rb, :] + sign_rows(rb) * ynyq + bias * z
            o_ref[0, rows, :] = (x0_ref[rows, :].astype(F32) * y).astype(o_ref.dtype)


def hyena_call(u_hy, conv_w, hy_bias, fwd_tab, inv_tab, a_co, b_co, g_nyq, *, rb=256, fb=16):
    B, L, _ = u_hy.shape
    W = HY_WIDTH
    M = HY_BLOCK
    n_seg = a_co.shape[0]
    const2 = lambda b: (0, 0)
    const3 = lambda b: (0, 0, 0)
    return pl.pallas_call(
        functools.partial(_hyena_kernel, L=L, M=M, rb=rb, fb=fb),
        grid=(B,),
        in_specs=[pl.BlockSpec((1, L, HY_COLS), lambda b: (b, 0, 0)),
                  pl.BlockSpec((3, HY_COLS), const2),
                  pl.BlockSpec((1, W), const2),
                  pl.BlockSpec((2 * M, M), const2),
                  pl.BlockSpec((M, 2 * M), const2),
                  pl.BlockSpec((n_seg, M, W), const3, pipeline_mode=pl.Buffered(1)),
                  pl.BlockSpec((n_seg, M, W), const3, pipeline_mode=pl.Buffered(1)),
                  pl.BlockSpec((n_seg, 8, W), const3)],
        out_specs=pl.BlockSpec((1, L, W), lambda b: (b, 0, 0)),
        out_shape=jax.ShapeDtypeStruct((B, L, W), BF16),
        scratch_shapes=[pltpu.VMEM((L, W), BF16),
                        pltpu.VMEM((L, W), BF16),
                        pltpu.VMEM((L // M, 2 * M, W), F32),
                        pltpu.VMEM((2 * M, W), BF16),
                        pltpu.VMEM((M, W), F32)],
        compiler_params=_cp(("arbitrary",)),
        name="hyena",
    )(u_hy, conv_w, hy_bias.reshape(1, W), fwd_tab, inv_tab, a_co, b_co, g_nyq)


def _split3(x):
    x1 = x.astype(BF16)
    r = x - x1.astype(F32)
    x2 = r.astype(BF16)
    x3 = (r - x2.astype(F32)).astype(BF16)
    return x1, x2, x3


def _tri_left(m, x):
    x1, x2, x3 = _split3(x)
    return _dot(m, x1) + _dot(m, x2) + _dot(m, x3)


def _tri_right(x, m):
    x1, x2, x3 = _split3(x)
    return _dot(x1, m) + _dot(x2, m) + _dot(x3, m)


def _tri_mats(n, block):
    r = np.arange(n)
    same = (r[:, None] // block) == (r[None, :] // block)
    lower = same & (r[:, None] >= r[None, :])
    upper = same & (r[:, None] <= r[None, :])
    return jnp.asarray(lower, BF16), jnp.asarray(upper, BF16)


def _log_decay_tok(a, wup_ref, bias_ref):
    return _log_sigmoid(_dot_hi(a, wup_ref[...]) + bias_ref[...]) * (1.0 / GLA_TAU)


def _log_decay_feat(at, wupt_ref, biast_ref):
    return _log_sigmoid(_dot_hi(wupt_ref[...], at) + biast_ref[...]) * (1.0 / GLA_TAU)


def _ctx_kernel(x_ref, nw_ref, sc_ref, sh_ref, wkt_ref, wv_ref, wat_ref,
                wuptf_ref, biastf_ref, wuptb_ref, biastb_ref, lo_ref, up_ref, sf_ref, sb_ref):
    h = _modulated_norm(x_ref, nw_ref, sc_ref, sh_ref)
    nt = (((1,), (1,)), ((), ()))
    kt = lax.dot_general(wkt_ref[...], h, nt, preferred_element_type=F32)
    v = _dot(h, wv_ref[...]).astype(BF16)
    at = lax.dot_general(wat_ref[...], h, nt, preferred_element_type=F32)
    la_f = _log_decay_feat(at, wuptf_ref, biastf_ref)
    la_b = _log_decay_feat(at, wuptb_ref, biastb_ref)
    lo = lo_ref[...]
    up = up_ref[...]
    w_f = jnp.exp(_tri_right(la_f, lo) - la_f)
    w_b = jnp.exp(_tri_right(la_b, up) - la_b)
    kf = (kt * w_f).astype(BF16)
    kb = (kt * w_b).astype(BF16)
    for hd in range(GLA_HEADS):
        r0, c0 = hd * GLA_DK, hd * GLA_DV
        sf_ref[0, hd] = _dot(kf[r0:r0 + GLA_DK], v[:, c0:c0 + GLA_DV])
        sb_ref[0, hd] = _dot(kb[r0:r0 + GLA_DK], v[:, c0:c0 + GLA_DV])


def ctx_call(ctx, norm_w, csc, csh, w_kt, w_v, w_at, wupt_f, biast_f, wupt_b, biast_b):
    B, Lc, D = ctx.shape
    lo, up = _tri_mats(Lc, Lc)
    const2 = lambda b: (0, 0)
    const3 = lambda b: (0, 0, 0)
    st = jax.ShapeDtypeStruct((B, GLA_HEADS, GLA_DK, GLA_DV), F32)
    st_spec = pl.BlockSpec((1, GLA_HEADS, GLA_DK, GLA_DV), lambda b: (b, 0, 0, 0))
    return pl.pallas_call(
        _ctx_kernel,
        grid=(B,),
        in_specs=[pl.BlockSpec((1, Lc, D), lambda b: (b, 0, 0)),
                  pl.BlockSpec((1, D), const2),
                  pl.BlockSpec((1, 1, D), const3),
                  pl.BlockSpec((1, 1, D), const3),
                  pl.BlockSpec((GLA_QK, D), const2),
                  pl.BlockSpec((D, GLA_VW), const2),
                  pl.BlockSpec((LANES, D), const2),
                  pl.BlockSpec((GLA_QK, LANES), const2),
                  pl.BlockSpec((GLA_QK, 1), const2),
                  pl.BlockSpec((GLA_QK, LANES), const2),
                  pl.BlockSpec((GLA_QK, 1), const2),
                  pl.BlockSpec((Lc, Lc), const2),
                  pl.BlockSpec((Lc, Lc), const2)],
        out_specs=(st_spec, st_spec),
        out_shape=(st, st),
        compiler_params=_cp(("arbitrary",)),
        name="ctx",
    )(ctx, norm_w.reshape(1, D), csc, csh, w_kt, w_v, w_at, wupt_f, biast_f, wupt_b, biast_b, lo, up)


def _gla_kernel(qvr_ref, kt_ref, a_ref, at_ref, sf_ref, sb_ref,
                wupf_ref, biasf_ref, wupb_ref, biasb_ref,
                wuptf_ref, biastf_ref, wuptb_ref, biastb_ref,
                lo_ref, up_ref, nw_ref, o_ref,
                laf_ref, lab_ref, latf_ref, latb_ref, accf_ref, accb_ref, stf_ref, stb_ref, *, L, rb):
    n_pairs = L // PAIR
    C = GLA_CHUNK
    H, DK, DV = GLA_HEADS, GLA_DK, GLA_DV
    scale = DK ** -0.5

    for t0 in range(0, L, rb):
        a = a_ref[0, t0:t0 + rb, :]
        laf_ref[t0:t0 + rb, :] = _log_decay_tok(a, wupf_ref, biasf_ref)
        lab_ref[t0:t0 + rb, :] = _log_decay_tok(a, wupb_ref, biasb_ref)
    for p in range(n_pairs):
        at = at_ref[0, :, p * PAIR:(p + 1) * PAIR]
        latf_ref[p] = _log_decay_feat(at, wuptf_ref, biastf_ref)
        latb_ref[p] = _log_decay_feat(at, wuptb_ref, biastb_ref)

    lo = lo_ref[...]
    up = up_ref[...]
    rr = lax.broadcasted_iota(jnp.int32, (C, C), 0)
    cc = lax.broadcasted_iota(jnp.int32, (C, C), 1)

    def half_pair(p, la_ref, lat_ref, st_ref, acc_ref, forward):
        m_tok = lo if forward else up
        m_feat = up if forward else lo
        keep = (rr >= cc) if forward else (rr <= cc)
        last = C - 1 if forward else 0
        t0 = pl.multiple_of(p * PAIR, PAIR)
        b_tok = _tri_left(m_tok, la_ref[pl.ds(t0, PAIR), :])
        b_feat = _tri_right(lat_ref[p], m_feat)
        q = qvr_ref[0, pl.ds(t0, PAIR), 0:GLA_QK].astype(F32)
        qe = (q * scale * jnp.exp(b_tok)).astype(BF16)
        kt = kt_ref[0, p].astype(F32)
        ket = (kt * jnp.exp(-b_feat)).astype(BF16)
        v = qvr_ref[0, pl.ds(t0, PAIR), GLA_QK:GLA_QK + GLA_VW]
        for half in ((0, 1) if forward else (1, 0)):
            r0 = half * C
            b_last = b_feat[:, r0 + last:r0 + last + 1]
            kdt = (kt[:, r0:r0 + C] * jnp.exp(b_last - b_feat[:, r0:r0 + C])).astype(BF16)
            g = jnp.exp(b_last)
            for hd in range(H):
                k0, v0 = hd * DK, hd * DV
                qe_h = qe[r0:r0 + C, k0:k0 + DK]
                v_h = v[r0:r0 + C, v0:v0 + DV]
                s_prev = st_ref[hd]
                sc = _dot(qe_h, ket[k0:k0 + DK, r0:r0 + C])
                sc = jnp.where(keep, sc, 0.0).astype(BF16)
                acc_ref[pl.ds(t0 + r0, C), v0:v0 + DV] = _dot(sc, v_h) + _dot(qe_h, s_prev.astype(BF16))
                st_ref[hd] = g[k0:k0 + DK] * s_prev + _dot(kdt[k0:k0 + DK], v_h)

    for hd in range(H):
        stf_ref[hd] = sf_ref[0, hd]
        stb_ref[hd] = sb_ref[0, hd]

    def pair_body(i, carry):
        half_pair(i, laf_ref, latf_ref, stf_ref, accf_ref, True)
        half_pair(n_pairs - 1 - i, lab_ref, latb_ref, stb_ref, accb_ref, False)
        return carry

    lax.fori_loop(0, n_pairs, pair_body, 0, unroll=2)

    for t0 in range(0, L, rb):
        r = qvr_ref[0, t0:t0 + rb, GLA_QK + GLA_VW:GLA_QK + 2 * GLA_VW].astype(F32)
        for hd in range(H):
            v0 = hd * DV
            o = _rms_scale(accf_ref[t0:t0 + rb, v0:v0 + DV] + accb_ref[t0:t0 + rb, v0:v0 + DV])
            y = o * nw_ref[:, v0:v0 + DV] * _silu(r[:, v0:v0 + DV])
            o_ref[0, t0:t0 + rb, v0:v0 + DV] = y.astype(o_ref.dtype)


def gla_call(qvr, kt, a, at, s_f, s_b, wup_f, bias_f, wup_b, bias_b,
             wupt_f, biast_f, wupt_b, biast_b, norm_w, *, rb=256):
    B, L, _ = qvr.shape
    lo, up = _tri_mats(PAIR, GLA_CHUNK)
    const2 = lambda b: (0, 0)
    st_spec = pl.BlockSpec((1, GLA_HEADS, GLA_DK, GLA_DV), lambda b: (b, 0, 0, 0))
    return pl.pallas_call(
        functools.partial(_gla_kernel, L=L, rb=rb),
        grid=(B,),
        in_specs=[pl.BlockSpec((1, L, GLA_QK + 2 * GLA_VW), lambda b: (b, 0, 0)),
                  pl.BlockSpec((1, L // PAIR, GLA_QK, PAIR), lambda b: (b, 0, 0, 0)),
                  pl.BlockSpec((1, L, LANES), lambda b: (b, 0, 0)),
                  pl.BlockSpec((1, LANES, L), lambda b: (b, 0, 0)),
                  st_spec, st_spec,
                  pl.BlockSpec((LANES, GLA_QK), const2),
                  pl.BlockSpec((1, GLA_QK), const2),
                  pl.BlockSpec((LANES, GLA_QK), const2),
                  pl.BlockSpec((1, GLA_QK), const2),
                  pl.BlockSpec((GLA_QK, LANES), const2),
                  pl.BlockSpec((GLA_QK, 1), const2),
                  pl.BlockSpec((GLA_QK, LANES), const2),
                  pl.BlockSpec((GLA_QK, 1), const2),
                  pl.BlockSpec((PAIR, PAIR), const2),
                  pl.BlockSpec((PAIR, PAIR), const2),
                  pl.BlockSpec((1, GLA_VW), const2)],
        out_specs=pl.BlockSpec((1, L, GLA_VW), lambda b: (b, 0, 0)),
        out_shape=jax.ShapeDtypeStruct((B, L, GLA_VW), BF16),
        scratch_shapes=[pltpu.VMEM((L, GLA_QK), F32),
                        pltpu.VMEM((L, GLA_QK), F32),
                        pltpu.VMEM((L // PAIR, GLA_QK, PAIR), F32),
                        pltpu.VMEM((L // PAIR, GLA_QK, PAIR), F32),
                        pltpu.VMEM((L, GLA_VW), F32),
                        pltpu.VMEM((L, GLA_VW), F32),
                        pltpu.VMEM((GLA_HEADS, GLA_DK, GLA_DV), F32),
                        pltpu.VMEM((GLA_HEADS, GLA_DK, GLA_DV), F32)],
        compiler_params=_cp(("arbitrary",)),
        name="gla",
    )(qvr, kt, a, at, s_f, s_b, wup_f, bias_f, wup_b, bias_b,
      wupt_f, biast_f, wupt_b, biast_b, lo, up, norm_w.reshape(1, GLA_VW))


def _merge_kernel(x_ref, yhy_ref, ygla_ref, gate_ref, g1_ref, whp_ref, wgp_ref, wo_ref,
                  nw_ref, sc_ref, sh_ref, xn_ref, h2_ref):
    D = D_MODEL
    g_hy = _sigmoid(gate_ref[0, :, 0:D].astype(F32))
    g_gla = _sigmoid(gate_ref[0, :, D:2 * D].astype(F32))
    merged = g_hy * _dot(yhy_ref[0], whp_ref[...]) + g_gla * _dot(ygla_ref[0], wgp_ref[...])
    mix = _dot(merged.astype(BF16), wo_ref[...])
    xn = x_ref[0] + g1_ref[0] * mix
    xn_ref[0] = xn
    y = _rms_scale(xn) * nw_ref[...]
    h2_ref[0] = (y * (1.0 + sc_ref[0]) + sh_ref[0]).astype(h2_ref.dtype)


def merge_call(x, y_hy, y_gla, gates, g1, w_hy_proj, w_gla_proj, w_out, norm2_w, sc2, sh2, *, tm=512):
    B, L, D = x.shape
    tok = lambda b, i: (b, i, 0)
    bvec = lambda b, i: (b, 0, 0)
    const = lambda b, i: (0, 0)
    return pl.pallas_call(
        _merge_kernel,
        grid=(B, L // tm),
        in_specs=[pl.BlockSpec((1, tm, D), tok),
                  pl.BlockSpec((1, tm, HY_WIDTH), tok),
                  pl.BlockSpec((1, tm, GLA_VW), tok),
                  pl.BlockSpec((1, tm, 2 * D), tok),
                  pl.BlockSpec((1, 1, D), bvec),
                  pl.BlockSpec((HY_WIDTH, D), const),
                  pl.BlockSpec((GLA_VW, D), const),
                  pl.BlockSpec((D, D), const),
                  pl.BlockSpec((1, D), const),
                  pl.BlockSpec((1, 1, D), bvec),
                  pl.BlockSpec((1, 1, D), bvec)],
        out_specs=(pl.BlockSpec((1, tm, D), tok), pl.BlockSpec((1, tm, D), tok)),
        out_shape=(jax.ShapeDtypeStruct((B, L, D), F32), jax.ShapeDtypeStruct((B, L, D), BF16)),
        compiler_params=_cp(("arbitrary", "arbitrary")),
        name="merge",
    )(x, y_hy, y_gla, gates, g1, w_hy_proj, w_gla_proj, w_out, norm2_w.reshape(1, D), sc2, sh2)


FFN_PAD = GRID_W


def _ffn_act_kernel(h_ref, wg0_ref, wgn_ref, wup_ref, dw_ref, dwb_ref, o_ref,
                    g0_ref, l0_ref, r0_ref, g1_ref, l1_ref, r1_ref, s_ref, *, L, rb, n_chunks):
    j = pl.program_id(1)
    hc = wup_ref.shape[1]
    mb = 128
    sub = lax.broadcasted_iota(jnp.int32, (8, hc), 0)
    sets = ((g0_ref, l0_ref, r0_ref), (g1_ref, l1_ref, r1_ref))

    def project(dst, w_ref, t0):
        gate_ref, gl_ref, gr_ref = dst
        gate_ref[FFN_PAD + t0:FFN_PAD + t0 + mb, :] = _dot(h_ref[0, t0:t0 + mb, :], w_ref[...])
        for r0 in range(FFN_PAD + t0, FFN_PAD + t0 + mb, GRID_W):
            g = gate_ref[r0:r0 + GRID_W, :]
            left = pltpu.roll(g, 1, axis=0)
            right = pltpu.roll(g, GRID_W - 1, axis=0)
            gl_ref[r0:r0 + 8, :] = jnp.where(sub == 0, 0.0, left[0:8])
            gl_ref[r0 + 8:r0 + GRID_W, :] = left[8:]
            gr_ref[r0:r0 + GRID_W - 8, :] = right[:GRID_W - 8]
            gr_ref[r0 + GRID_W - 8:r0 + GRID_W, :] = jnp.where(sub == 7, 0.0, right[GRID_W - 8:])

    @pl.when(j == 0)
    def _():
        zeros = jnp.zeros((FFN_PAD, hc), F32)
        for ref in sets[0] + sets[1]:
            ref[0:FFN_PAD, :] = zeros
            ref[FFN_PAD + L:FFN_PAD + L + FFN_PAD, :] = zeros
        for t0 in range(0, L, mb):
            project(sets[0], wg0_ref, t0)

    def step(cur, nxt):
        gate_ref, gl_ref, gr_ref = cur
        dw = dw_ref[...]
        for t0 in range(0, L, mb):
            project(nxt, wgn_ref, t0)
            s_ref[t0:t0 + mb, :] = _dot(h_ref[0, t0:t0 + mb, :], wup_ref[...])
            for r0 in range(t0, t0 + mb, rb):
                conv = jnp.broadcast_to(dwb_ref[...], (rb, hc))
                for dy in range(3):
                    rows = pl.ds(FFN_PAD + r0 + (dy - 1) * GRID_W, rb)
                    conv = (conv + gl_ref[rows, :] * dw[3 * dy:3 * dy + 1]
                            + gate_ref[rows, :] * dw[3 * dy + 1:3 * dy + 2]
                            + gr_ref[rows, :] * dw[3 * dy + 2:3 * dy + 3])
                o_ref[0, r0:r0 + rb, :] = (_silu(conv) * s_ref[r0:r0 + rb, :]).astype(o_ref.dtype)

    parity = lax.rem(j, 2)
    pl.when(parity == 0)(lambda: step(sets[0], sets[1]))
    pl.when(parity == 1)(lambda: step(sets[1], sets[0]))


def ffn_act_call(h2, w_gate, w_up, dw, dw_bias, *, hc=256, rb=64):
    B, L, D = h2.shape
    F = w_up.shape[1]
    n_chunks = F // hc
    return pl.pallas_call(
        functools.partial(_ffn_act_kernel, L=L, rb=rb, n_chunks=n_chunks),
        grid=(B, n_chunks),
        in_specs=[pl.BlockSpec((1, L, D), lambda b, j: (b, 0, 0)),
                  pl.BlockSpec((D, hc), lambda b, j: (0, 0)),
                  pl.BlockSpec((D, hc), lambda b, j: (0, jnp.minimum(j + 1, n_chunks - 1))),
                  pl.BlockSpec((D, hc), lambda b, j: (0, j)),
                  pl.BlockSpec((9, hc), lambda b, j: (0, j)),
                  pl.BlockSpec((1, hc), lambda b, j: (0, j))],
        out_specs=pl.BlockSpec((1, L, hc), lambda b, j: (b, 0, j)),
        out_shape=jax.ShapeDtypeStruct((B, L, F), BF16),
        scratch_shapes=[pltpu.VMEM((L + 2 * FFN_PAD, hc), F32)] * 6 + [pltpu.VMEM((L, hc), F32)],
        compiler_params=_cp(("arbitrary", "arbitrary")),
        name="ffn_act",
    )(h2, w_gate, w_gate, w_up, dw, dw_bias.reshape(1, F))


def _ffn_out_kernel(act_ref, xn_ref, g2_ref, wout_ref, fw_ref, o_ref):
    xo = xn_ref[0] + g2_ref[0] * _dot(act_ref[0], wout_ref[...])
    o_ref[0] = _rms_scale(xo) * fw_ref[...]


def ffn_out_call(act, x_new, g2, w_out, final_w, *, tm=512):
    B, L, F = act.shape
    D = w_out.shape[1]
    tok = lambda b, i: (b, i, 0)
    const = lambda b, i: (0, 0)
    return pl.pallas_call(
        _ffn_out_kernel,
        grid=(B, L // tm),
        in_specs=[pl.BlockSpec((1, tm, F), tok),
                  pl.BlockSpec((1, tm, D), tok),
                  pl.BlockSpec((1, 1, D), lambda b, i: (b, 0, 0)),
                  pl.BlockSpec((F, D), const, pipeline_mode=pl.Buffered(1)),
                  pl.BlockSpec((1, D), const)],
        out_specs=pl.BlockSpec((1, tm, D), tok),
        out_shape=jax.ShapeDtypeStruct((B, L, D), F32),
        compiler_params=_cp(("arbitrary", "arbitrary")),
        name="ffn_out",
    )(act, x_new, g2, w_out, final_w.reshape(1, D))


def _pad_rows(w, rows, at=0):
    out = jnp.zeros((rows, w.shape[1]), w.dtype)
    return lax.dynamic_update_slice(out, w, (at, 0))


def kernel(x, c, ctx, c_ctx, w_ada, b_ada, norm1_w, norm2_w, w_in, hy_conv_w, hy_filt_w1, hy_filt_b1,
           hy_filt_w2, hy_filt_b2, hy_filt_w3, hy_filt_freq, hy_decay, hy_bias, gla_a_up_f, gla_a_bias_f,
           gla_a_up_b, gla_a_bias_b, gla_norm_w, w_hy_proj, w_gla_proj, w_out, ffn_w_in, ffn_dw,
           ffn_dw_bias, ffn_w_out, final_norm_w):
    assert w_ada.shape[0] == 1, "single-layer block"
    B, L, D = x.shape
    R = GLA_LOWRANK

    n_rows = -(-(B + 1) // 8) * 8
    c_rows = jnp.zeros((n_rows, D), F32).at[:B].set(c).at[B].set(c_ctx)
    mod = ada_call(c_rows, w_ada[0], b_ada[0])
    sh1, sc1, g1, sh2, sc2, g2 = [m[:B, None, :] for m in jnp.split(mod, 6, axis=-1)]
    csh1, csc1 = mod[B:B + 1, None, 0:D], mod[B:B + 1, None, D:2 * D]

    w = w_in[0]
    o_q = HY_COLS
    o_k = o_q + GLA_QK
    o_v = o_k + GLA_QK
    o_r = o_v + GLA_VW
    o_af = o_r + GLA_VW
    o_ab = o_af + R
    o_g = o_ab + R
    w_main = jnp.concatenate([w[:, :o_k], w[:, o_v:o_af], w[:, o_g:]], axis=1).astype(BF16)
    w_kt = w[:, o_k:o_v].T.astype(BF16)
    w_v = w[:, o_v:o_r].astype(BF16)
    w_a = jnp.pad(w[:, o_af:o_g], ((0, 0), (0, LANES - 2 * R))).astype(BF16)
    w_at = w_a.T
    wup_f = _pad_rows(gla_a_up_f[0], LANES, 0)
    wup_b = _pad_rows(gla_a_up_b[0], LANES, R)
    wupt_f, wupt_b = wup_f.T, wup_b.T
    bias_f, bias_b = gla_a_bias_f[0].reshape(1, GLA_QK), gla_a_bias_b[0].reshape(1, GLA_QK)
    biast_f, biast_b = bias_f.reshape(GLA_QK, 1), bias_b.reshape(GLA_QK, 1)

    cos32, sin32 = dft_tables(HY_BLOCK)
    a_co, b_co, g_nyq = filter_call(L, hy_filt_w1[0], hy_filt_b1[0], hy_filt_w2[0], hy_filt_b2[0],
                                    hy_filt_w3[0], hy_filt_freq[0], hy_decay[0], cos32, sin32)
    fwd_tab = jnp.concatenate([cos32, sin32], axis=0).astype(BF16)
    inv_tab = jnp.concatenate([cos32, sin32], axis=1).astype(BF16)

    s_f, s_b = ctx_call(ctx, norm1_w[0], csc1, csh1, w_kt, w_v, w_at, wupt_f, biast_f, wupt_b, biast_b)

    u_hy, qvr, gates, kt, a, at = inproj_call(x, norm1_w[0], sc1, sh1, w_main, w_kt, w_a, w_at)
    y_hy = hyena_call(u_hy, hy_conv_w[0], hy_bias[0], fwd_tab, inv_tab, a_co, b_co, g_nyq)
    y_gla = gla_call(qvr, kt, a, at, s_f, s_b, wup_f, bias_f, wup_b, bias_b,
                     wupt_f, biast_f, wupt_b, biast_b, gla_norm_w[0])
    x_new, h2 = merge_call(x, y_hy, y_gla, gates, g1, w_hy_proj[0].astype(BF16), w_gla_proj[0].astype(BF16),
                           w_out[0].astype(BF16), norm2_w[0], sc2, sh2)
    w_ffn = ffn_w_in[0]
    act = ffn_act_call(h2, w_ffn[:, FFN_HIDDEN:].astype(BF16), w_ffn[:, :FFN_HIDDEN].astype(BF16),
                       ffn_dw[0].reshape(9, FFN_HIDDEN), ffn_dw_bias[0])
    return ffn_out_call(act, x_new, g2, ffn_w_out[0].astype(BF16), final_norm_w)
```

```python
import functools
import math

import jax
import jax.numpy as jnp
import numpy as np
from jax import lax
from jax.experimental import pallas as pl
from jax.experimental.pallas import tpu as pltpu

F32 = jnp.float32
BF16 = jnp.bfloat16
HIGHEST = lax.Precision.HIGHEST

D_MODEL = 1024
GRID_W = 64
HY_WIDTH = 512
HY_BANDS = 16
HY_EMB = 1 + 2 * HY_BANDS
HY_FILT_HIDDEN = 64
GLA_HEADS = 4
GLA_DK = 64
GLA_DV = 128
GLA_LOWRANK = 16
GLA_TAU = 16.0
GLA_CHUNK = 64
FFN_HIDDEN = 2816
EPS = 1e-6
HY_COLS = 3 * HY_WIDTH
GLA_QK = GLA_HEADS * GLA_DK
GLA_VW = GLA_HEADS * GLA_DV

LANES = 128
PAIR = 2 * GLA_CHUNK
VMEM_LIMIT = 56 * 1024 * 1024


def _cp(sem):
    return pltpu.CompilerParams(dimension_semantics=sem, vmem_limit_bytes=VMEM_LIMIT)


def _dot(a, b):
    return jnp.dot(a, b, preferred_element_type=F32)


def _dot_hi(a, b):
    return jnp.dot(a, b, preferred_element_type=F32, precision=HIGHEST)


def _sigmoid(x):
    return 1.0 / (1.0 + jnp.exp(-x))


def _silu(x):
    return x * _sigmoid(x)


def _log_sigmoid(x):
    return jnp.minimum(x, 0.0) - jnp.log(1.0 + jnp.exp(-jnp.abs(x)))


def _rms_scale(xf):
    return xf * lax.rsqrt(jnp.mean(xf * xf, axis=-1, keepdims=True) + EPS)


def _ada_kernel(c_ref, w_ref, b_ref, o_ref):
    o_ref[...] = _dot_hi(_silu(c_ref[...]), w_ref[...]) + b_ref[...]


def ada_call(c_rows, w_ada, b_ada):
    rows, d = c_rows.shape
    n = w_ada.shape[1]
    bn = 512
    return pl.pallas_call(
        _ada_kernel,
        grid=(n // bn,),
        in_specs=[pl.BlockSpec((rows, d), lambda j: (0, 0)),
                  pl.BlockSpec((d, bn), lambda j: (0, j)),
                  pl.BlockSpec((1, bn), lambda j: (0, j))],
        out_specs=pl.BlockSpec((rows, bn), lambda j: (0, j)),
        out_shape=jax.ShapeDtypeStruct((rows, n), F32),
        compiler_params=_cp(("arbitrary",)),
        name="ada",
    )(c_rows, w_ada, b_ada.reshape(1, n))


HY_BLOCK = 512


def _filt_mlp_kernel(feats_ref, t_ref, w1_ref, b1_ref, w2_ref, b2_ref, w3_ref, freq_ref, decay_ref, taps_ref):
    freq = freq_ref[...]
    h = jnp.sin(freq * (_dot_hi(feats_ref[...], w1_ref[...]) + b1_ref[...]))
    h = jnp.sin(freq * (_dot_hi(h, w2_ref[...]) + b2_ref[...]))
    k = _dot_hi(h, w3_ref[...]) * jnp.exp(-t_ref[...] * decay_ref[...])
    L = k.shape[0]
    row = lax.broadcasted_iota(jnp.int32, (L, HY_WIDTH), 0)
    k_f = k[:, :HY_WIDTH]
    k_b = jnp.where(row == 0, 0.0, k[:, HY_WIDTH:])
    norm = jnp.sum(jnp.abs(k_f), axis=0, keepdims=True) + jnp.sum(jnp.abs(k_b), axis=0, keepdims=True)
    taps_ref[0:L, :] = k_f / norm
    taps_ref[L:2 * L, :] = k_b / norm


def _filt_blk_kernel(c_ref, s_ref, x_ref, cx_ref, sx_ref, aux_ref, *, M):
    x = x_ref[...]
    cx_ref[0] = _dot_hi(c_ref[...], x)
    sx_ref[0] = _dot_hi(s_ref[...], x)
    row = lax.broadcasted_iota(jnp.int32, (M, HY_WIDTH), 0)
    alt = jnp.sum(jnp.where((row & 1) == 0, x, -x), axis=0, keepdims=True)
    sub = lax.broadcasted_iota(jnp.int32, (8, HY_WIDTH), 0)
    aux_ref[0] = jnp.where(sub == 0, x[0:8], jnp.where(sub == 1, alt, 0.0))


def _filt_seg_kernel(cx1_ref, sx1_ref, aux1_ref, cx2_ref, sx2_ref, aux2_ref, a_ref, b_ref, gn_ref, *, M, P):
    d = pl.program_id(0) - (P - 1)
    f = lax.broadcasted_iota(jnp.int32, (M, 1), 0)
    sf = jnp.where((f & 1) == 0, 1.0, -1.0)
    s = jnp.where(d == 0, 1.0, sf)
    sg1 = jnp.where(d >= 0, -1.0, 1.0)
    sg2 = jnp.where(d >= 1, -1.0, 1.0)
    w = jnp.where(f == 0, 0.5 / M, 1.0 / M)
    x2_0 = aux2_ref[0, 0:1, :]
    a_ref[0] = w * (cx1_ref[0] + s * (cx2_ref[0] - x2_0))
    b_ref[0] = w * (sg1 * sx1_ref[0] + (sg2 * s) * sx2_ref[0])
    gn = (aux1_ref[0, 1:2, :] + aux2_ref[0, 1:2, :] - x2_0) * (0.5 / M)
    gn_ref[0] = jnp.broadcast_to(gn, gn_ref.shape[1:])


def filter_call(L, hy_filt_w1, hy_filt_b1, hy_filt_w2, hy_filt_b2, hy_filt_w3, hy_filt_freq, hy_decay,
                cos32, sin32):
    M = HY_BLOCK
    P = L // M
    t = jnp.linspace(0.0, 1.0, L, dtype=F32)[:, None]
    w = (2.0 * np.pi / L) * jnp.arange(L, dtype=F32)[:, None]
    f = jnp.linspace(1e-4, HY_BANDS - 1, HY_BANDS, dtype=F32)[None, :]
    feats = jnp.concatenate([t, jnp.cos(f * w), -jnp.sin(f * w)], axis=-1)
    feats = jnp.pad(feats, ((0, 0), (0, LANES - HY_EMB)))
    w1 = jnp.pad(hy_filt_w1, ((0, LANES - HY_EMB), (0, 0)))
    hid = HY_FILT_HIDDEN
    taps = pl.pallas_call(
        _filt_mlp_kernel,
        out_shape=jax.ShapeDtypeStruct((2 * L, HY_WIDTH), F32),
        compiler_params=pltpu.CompilerParams(vmem_limit_bytes=VMEM_LIMIT),
        name="filt_mlp",
    )(feats, t, w1, hy_filt_b1.reshape(1, hid), hy_filt_w2, hy_filt_b2.reshape(1, hid), hy_filt_w3,
      hy_filt_freq.reshape(1, hid), hy_decay.reshape(1, 2 * HY_WIDTH))

    blk = lambda n: jax.ShapeDtypeStruct((n, M, HY_WIDTH), F32)
    aux = lambda n: jax.ShapeDtypeStruct((n, 8, HY_WIDTH), F32)
    blk_spec = lambda imap: pl.BlockSpec((1, M, HY_WIDTH), imap)
    aux_spec = lambda imap: pl.BlockSpec((1, 8, HY_WIDTH), imap)
    own = lambda g: (g, 0, 0)
    cx, sx, ax = pl.pallas_call(
        functools.partial(_filt_blk_kernel, M=M),
        grid=(2 * P,),
        in_specs=[pl.BlockSpec((M, M), lambda g: (0, 0)),
                  pl.BlockSpec((M, M), lambda g: (0, 0)),
                  pl.BlockSpec((M, HY_WIDTH), lambda g: (g, 0))],
        out_specs=(blk_spec(own), blk_spec(own), aux_spec(own)),
        out_shape=(blk(2 * P), blk(2 * P), aux(2 * P)),
        compiler_params=_cp(("arbitrary",)),
        name="filt_blk",
    )(cos32, sin32, taps)

    def first(g):
        d = g - (P - 1)
        return (jnp.where(d >= 0, d, P - d), 0, 0)

    def second(g):
        d = g - (P - 1)
        return (jnp.where(d >= 1, d - 1, jnp.where(d == 0, P, P - d - 1)), 0, 0)

    n_seg = 2 * P - 1
    a_co, b_co, g_nyq = pl.pallas_call(
        functools.partial(_filt_seg_kernel, M=M, P=P),
        grid=(n_seg,),
        in_specs=[blk_spec(first), blk_spec(first), aux_spec(first),
                  blk_spec(second), blk_spec(second), aux_spec(second)],
        out_specs=(blk_spec(own), blk_spec(own), aux_spec(own)),
        out_shape=(blk(n_seg), blk(n_seg), aux(n_seg)),
        compiler_params=_cp(("arbitrary",)),
        name="filt_seg",
    )(cx, sx, ax, cx, sx, ax)
    return a_co, b_co, g_nyq


def dft_tables(M):
    n_fft = 2 * M
    idx = (jnp.arange(M, dtype=jnp.int32)[:, None] * jnp.arange(M, dtype=jnp.int32)[None, :]) % n_fft
    ang = idx.astype(F32) * (2.0 * np.pi / n_fft)
    return jnp.cos(ang), jnp.sin(ang)


def _modulated_norm(x_ref, nw_ref, sc_ref, sh_ref):
    y = _rms_scale(x_ref[0].astype(F32)) * nw_ref[...]
    return (y * (1.0 + sc_ref[0]) + sh_ref[0]).astype(BF16)


def _inproj_kernel(x_ref, nw_ref, sc_ref, sh_ref, w_ref, wkt_ref, wa_ref, wat_ref,
                   hy_ref, qvr_ref, gate_ref, kt_ref, a_ref, at_ref, *, tm):
    h = _modulated_norm(x_ref, nw_ref, sc_ref, sh_ref)
    col = 0
    for o_ref in (hy_ref, qvr_ref, gate_ref):
        width = o_ref.shape[-1]
        done = 0
        while done < width:
            step = min(512, width - done)
            o_ref[0, :, done:done + step] = _dot(h, w_ref[:, col:col + step]).astype(o_ref.dtype)
            done += step
            col += step
    nt = (((1,), (1,)), ((), ()))
    kt = lax.dot_general(wkt_ref[...], h, nt, preferred_element_type=F32).astype(BF16)
    for p in range(tm // PAIR):
        kt_ref[0, p] = kt[:, p * PAIR:(p + 1) * PAIR]
    a_ref[0] = _dot(h, wa_ref[...])
    at_ref[0] = lax.dot_general(wat_ref[...], h, nt, preferred_element_type=F32)


def inproj_call(x, norm_w, sc, sh, w_main, w_kt, w_a, w_at, *, tm=512):
    B, L, D = x.shape
    n_main = w_main.shape[1]
    const = lambda b, i: (0, 0)
    tok = lambda b, i: (b, i, 0)
    bvec = lambda b, i: (b, 0, 0)
    return pl.pallas_call(
        functools.partial(_inproj_kernel, tm=tm),
        grid=(B, L // tm),
        in_specs=[pl.BlockSpec((1, tm, D), tok),
                  pl.BlockSpec((1, D), const),
                  pl.BlockSpec((1, 1, D), bvec),
                  pl.BlockSpec((1, 1, D), bvec),
                  pl.BlockSpec((D, n_main), const, pipeline_mode=pl.Buffered(1)),
                  pl.BlockSpec((GLA_QK, D), const),
                  pl.BlockSpec((D, LANES), const),
                  pl.BlockSpec((LANES, D), const)],
        out_specs=(pl.BlockSpec((1, tm, HY_COLS), tok),
                   pl.BlockSpec((1, tm, GLA_QK + 2 * GLA_VW), tok),
                   pl.BlockSpec((1, tm, 2 * D), tok),
                   pl.BlockSpec((1, tm // PAIR, GLA_QK, PAIR), lambda b, i: (b, i, 0, 0)),
                   pl.BlockSpec((1, tm, LANES), tok),
                   pl.BlockSpec((1, LANES, tm), lambda b, i: (b, 0, i))),
        out_shape=(jax.ShapeDtypeStruct((B, L, HY_COLS), BF16),
                   jax.ShapeDtypeStruct((B, L, GLA_QK + 2 * GLA_VW), BF16),
                   jax.ShapeDtypeStruct((B, L, 2 * D), BF16),
                   jax.ShapeDtypeStruct((B, L // PAIR, GLA_QK, PAIR), BF16),
                   jax.ShapeDtypeStruct((B, L, LANES), F32),
                   jax.ShapeDtypeStruct((B, LANES, L), F32)),
        compiler_params=_cp(("arbitrary", "arbitrary")),
        name="inproj",
    )(x, norm_w.reshape(1, D), sc, sh, w_main, w_kt, w_a, w_at)


def _short_conv_rows(u_ref, w_ref, t0, rows, col0, ncol, L):
    halo = 16
    lo = max(t0 - halo, 0)
    hi = min(t0 + rows + halo, L)
    n = hi - lo
    blk = u_ref[0, lo:hi, col0:col0 + ncol].astype(F32)
    ridx = lax.broadcasted_iota(jnp.int32, (n, 1), 0)
    prev = pltpu.roll(blk, 1, axis=0)
    nxt = pltpu.roll(blk, n - 1, axis=0)
    if lo == 0:
        prev = jnp.where(ridx == 0, 0.0, prev)
    if hi == L:
        nxt = jnp.where(ridx == n - 1, 0.0, nxt)
    w = w_ref[:, col0:col0 + ncol]
    out = prev * w[0:1] + blk * w[1:2] + nxt * w[2:3]
    return out[t0 - lo:t0 - lo + rows]


def _hyena_kernel(u_ref, cw_ref, bias_ref, fwd_ref, inv_ref, a_ref, b_ref, gn_ref, o_ref,
                  zb_ref, x0_ref, zp_ref, uv_ref, y_ref, *, L, M, rb, fb):
    W = HY_WIDTH
    P = L // M

    def sign_rows(n):
        t = lax.broadcasted_iota(jnp.int32, (n, 1), 0)
        return jnp.where((t & 1) == 0, 1.0, -1.0)

    znyq = [jnp.zeros((1, W), F32) for _ in range(P)]
    for t0 in range(0, L, rb):
        x1 = _short_conv_rows(u_ref, cw_ref, t0, rb, W, W, L)
        v = _short_conv_rows(u_ref, cw_ref, t0, rb, 2 * W, W, L)
        z = x1 * v
        zb_ref[t0:t0 + rb, :] = z.astype(BF16)
        x0_ref[t0:t0 + rb, :] = _short_conv_rows(u_ref, cw_ref, t0, rb, 0, W, L).astype(BF16)
        znyq[t0 // M] = znyq[t0 // M] + jnp.sum(sign_rows(rb) * z, axis=0, keepdims=True)

    for j in range(P):
        zp_ref[j] = _dot(fwd_ref[...], zb_ref[j * M:(j + 1) * M, :])

    bias = bias_ref[...]
    for i in range(P):
        for f0 in range(0, M, fb):
            u = jnp.zeros((fb, W), F32)
            v = jnp.zeros((fb, W), F32)
            for j in range(P):
                d = i - j + P - 1
                zre = zp_ref[j, f0:f0 + fb, :]
                p = zp_ref[j, M + f0:M + f0 + fb, :]
                a = a_ref[d, f0:f0 + fb, :]
                b = b_ref[d, f0:f0 + fb, :]
                u = u + zre * a + p * b
                v = v + p * a - zre * b
            uv_ref[f0:f0 + fb, :] = u.astype(BF16)
            uv_ref[M + f0:M + f0 + fb, :] = v.astype(BF16)
        y_ref[...] = _dot(inv_ref[...], uv_ref[...])
        ynyq = jnp.zeros((1, W), F32)
        for j in range(P):
            ynyq = ynyq + znyq[j] * gn_ref[i - j + P - 1, 0:1, :]
        for r0 in range(0, M, rb):
            rows = slice(i * M + r0, i * M + r0 + rb)
            z = zb_ref[rows, :].astype(F32)
            y = y_ref[r0:r0 + rb, :] + sign_rows(rb) * ynyq + bias * z
            o_ref[0, rows, :] = (x0_ref[rows, :].astype(F32) * y).astype(o_ref.dtype)


def hyena_call(u_hy, conv_w, hy_bias, fwd_tab, inv_tab, a_co, b_co, g_nyq, *, rb=256, fb=16):
    B, L, _ = u_hy.shape
    W = HY_WIDTH
    M = HY_BLOCK
    n_seg = a_co.shape[0]
    const2 = lambda b: (0, 0)
    const3 = lambda b: (0, 0, 0)
    return pl.pallas_call(
        functools.partial(_hyena_kernel, L=L, M=M, rb=rb, fb=fb),
        grid=(B,),
        in_specs=[pl.BlockSpec((1, L, HY_COLS), lambda b: (b, 0, 0)),
                  pl.BlockSpec((3, HY_COLS), const2),
                  pl.BlockSpec((1, W), const2),
                  pl.BlockSpec((2 * M, M), const2),
                  pl.BlockSpec((M, 2 * M), const2),
                  pl.BlockSpec((n_seg, M, W), const3, pipeline_mode=pl.Buffered(1)),
                  pl.BlockSpec((n_seg, M, W), const3, pipeline_mode=pl.Buffered(1)),
                  pl.BlockSpec((n_seg, 8, W), const3)],
        out_specs=pl.BlockSpec((1, L, W), lambda b: (b, 0, 0)),
        out_shape=jax.ShapeDtypeStruct((B, L, W), BF16),
        scratch_shapes=[pltpu.VMEM((L, W), BF16),
                        pltpu.VMEM((L, W), BF16),
                        pltpu.VMEM((L // M, 2 * M, W), F32),
                        pltpu.VMEM((2 * M, W), BF16),
                        pltpu.VMEM((M, W), F32)],
        compiler_params=_cp(("arbitrary",)),
        name="hyena",
    )(u_hy, conv_w, hy_bias.reshape(1, W), fwd_tab, inv_tab, a_co, b_co, g_nyq)


def _split3(x):
    x1 = x.astype(BF16)
    r = x - x1.astype(F32)
    x2 = r.astype(BF16)
    x3 = (r - x2.astype(F32)).astype(BF16)
    return x1, x2, x3


def _tri_left(m, x):
    x1, x2, x3 = _split3(x)
    return _dot(m, x1) + _dot(m, x2) + _dot(m, x3)


def _tri_right(x, m):
    x1, x2, x3 = _split3(x)
    return _dot(x1, m) + _dot(x2, m) + _dot(x3, m)


def _tri_mats(n, block):
    r = np.arange(n)
    same = (r[:, None] // block) == (r[None, :] // block)
    lower = same & (r[:, None] >= r[None, :])
    upper = same & (r[:, None] <= r[None, :])
    return jnp.asarray(lower, BF16), jnp.asarray(upper, BF16)


def _log_decay_tok(a, wup_ref, bias_ref):
    return _log_sigmoid(_dot_hi(a, wup_ref[...]) + bias_ref[...]) * (1.0 / GLA_TAU)


def _log_decay_feat(at, wupt_ref, biast_ref):
    return _log_sigmoid(_dot_hi(wupt_ref[...], at) + biast_ref[...]) * (1.0 / GLA_TAU)


def _ctx_kernel(x_ref, nw_ref, sc_ref, sh_ref, wkt_ref, wv_ref, wat_ref,
                wuptf_ref, biastf_ref, wuptb_ref, biastb_ref, lo_ref, up_ref, sf_ref, sb_ref):
    h = _modulated_norm(x_ref, nw_ref, sc_ref, sh_ref)
    nt = (((1,), (1,)), ((), ()))
    kt = lax.dot_general(wkt_ref[...], h, nt, preferred_element_type=F32)
    v = _dot(h, wv_ref[...]).astype(BF16)
    at = lax.dot_general(wat_ref[...], h, nt, preferred_element_type=F32)
    la_f = _log_decay_feat(at, wuptf_ref, biastf_ref)
    la_b = _log_decay_feat(at, wuptb_ref, biastb_ref)
    lo = lo_ref[...]
    up = up_ref[...]
    w_f = jnp.exp(_tri_right(la_f, lo) - la_f)
    w_b = jnp.exp(_tri_right(la_b, up) - la_b)
    kf = (kt * w_f).astype(BF16)
    kb = (kt * w_b).astype(BF16)
    for hd in range(GLA_HEADS):
        r0, c0 = hd * GLA_DK, hd * GLA_DV
        sf_ref[0, hd] = _dot(kf[r0:r0 + GLA_DK], v[:, c0:c0 + GLA_DV])
        sb_ref[0, hd] = _dot(kb[r0:r0 + GLA_DK], v[:, c0:c0 + GLA_DV])


def ctx_call(ctx, norm_w, csc, csh, w_kt, w_v, w_at, wupt_f, biast_f, wupt_b, biast_b):
    B, Lc, D = ctx.shape
    lo, up = _tri_mats(Lc, Lc)
    const2 = lambda b: (0, 0)
    const3 = lambda b: (0, 0, 0)
    st = jax.ShapeDtypeStruct((B, GLA_HEADS, GLA_DK, GLA_DV), F32)
    st_spec = pl.BlockSpec((1, GLA_HEADS, GLA_DK, GLA_DV), lambda b: (b, 0, 0, 0))
    return pl.pallas_call(
        _ctx_kernel,
        grid=(B,),
        in_specs=[pl.BlockSpec((1, Lc, D), lambda b: (b, 0, 0)),
                  pl.BlockSpec((1, D), const2),
                  pl.BlockSpec((1, 1, D), const3),
                  pl.BlockSpec((1, 1, D), const3),
                  pl.BlockSpec((GLA_QK, D), const2),
                  pl.BlockSpec((D, GLA_VW), const2),
                  pl.BlockSpec((LANES, D), const2),
                  pl.BlockSpec((GLA_QK, LANES), const2),
                  pl.BlockSpec((GLA_QK, 1), const2),
                  pl.BlockSpec((GLA_QK, LANES), const2),
                  pl.BlockSpec((GLA_QK, 1), const2),
                  pl.BlockSpec((Lc, Lc), const2),
                  pl.BlockSpec((Lc, Lc), const2)],
        out_specs=(st_spec, st_spec),
        out_shape=(st, st),
        compiler_params=_cp(("arbitrary",)),
        name="ctx",
    )(ctx, norm_w.reshape(1, D), csc, csh, w_kt, w_v, w_at, wupt_f, biast_f, wupt_b, biast_b, lo, up)


def _gla_kernel(qvr_ref, kt_ref, a_ref, at_ref, sf_ref, sb_ref,
                wupf_ref, biasf_ref, wupb_ref, biasb_ref,
                wuptf_ref, biastf_ref, wuptb_ref, biastb_ref,
                lo_ref, up_ref, nw_ref, o_ref,
                laf_ref, lab_ref, latf_ref, latb_ref, accf_ref, accb_ref, stf_ref, stb_ref, *, L, rb):
    n_pairs = L // PAIR
    C = GLA_CHUNK
    H, DK, DV = GLA_HEADS, GLA_DK, GLA_DV
    scale = DK ** -0.5

    for t0 in range(0, L, rb):
        a = a_ref[0, t0:t0 + rb, :]
        laf_ref[t0:t0 + rb, :] = _log_decay_tok(a, wupf_ref, biasf_ref)
        lab_ref[t0:t0 + rb, :] = _log_decay_tok(a, wupb_ref, biasb_ref)
    for p in range(n_pairs):
        at = at_ref[0, :, p * PAIR:(p + 1) * PAIR]
        latf_ref[p] = _log_decay_feat(at, wuptf_ref, biastf_ref)
        latb_ref[p] = _log_decay_feat(at, wuptb_ref, biastb_ref)

    lo = lo_ref[...]
    up = up_ref[...]
    rr = lax.broadcasted_iota(jnp.int32, (C, C), 0)
    cc = lax.broadcasted_iota(jnp.int32, (C, C), 1)

    def half_pair(p, la_ref, lat_ref, st_ref, acc_ref, forward):
        m_tok = lo if forward else up
        m_feat = up if forward else lo
        keep = (rr >= cc) if forward else (rr <= cc)
        last = C - 1 if forward else 0
        t0 = pl.multiple_of(p * PAIR, PAIR)
        b_tok = _tri_left(m_tok, la_ref[pl.ds(t0, PAIR), :])
        b_feat = _tri_right(lat_ref[p], m_feat)
        q = qvr_ref[0, pl.ds(t0, PAIR), 0:GLA_QK].astype(F32)
        qe = (q * scale * jnp.exp(b_tok)).astype(BF16)
        kt = kt_ref[0, p].astype(F32)
        ket = (kt * jnp.exp(-b_feat)).astype(BF16)
        v = qvr_ref[0, pl.ds(t0, PAIR), GLA_QK:GLA_QK + GLA_VW]
        for half in ((0, 1) if forward else (1, 0)):
            r0 = half * C
            b_last = b_feat[:, r0 + last:r0 + last + 1]
            kdt = (kt[:, r0:r0 + C] * jnp.exp(b_last - b_feat[:, r0:r0 + C])).astype(BF16)
            g = jnp.exp(b_last)
            for hd in range(H):
                k0, v0 = hd * DK, hd * DV
                qe_h = qe[r0:r0 + C, k0:k0 + DK]
                v_h = v[r0:r0 + C, v0:v0 + DV]
                s_prev = st_ref[hd]
                sc = _dot(qe_h, ket[k0:k0 + DK, r0:r0 + C])
                sc = jnp.where(keep, sc, 0.0).astype(BF16)
                acc_ref[pl.ds(t0 + r0, C), v0:v0 + DV] = _dot(sc, v_h) + _dot(qe_h, s_prev.astype(BF16))
                st_ref[hd] = g[k0:k0 + DK] * s_prev + _dot(kdt[k0:k0 + DK], v_h)

    for hd in range(H):
        stf_ref[hd] = sf_ref[0, hd]
        stb_ref[hd] = sb_ref[0, hd]

    def pair_body(i, carry):
        half_pair(i, laf_ref, latf_ref, stf_ref, accf_ref, True)
        half_pair(n_pairs - 1 - i, lab_ref, latb_ref, stb_ref, accb_ref, False)
        return carry

    lax.fori_loop(0, n_pairs, pair_body, 0, unroll=2)

    for t0 in range(0, L, rb):
        r = qvr_ref[0, t0:t0 + rb, GLA_QK + GLA_VW:GLA_QK + 2 * GLA_VW].astype(F32)
        for hd in range(H):
            v0 = hd * DV
            o = _rms_scale(accf_ref[t0:t0 + rb, v0:v0 + DV] + accb_ref[t0:t0 + rb, v0:v0 + DV])
            y = o * nw_ref[:, v0:v0 + DV] * _silu(r[:, v0:v0 + DV])
            o_ref[0, t0:t0 + rb, v0:v0 + DV] = y.astype(o_ref.dtype)


def gla_call(qvr, kt, a, at, s_f, s_b, wup_f, bias_f, wup_b, bias_b,
             wupt_f, biast_f, wupt_b, biast_b, norm_w, *, rb=256):
    B, L, _ = qvr.shape
    lo, up = _tri_mats(PAIR, GLA_CHUNK)
    const2 = lambda b: (0, 0)
    st_spec = pl.BlockSpec((1, GLA_HEADS, GLA_DK, GLA_DV), lambda b: (b, 0, 0, 0))
    return pl.pallas_call(
        functools.partial(_gla_kernel, L=L, rb=rb),
        grid=(B,),
        in_specs=[pl.BlockSpec((1, L, GLA_QK + 2 * GLA_VW), lambda b: (b, 0, 0)),
                  pl.BlockSpec((1, L // PAIR, GLA_QK, PAIR), lambda b: (b, 0, 0, 0)),
                  pl.BlockSpec((1, L, LANES), lambda b: (b, 0, 0)),
                  pl.BlockSpec((1, LANES, L), lambda b: (b, 0, 0)),
                  st_spec, st_spec,
                  pl.BlockSpec((LANES, GLA_QK), const2),
                  pl.BlockSpec((1, GLA_QK), const2),
                  pl.BlockSpec((LANES, GLA_QK), const2),
                  pl.BlockSpec((1, GLA_QK), const2),
                  pl.BlockSpec((GLA_QK, LANES), const2),
                  pl.BlockSpec((GLA_QK, 1), const2),
                  pl.BlockSpec((GLA_QK, LANES), const2),
                  pl.BlockSpec((GLA_QK, 1), const2),
                  pl.BlockSpec((PAIR, PAIR), const2),
                  pl.BlockSpec((PAIR, PAIR), const2),
                  pl.BlockSpec((1, GLA_VW), const2)],
        out_specs=pl.BlockSpec((1, L, GLA_VW), lambda b: (b, 0, 0)),
        out_shape=jax.ShapeDtypeStruct((B, L, GLA_VW), BF16),
        scratch_shapes=[pltpu.VMEM((L, GLA_QK), F32),
                        pltpu.VMEM((L, GLA_QK), F32),
                        pltpu.VMEM((L // PAIR, GLA_QK, PAIR), F32),
                        pltpu.VMEM((L // PAIR, GLA_QK, PAIR), F32),
                        pltpu.VMEM((L, GLA_VW), F32),
                        pltpu.VMEM((L, GLA_VW), F32),
                        pltpu.VMEM((GLA_HEADS, GLA_DK, GLA_DV), F32),
                        pltpu.VMEM((GLA_HEADS, GLA_DK, GLA_DV), F32)],
        compiler_params=_cp(("arbitrary",)),
        name="gla",
    )(qvr, kt, a, at, s_f, s_b, wup_f, bias_f, wup_b, bias_b,
      wupt_f, biast_f, wupt_b, biast_b, lo, up, norm_w.reshape(1, GLA_VW))


def _merge_kernel(x_ref, yhy_ref, ygla_ref, gate_ref, g1_ref, whp_ref, wgp_ref, wo_ref,
                  nw_ref, sc_ref, sh_ref, xn_ref, h2_ref):
    D = D_MODEL
    g_hy = _sigmoid(gate_ref[0, :, 0:D].astype(F32))
    g_gla = _sigmoid(gate_ref[0, :, D:2 * D].astype(F32))
    merged = g_hy * _dot(yhy_ref[0], whp_ref[...]) + g_gla * _dot(ygla_ref[0], wgp_ref[...])
    mix = _dot(merged.astype(BF16), wo_ref[...])
    xn = x_ref[0] + g1_ref[0] * mix
    xn_ref[0] = xn
    y = _rms_scale(xn) * nw_ref[...]
    h2_ref[0] = (y * (1.0 + sc_ref[0]) + sh_ref[0]).astype(h2_ref.dtype)


def merge_call(x, y_hy, y_gla, gates, g1, w_hy_proj, w_gla_proj, w_out, norm2_w, sc2, sh2, *, tm=512):
    B, L, D = x.shape
    tok = lambda b, i: (b, i, 0)
    bvec = lambda b, i: (b, 0, 0)
    const = lambda b, i: (0, 0)
    return pl.pallas_call(
        _merge_kernel,
        grid=(B, L // tm),
        in_specs=[pl.BlockSpec((1, tm, D), tok),
                  pl.BlockSpec((1, tm, HY_WIDTH), tok),
                  pl.BlockSpec((1, tm, GLA_VW), tok),
                  pl.BlockSpec((1, tm, 2 * D), tok),
                  pl.BlockSpec((1, 1, D), bvec),
                  pl.BlockSpec((HY_WIDTH, D), const),
                  pl.BlockSpec((GLA_VW, D), const),
                  pl.BlockSpec((D, D), const),
                  pl.BlockSpec((1, D), const),
                  pl.BlockSpec((1, 1, D), bvec),
                  pl.BlockSpec((1, 1, D), bvec)],
        out_specs=(pl.BlockSpec((1, tm, D), tok), pl.BlockSpec((1, tm, D), tok)),
        out_shape=(jax.ShapeDtypeStruct((B, L, D), F32), jax.ShapeDtypeStruct((B, L, D), BF16)),
        compiler_params=_cp(("arbitrary", "arbitrary")),
        name="merge",
    )(x, y_hy, y_gla, gates, g1, w_hy_proj, w_gla_proj, w_out, norm2_w.reshape(1, D), sc2, sh2)


FFN_PAD = GRID_W


def _ffn_act_kernel(h_ref, wg0_ref, wcat_ref, dw_ref, dwb_ref, o_ref,
                    g0_ref, l0_ref, r0_ref, g1_ref, l1_ref, r1_ref, s_ref, *, L, rb, n_chunks):
    j = pl.program_id(1)
    hc = dw_ref.shape[1]
    mb = 256
    sub = lax.broadcasted_iota(jnp.int32, (8, hc), 0)
    sets = ((g0_ref, l0_ref, r0_ref), (g1_ref, l1_ref, r1_ref))

    def shifted_copies(dst, t0):
        gate_ref, gl_ref, gr_ref = dst
        for r0 in range(FFN_PAD + t0, FFN_PAD + t0 + mb, GRID_W):
            g = gate_ref[r0:r0 + GRID_W, :]
            left = pltpu.roll(g, 1, axis=0)
            right = pltpu.roll(g, GRID_W - 1, axis=0)
            gl_ref[r0:r0 + 8, :] = jnp.where(sub == 0, 0.0, left[0:8])
            gl_ref[r0 + 8:r0 + GRID_W, :] = left[8:]
            gr_ref[r0:r0 + GRID_W - 8, :] = right[:GRID_W - 8]
            gr_ref[r0 + GRID_W - 8:r0 + GRID_W, :] = jnp.where(sub == 7, 0.0, right[GRID_W - 8:])

    @pl.when(j == 0)
    def _():
        zeros = jnp.zeros((FFN_PAD, hc), F32)
        for ref in sets[0] + sets[1]:
            ref[0:FFN_PAD, :] = zeros
            ref[FFN_PAD + L:FFN_PAD + L + FFN_PAD, :] = zeros
        for t0 in range(0, L, mb):
            g0_ref[FFN_PAD + t0:FFN_PAD + t0 + mb, :] = _dot(h_ref[0, t0:t0 + mb, :], wg0_ref[...])
            shifted_copies(sets[0], t0)

    def step(cur, nxt):
        gate_ref, gl_ref, gr_ref = cur
        dw = dw_ref[...]
        for t0 in range(0, L, mb):
            both = _dot(h_ref[0, t0:t0 + mb, :], wcat_ref[...])
            nxt[0][FFN_PAD + t0:FFN_PAD + t0 + mb, :] = both[:, :hc]
            s_ref[t0:t0 + mb, :] = both[:, hc:]
            shifted_copies(nxt, t0)
            for r0 in range(t0, t0 + mb, rb):
                conv = jnp.broadcast_to(dwb_ref[...], (rb, hc))
                for dy in range(3):
                    rows = pl.ds(FFN_PAD + r0 + (dy - 1) * GRID_W, rb)
                    conv = (conv + gl_ref[rows, :] * dw[3 * dy:3 * dy + 1]
                            + gate_ref[rows, :] * dw[3 * dy + 1:3 * dy + 2]
                            + gr_ref[rows, :] * dw[3 * dy + 2:3 * dy + 3])
                o_ref[0, r0:r0 + rb, :] = (_silu(conv) * s_ref[r0:r0 + rb, :]).astype(o_ref.dtype)

    parity = lax.rem(j, 2)
    pl.when(parity == 0)(lambda: step(sets[0], sets[1]))
    pl.when(parity == 1)(lambda: step(sets[1], sets[0]))


def ffn_act_call(h2, w_gate, w_up, dw, dw_bias, *, hc=256, rb=64):
    B, L, D = h2.shape
    F = w_up.shape[1]
    n_chunks = F // hc
    wg_next = jnp.concatenate([w_gate[:, hc:], w_gate[:, F - hc:]], axis=1)
    w_cat = jnp.stack([wg_next.reshape(D, n_chunks, hc), w_up.reshape(D, n_chunks, hc)], axis=2)
    w_cat = w_cat.reshape(D, 2 * F)
    return pl.pallas_call(
        functools.partial(_ffn_act_kernel, L=L, rb=rb, n_chunks=n_chunks),
        grid=(B, n_chunks),
        in_specs=[pl.BlockSpec((1, L, D), lambda b, j: (b, 0, 0)),
                  pl.BlockSpec((D, hc), lambda b, j: (0, 0)),
                  pl.BlockSpec((D, 2 * hc), lambda b, j: (0, j)),
                  pl.BlockSpec((9, hc), lambda b, j: (0, j)),
                  pl.BlockSpec((1, hc), lambda b, j: (0, j))],
        out_specs=pl.BlockSpec((1, L, hc), lambda b, j: (b, 0, j)),
        out_shape=jax.ShapeDtypeStruct((B, L, F), BF16),
        scratch_shapes=[pltpu.VMEM((L + 2 * FFN_PAD, hc), F32)] * 6 + [pltpu.VMEM((L, hc), F32)],
        compiler_params=_cp(("arbitrary", "arbitrary")),
        name="ffn_act",
    )(h2, w_gate, w_cat, dw, dw_bias.reshape(1, F))


def _ffn_out_kernel(act_ref, xn_ref, g2_ref, wout_ref, fw_ref, o_ref):
    xo = xn_ref[0] + g2_ref[0] * _dot(act_ref[0], wout_ref[...])
    o_ref[0] = _rms_scale(xo) * fw_ref[...]


def ffn_out_call(act, x_new, g2, w_out, final_w, *, tm=512):
    B, L, F = act.shape
    D = w_out.shape[1]
    tok = lambda b, i: (b, i, 0)
    const = lambda b, i: (0, 0)
    return pl.pallas_call(
        _ffn_out_kernel,
        grid=(B, L // tm),
        in_specs=[pl.BlockSpec((1, tm, F), tok),
                  pl.BlockSpec((1, tm, D), tok),
                  pl.BlockSpec((1, 1, D), lambda b, i: (b, 0, 0)),
                  pl.BlockSpec((F, D), const, pipeline_mode=pl.Buffered(1)),
                  pl.BlockSpec((1, D), const)],
        out_specs=pl.BlockSpec((1, tm, D), tok),
        out_shape=jax.ShapeDtypeStruct((B, L, D), F32),
        compiler_params=_cp(("arbitrary", "arbitrary")),
        name="ffn_out",
    )(act, x_new, g2, w_out, final_w.reshape(1, D))


def _pad_rows(w, rows, at=0):
    out = jnp.zeros((rows, w.shape[1]), w.dtype)
    return lax.dynamic_update_slice(out, w, (at, 0))


def kernel(x, c, ctx, c_ctx, w_ada, b_ada, norm1_w, norm2_w, w_in, hy_conv_w, hy_filt_w1, hy_filt_b1,
           hy_filt_w2, hy_filt_b2, hy_filt_w3, hy_filt_freq, hy_decay, hy_bias, gla_a_up_f, gla_a_bias_f,
           gla_a_up_b, gla_a_bias_b, gla_norm_w, w_hy_proj, w_gla_proj, w_out, ffn_w_in, ffn_dw,
           ffn_dw_bias, ffn_w_out, final_norm_w):
    assert w_ada.shape[0] == 1, "single-layer block"
    B, L, D = x.shape
    R = GLA_LOWRANK

    n_rows = -(-(B + 1) // 8) * 8
    c_rows = jnp.zeros((n_rows, D), F32).at[:B].set(c).at[B].set(c_ctx)
    mod = ada_call(c_rows, w_ada[0], b_ada[0])
    sh1, sc1, g1, sh2, sc2, g2 = [m[:B, None, :] for m in jnp.split(mod, 6, axis=-1)]
    csh1, csc1 = mod[B:B + 1, None, 0:D], mod[B:B + 1, None, D:2 * D]

    w = w_in[0]
    o_q = HY_COLS
    o_k = o_q + GLA_QK
    o_v = o_k + GLA_QK
    o_r = o_v + GLA_VW
    o_af = o_r + GLA_VW
    o_ab = o_af + R
    o_g = o_ab + R
    w_main = jnp.concatenate([w[:, :o_k], w[:, o_v:o_af], w[:, o_g:]], axis=1).astype(BF16)
    w_kt = w[:, o_k:o_v].T.astype(BF16)
    w_v = w[:, o_v:o_r].astype(BF16)
    w_a = jnp.pad(w[:, o_af:o_g], ((0, 0), (0, LANES - 2 * R))).astype(BF16)
    w_at = w_a.T
    wup_f = _pad_rows(gla_a_up_f[0], LANES, 0)
    wup_b = _pad_rows(gla_a_up_b[0], LANES, R)
    wupt_f, wupt_b = wup_f.T, wup_b.T
    bias_f, bias_b = gla_a_bias_f[0].reshape(1, GLA_QK), gla_a_bias_b[0].reshape(1, GLA_QK)
    biast_f, biast_b = bias_f.reshape(GLA_QK, 1), bias_b.reshape(GLA_QK, 1)

    cos32, sin32 = dft_tables(HY_BLOCK)
    a_co, b_co, g_nyq = filter_call(L, hy_filt_w1[0], hy_filt_b1[0], hy_filt_w2[0], hy_filt_b2[0],
                                    hy_filt_w3[0], hy_filt_freq[0], hy_decay[0], cos32, sin32)
    fwd_tab = jnp.concatenate([cos32, sin32], axis=0).astype(BF16)
    inv_tab = jnp.concatenate([cos32, sin32], axis=1).astype(BF16)

    s_f, s_b = ctx_call(ctx, norm1_w[0], csc1, csh1, w_kt, w_v, w_at, wupt_f, biast_f, wupt_b, biast_b)

    u_hy, qvr, gates, kt, a, at = inproj_call(x, norm1_w[0], sc1, sh1, w_main, w_kt, w_a, w_at)
    y_hy = hyena_call(u_hy, hy_conv_w[0], hy_bias[0], fwd_tab, inv_tab, a_co, b_co, g_nyq)
    y_gla = gla_call(qvr, kt, a, at, s_f, s_b, wup_f, bias_f, wup_b, bias_b,
                     wupt_f, biast_f, wupt_b, biast_b, gla_norm_w[0])
    x_new, h2 = merge_call(x, y_hy, y_gla, gates, g1, w_hy_proj[0].astype(BF16), w_gla_proj[0].astype(BF16),
                           w_out[0].astype(BF16), norm2_w[0], sc2, sh2)
    w_ffn = ffn_w_in[0]
    act = ffn_act_call(h2, w_ffn[:, FFN_HIDDEN:].astype(BF16), w_ffn[:, :FFN_HIDDEN].astype(BF16),
                       ffn_dw[0].reshape(9, FFN_HIDDEN), ffn_dw_bias[0])
    return ffn_out_call(act, x_new, g2, ffn_w_out[0].astype(BF16), final_norm_w)
```

```python
import functools
import math

import jax
import jax.numpy as jnp
import numpy as np
from jax import lax
from jax.experimental import pallas as pl
from jax.experimental.pallas import tpu as pltpu

F32 = jnp.float32
BF16 = jnp.bfloat16
HIGHEST = lax.Precision.HIGHEST

D_MODEL = 1024
GRID_W = 64
HY_WIDTH = 512
HY_BANDS = 16
HY_EMB = 1 + 2 * HY_BANDS
HY_FILT_HIDDEN = 64
GLA_HEADS = 4
GLA_DK = 64
GLA_DV = 128
GLA_LOWRANK = 16
GLA_TAU = 16.0
GLA_CHUNK = 64
FFN_HIDDEN = 2816
EPS = 1e-6
HY_COLS = 3 * HY_WIDTH
GLA_QK = GLA_HEADS * GLA_DK
GLA_VW = GLA_HEADS * GLA_DV

LANES = 128
PAIR = 2 * GLA_CHUNK
VMEM_LIMIT = 56 * 1024 * 1024


def _cp(sem):
    return pltpu.CompilerParams(dimension_semantics=sem, vmem_limit_bytes=VMEM_LIMIT)


def _dot(a, b):
    return jnp.dot(a, b, preferred_element_type=F32)


def _dot_hi(a, b):
    return jnp.dot(a, b, preferred_element_type=F32, precision=HIGHEST)


def _sigmoid(x):
    return 1.0 / (1.0 + jnp.exp(-x))


def _silu(x):
    return x * _sigmoid(x)


def _log_sigmoid(x):
    return jnp.minimum(x, 0.0) - jnp.log(1.0 + jnp.exp(-jnp.abs(x)))


def _rms_scale(xf):
    return xf * lax.rsqrt(jnp.mean(xf * xf, axis=-1, keepdims=True) + EPS)


def _ada_kernel(c_ref, w_ref, b_ref, o_ref):
    o_ref[...] = _dot_hi(_silu(c_ref[...]), w_ref[...]) + b_ref[...]


def ada_call(c_rows, w_ada, b_ada):
    rows, d = c_rows.shape
    n = w_ada.shape[1]
    bn = 512
    return pl.pallas_call(
        _ada_kernel,
        grid=(n // bn,),
        in_specs=[pl.BlockSpec((rows, d), lambda j: (0, 0)),
                  pl.BlockSpec((d, bn), lambda j: (0, j)),
                  pl.BlockSpec((1, bn), lambda j: (0, j))],
        out_specs=pl.BlockSpec((rows, bn), lambda j: (0, j)),
        out_shape=jax.ShapeDtypeStruct((rows, n), F32),
        compiler_params=_cp(("arbitrary",)),
        name="ada",
    )(c_rows, w_ada, b_ada.reshape(1, n))


HY_BLOCK = 512


def _filt_mlp_kernel(feats_ref, t_ref, w1_ref, b1_ref, w2_ref, b2_ref, w3_ref, freq_ref, decay_ref, taps_ref):
    freq = freq_ref[...]
    h = jnp.sin(freq * (_dot_hi(feats_ref[...], w1_ref[...]) + b1_ref[...]))
    h = jnp.sin(freq * (_dot_hi(h, w2_ref[...]) + b2_ref[...]))
    k = _dot_hi(h, w3_ref[...]) * jnp.exp(-t_ref[...] * decay_ref[...])
    L = k.shape[0]
    row = lax.broadcasted_iota(jnp.int32, (L, HY_WIDTH), 0)
    k_f = k[:, :HY_WIDTH]
    k_b = jnp.where(row == 0, 0.0, k[:, HY_WIDTH:])
    norm = jnp.sum(jnp.abs(k_f), axis=0, keepdims=True) + jnp.sum(jnp.abs(k_b), axis=0, keepdims=True)
    taps_ref[0:L, :] = k_f / norm
    taps_ref[L:2 * L, :] = k_b / norm


def _filt_blk_kernel(c_ref, s_ref, x_ref, cx_ref, sx_ref, aux_ref, *, M):
    x = x_ref[...]
    cx_ref[0] = _dot_hi(c_ref[...], x)
    sx_ref[0] = _dot_hi(s_ref[...], x)
    row = lax.broadcasted_iota(jnp.int32, (M, HY_WIDTH), 0)
    alt = jnp.sum(jnp.where((row & 1) == 0, x, -x), axis=0, keepdims=True)
    sub = lax.broadcasted_iota(jnp.int32, (8, HY_WIDTH), 0)
    aux_ref[0] = jnp.where(sub == 0, x[0:8], jnp.where(sub == 1, alt, 0.0))


def _filt_seg_kernel(cx1_ref, sx1_ref, aux1_ref, cx2_ref, sx2_ref, aux2_ref, a_ref, b_ref, gn_ref, *, M, P):
    d = pl.program_id(0) - (P - 1)
    f = lax.broadcasted_iota(jnp.int32, (M, 1), 0)
    sf = jnp.where((f & 1) == 0, 1.0, -1.0)
    s = jnp.where(d == 0, 1.0, sf)
    sg1 = jnp.where(d >= 0, -1.0, 1.0)
    sg2 = jnp.where(d >= 1, -1.0, 1.0)
    w = jnp.where(f == 0, 0.5 / M, 1.0 / M)
    x2_0 = aux2_ref[0, 0:1, :]
    a_ref[0] = w * (cx1_ref[0] + s * (cx2_ref[0] - x2_0))
    b_ref[0] = w * (sg1 * sx1_ref[0] + (sg2 * s) * sx2_ref[0])
    gn = (aux1_ref[0, 1:2, :] + aux2_ref[0, 1:2, :] - x2_0) * (0.5 / M)
    gn_ref[0] = jnp.broadcast_to(gn, gn_ref.shape[1:])


def filter_call(L, hy_filt_w1, hy_filt_b1, hy_filt_w2, hy_filt_b2, hy_filt_w3, hy_filt_freq, hy_decay,
                cos32, sin32):
    M = HY_BLOCK
    P = L // M
    t = jnp.linspace(0.0, 1.0, L, dtype=F32)[:, None]
    w = (2.0 * np.pi / L) * jnp.arange(L, dtype=F32)[:, None]
    f = jnp.linspace(1e-4, HY_BANDS - 1, HY_BANDS, dtype=F32)[None, :]
    feats = jnp.concatenate([t, jnp.cos(f * w), -jnp.sin(f * w)], axis=-1)
    feats = jnp.pad(feats, ((0, 0), (0, LANES - HY_EMB)))
    w1 = jnp.pad(hy_filt_w1, ((0, LANES - HY_EMB), (0, 0)))
    hid = HY_FILT_HIDDEN
    taps = pl.pallas_call(
        _filt_mlp_kernel,
        out_shape=jax.ShapeDtypeStruct((2 * L, HY_WIDTH), F32),
        compiler_params=pltpu.CompilerParams(vmem_limit_bytes=VMEM_LIMIT),
        name="filt_mlp",
    )(feats, t, w1, hy_filt_b1.reshape(1, hid), hy_filt_w2, hy_filt_b2.reshape(1, hid), hy_filt_w3,
      hy_filt_freq.reshape(1, hid), hy_decay.reshape(1, 2 * HY_WIDTH))

    blk = lambda n: jax.ShapeDtypeStruct((n, M, HY_WIDTH), F32)
    aux = lambda n: jax.ShapeDtypeStruct((n, 8, HY_WIDTH), F32)
    blk_spec = lambda imap: pl.BlockSpec((1, M, HY_WIDTH), imap)
    aux_spec = lambda imap: pl.BlockSpec((1, 8, HY_WIDTH), imap)
    own = lambda g: (g, 0, 0)
    cx, sx, ax = pl.pallas_call(
        functools.partial(_filt_blk_kernel, M=M),
        grid=(2 * P,),
        in_specs=[pl.BlockSpec((M, M), lambda g: (0, 0)),
                  pl.BlockSpec((M, M), lambda g: (0, 0)),
                  pl.BlockSpec((M, HY_WIDTH), lambda g: (g, 0))],
        out_specs=(blk_spec(own), blk_spec(own), aux_spec(own)),
        out_shape=(blk(2 * P), blk(2 * P), aux(2 * P)),
        compiler_params=_cp(("arbitrary",)),
        name="filt_blk",
    )(cos32, sin32, taps)

    def first(g):
        d = g - (P - 1)
        return (jnp.where(d >= 0, d, P - d), 0, 0)

    def second(g):
        d = g - (P - 1)
        return (jnp.where(d >= 1, d - 1, jnp.where(d == 0, P, P - d - 1)), 0, 0)

    n_seg = 2 * P - 1
    a_co, b_co, g_nyq = pl.pallas_call(
        functools.partial(_filt_seg_kernel, M=M, P=P),
        grid=(n_seg,),
        in_specs=[blk_spec(first), blk_spec(first), aux_spec(first),
                  blk_spec(second), blk_spec(second), aux_spec(second)],
        out_specs=(blk_spec(own), blk_spec(own), aux_spec(own)),
        out_shape=(blk(n_seg), blk(n_seg), aux(n_seg)),
        compiler_params=_cp(("arbitrary",)),
        name="filt_seg",
    )(cx, sx, ax, cx, sx, ax)
    return a_co, b_co, g_nyq


def dft_tables(M):
    n_fft = 2 * M
    idx = (jnp.arange(M, dtype=jnp.int32)[:, None] * jnp.arange(M, dtype=jnp.int32)[None, :]) % n_fft
    ang = idx.astype(F32) * (2.0 * np.pi / n_fft)
    return jnp.cos(ang), jnp.sin(ang)


def _modulated_norm(x_ref, nw_ref, sc_ref, sh_ref):
    y = _rms_scale(x_ref[0].astype(F32)) * nw_ref[...]
    return (y * (1.0 + sc_ref[0]) + sh_ref[0]).astype(BF16)


def _inproj_kernel(x_ref, nw_ref, sc_ref, sh_ref, w_ref, wkt_ref, wa_ref, wat_ref,
                   hy_ref, qvr_ref, gate_ref, kt_ref, a_ref, at_ref, *, tm):
    h = _modulated_norm(x_ref, nw_ref, sc_ref, sh_ref)
    col = 0
    for o_ref in (hy_ref, qvr_ref, gate_ref):
        width = o_ref.shape[-1]
        done = 0
        while done < width:
            step = min(512, width - done)
            o_ref[0, :, done:done + step] = _dot(h, w_ref[:, col:col + step]).astype(o_ref.dtype)
            done += step
            col += step
    nt = (((1,), (1,)), ((), ()))
    kt = lax.dot_general(wkt_ref[...], h, nt, preferred_element_type=F32).astype(BF16)
    for p in range(tm // PAIR):
        kt_ref[0, p] = kt[:, p * PAIR:(p + 1) * PAIR]
    a_ref[0] = _dot(h, wa_ref[...])
    at_ref[0] = lax.dot_general(wat_ref[...], h, nt, preferred_element_type=F32)


def inproj_call(x, norm_w, sc, sh, w_main, w_kt, w_a, w_at, *, tm=512):
    B, L, D = x.shape
    n_main = w_main.shape[1]
    const = lambda b, i: (0, 0)
    tok = lambda b, i: (b, i, 0)
    bvec = lambda b, i: (b, 0, 0)
    return pl.pallas_call(
        functools.partial(_inproj_kernel, tm=tm),
        grid=(B, L // tm),
        in_specs=[pl.BlockSpec((1, tm, D), tok),
                  pl.BlockSpec((1, D), const),
                  pl.BlockSpec((1, 1, D), bvec),
                  pl.BlockSpec((1, 1, D), bvec),
                  pl.BlockSpec((D, n_main), const, pipeline_mode=pl.Buffered(1)),
                  pl.BlockSpec((GLA_QK, D), const),
                  pl.BlockSpec((D, LANES), const),
                  pl.BlockSpec((LANES, D), const)],
        out_specs=(pl.BlockSpec((1, tm, HY_COLS), tok),
                   pl.BlockSpec((1, tm, GLA_QK + 2 * GLA_VW), tok),
                   pl.BlockSpec((1, tm, 2 * D), tok),
                   pl.BlockSpec((1, tm // PAIR, GLA_QK, PAIR), lambda b, i: (b, i, 0, 0)),
                   pl.BlockSpec((1, tm, LANES), tok),
                   pl.BlockSpec((1, LANES, tm), lambda b, i: (b, 0, i))),
        out_shape=(jax.ShapeDtypeStruct((B, L, HY_COLS), BF16),
                   jax.ShapeDtypeStruct((B, L, GLA_QK + 2 * GLA_VW), BF16),
                   jax.ShapeDtypeStruct((B, L, 2 * D), BF16),
                   jax.ShapeDtypeStruct((B, L // PAIR, GLA_QK, PAIR), BF16),
                   jax.ShapeDtypeStruct((B, L, LANES), F32),
                   jax.ShapeDtypeStruct((B, LANES, L), F32)),
        compiler_params=_cp(("arbitrary", "arbitrary")),
        name="inproj",
    )(x, norm_w.reshape(1, D), sc, sh, w_main, w_kt, w_a, w_at)


def _short_conv_rows(u_ref, w_ref, t0, rows, col0, ncol, L):
    halo = 16
    lo = max(t0 - halo, 0)
    hi = min(t0 + rows + halo, L)
    n = hi - lo
    blk = u_ref[0, lo:hi, col0:col0 + ncol].astype(F32)
    ridx = lax.broadcasted_iota(jnp.int32, (n, 1), 0)
    prev = pltpu.roll(blk, 1, axis=0)
    nxt = pltpu.roll(blk, n - 1, axis=0)
    if lo == 0:
        prev = jnp.where(ridx == 0, 0.0, prev)
    if hi == L:
        nxt = jnp.where(ridx == n - 1, 0.0, nxt)
    w = w_ref[:, col0:col0 + ncol]
    out = prev * w[0:1] + blk * w[1:2] + nxt * w[2:3]
    return out[t0 - lo:t0 - lo + rows]


def _hyena_kernel(u_ref, cw_ref, bias_ref, fwd_ref, inv_ref, a_ref, b_ref, gn_ref, o_ref,
                  zb_ref, x0_ref, zp_ref, uv_ref, y_ref, *, L, M, rb, fb):
    W = HY_WIDTH
    P = L // M

    def sign_rows(n):
        t = lax.broadcasted_iota(jnp.int32, (n, 1), 0)
        return jnp.where((t & 1) == 0, 1.0, -1.0)

    znyq = [jnp.zeros((1, W), F32) for _ in range(P)]
    for t0 in range(0, L, rb):
        x1 = _short_conv_rows(u_ref, cw_ref, t0, rb, W, W, L)
        v = _short_conv_rows(u_ref, cw_ref, t0, rb, 2 * W, W, L)
        z = x1 * v
        zb_ref[t0:t0 + rb, :] = z.astype(BF16)
        x0_ref[t0:t0 + rb, :] = _short_conv_rows(u_ref, cw_ref, t0, rb, 0, W, L).astype(BF16)
        znyq[t0 // M] = znyq[t0 // M] + jnp.sum(sign_rows(rb) * z, axis=0, keepdims=True)

    for j in range(P):
        zp_ref[j] = _dot(fwd_ref[...], zb_ref[j * M:(j + 1) * M, :])

    bias = bias_ref[...]
    for i in range(P):
        for f0 in range(0, M, fb):
            u = jnp.zeros((fb, W), F32)
            v = jnp.zeros((fb, W), F32)
            for j in range(P):
                d = i - j + P - 1
                zre = zp_ref[j, f0:f0 + fb, :]
                p = zp_ref[j, M + f0:M + f0 + fb, :]
                a = a_ref[d, f0:f0 + fb, :]
                b = b_ref[d, f0:f0 + fb, :]
                u = u + zre * a + p * b
                v = v + p * a - zre * b
            uv_ref[f0:f0 + fb, :] = u.astype(BF16)
            uv_ref[M + f0:M + f0 + fb, :] = v.astype(BF16)
        y_ref[...] = _dot(inv_ref[...], uv_ref[...])
        ynyq = jnp.zeros((1, W), F32)
        for j in range(P):
            ynyq = ynyq + znyq[j] * gn_ref[i - j + P - 1, 0:1, :]
        for r0 in range(0, M, rb):
            rows = slice(i * M + r0, i * M + r0 + rb)
            z = zb_ref[rows, :].astype(F32)
            y = y_ref[r0:r0 + rb, :] + sign_rows(rb) * ynyq + bias * z
            o_ref[0, rows, :] = (x0_ref[rows, :].astype(F32) * y).astype(o_ref.dtype)


def hyena_call(u_hy, conv_w, hy_bias, fwd_tab, inv_tab, a_co, b_co, g_nyq, *, rb=256, fb=16):
    B, L, _ = u_hy.shape
    W = HY_WIDTH
    M = HY_BLOCK
    n_seg = a_co.shape[0]
    const2 = lambda b: (0, 0)
    const3 = lambda b: (0, 0, 0)
    return pl.pallas_call(
        functools.partial(_hyena_kernel, L=L, M=M, rb=rb, fb=fb),
        grid=(B,),
        in_specs=[pl.BlockSpec((1, L, HY_COLS), lambda b: (b, 0, 0)),
                  pl.BlockSpec((3, HY_COLS), const2),
                  pl.BlockSpec((1, W), const2),
                  pl.BlockSpec((2 * M, M), const2),
                  pl.BlockSpec((M, 2 * M), const2),
                  pl.BlockSpec((n_seg, M, W), const3, pipeline_mode=pl.Buffered(1)),
                  pl.BlockSpec((n_seg, M, W), const3, pipeline_mode=pl.Buffered(1)),
                  pl.BlockSpec((n_seg, 8, W), const3)],
        out_specs=pl.BlockSpec((1, L, W), lambda b: (b, 0, 0)),
        out_shape=jax.ShapeDtypeStruct((B, L, W), BF16),
        scratch_shapes=[pltpu.VMEM((L, W), BF16),
                        pltpu.VMEM((L, W), BF16),
                        pltpu.VMEM((L // M, 2 * M, W), F32),
                        pltpu.VMEM((2 * M, W), BF16),
                        pltpu.VMEM((M, W), F32)],
        compiler_params=_cp(("arbitrary",)),
        name="hyena",
    )(u_hy, conv_w, hy_bias.reshape(1, W), fwd_tab, inv_tab, a_co, b_co, g_nyq)


def _split3(x):
    x1 = x.astype(BF16)
    r = x - x1.astype(F32)
    x2 = r.astype(BF16)
    x3 = (r - x2.astype(F32)).astype(BF16)
    return x1, x2, x3


def _tri_left(m, x):
    x1, x2, x3 = _split3(x)
    return _dot(m, x1) + _dot(m, x2) + _dot(m, x3)


def _tri_right(x, m):
    x1, x2, x3 = _split3(x)
    return _dot(x1, m) + _dot(x2, m) + _dot(x3, m)


def _tri_mats(n, block):
    r = np.arange(n)
    same = (r[:, None] // block) == (r[None, :] // block)
    lower = same & (r[:, None] >= r[None, :])
    upper = same & (r[:, None] <= r[None, :])
    return jnp.asarray(lower, BF16), jnp.asarray(upper, BF16)


def _log_decay_tok(a, wup_ref, bias_ref):
    return _log_sigmoid(_dot_hi(a, wup_ref[...]) + bias_ref[...]) * (1.0 / GLA_TAU)


def _log_decay_feat(at, wupt_ref, biast_ref):
    return _log_sigmoid(_dot_hi(wupt_ref[...], at) + biast_ref[...]) * (1.0 / GLA_TAU)


def _ctx_kernel(x_ref, nw_ref, sc_ref, sh_ref, wkt_ref, wv_ref, wat_ref,
                wuptf_ref, biastf_ref, wuptb_ref, biastb_ref, lo_ref, up_ref, sf_ref, sb_ref):
    h = _modulated_norm(x_ref, nw_ref, sc_ref, sh_ref)
    nt = (((1,), (1,)), ((), ()))
    kt = lax.dot_general(wkt_ref[...], h, nt, preferred_element_type=F32)
    v = _dot(h, wv_ref[...]).astype(BF16)
    at = lax.dot_general(wat_ref[...], h, nt, preferred_element_type=F32)
    la_f = _log_decay_feat(at, wuptf_ref, biastf_ref)
    la_b = _log_decay_feat(at, wuptb_ref, biastb_ref)
    lo = lo_ref[...]
    up = up_ref[...]
    w_f = jnp.exp(_tri_right(la_f, lo) - la_f)
    w_b = jnp.exp(_tri_right(la_b, up) - la_b)
    kf = (kt * w_f).astype(BF16)
    kb = (kt * w_b).astype(BF16)
    for hd in range(GLA_HEADS):
        r0, c0 = hd * GLA_DK, hd * GLA_DV
        sf_ref[0, hd] = _dot(kf[r0:r0 + GLA_DK], v[:, c0:c0 + GLA_DV])
        sb_ref[0, hd] = _dot(kb[r0:r0 + GLA_DK], v[:, c0:c0 + GLA_DV])


def ctx_call(ctx, norm_w, csc, csh, w_kt, w_v, w_at, wupt_f, biast_f, wupt_b, biast_b):
    B, Lc, D = ctx.shape
    lo, up = _tri_mats(Lc, Lc)
    const2 = lambda b: (0, 0)
    const3 = lambda b: (0, 0, 0)
    st = jax.ShapeDtypeStruct((B, GLA_HEADS, GLA_DK, GLA_DV), F32)
    st_spec = pl.BlockSpec((1, GLA_HEADS, GLA_DK, GLA_DV), lambda b: (b, 0, 0, 0))
    return pl.pallas_call(
        _ctx_kernel,
        grid=(B,),
        in_specs=[pl.BlockSpec((1, Lc, D), lambda b: (b, 0, 0)),
                  pl.BlockSpec((1, D), const2),
                  pl.BlockSpec((1, 1, D), const3),
                  pl.BlockSpec((1, 1, D), const3),
                  pl.BlockSpec((GLA_QK, D), const2),
                  pl.BlockSpec((D, GLA_VW), const2),
                  pl.BlockSpec((LANES, D), const2),
                  pl.BlockSpec((GLA_QK, LANES), const2),
                  pl.BlockSpec((GLA_QK, 1), const2),
                  pl.BlockSpec((GLA_QK, LANES), const2),
                  pl.BlockSpec((GLA_QK, 1), const2),
                  pl.BlockSpec((Lc, Lc), const2),
                  pl.BlockSpec((Lc, Lc), const2)],
        out_specs=(st_spec, st_spec),
        out_shape=(st, st),
        compiler_params=_cp(("arbitrary",)),
        name="ctx",
    )(ctx, norm_w.reshape(1, D), csc, csh, w_kt, w_v, w_at, wupt_f, biast_f, wupt_b, biast_b, lo, up)


def _gla_kernel(qvr_ref, kt_ref, a_ref, at_ref, sf_ref, sb_ref,
                wupf_ref, biasf_ref, wupb_ref, biasb_ref,
                wuptf_ref, biastf_ref, wuptb_ref, biastb_ref,
                lo_ref, up_ref, nw_ref, o_ref,
                laf_ref, lab_ref, latf_ref, latb_ref, accf_ref, accb_ref, stf_ref, stb_ref, *, L, rb):
    n_pairs = L // PAIR
    C = GLA_CHUNK
    H, DK, DV = GLA_HEADS, GLA_DK, GLA_DV
    scale = DK ** -0.5

    for t0 in range(0, L, rb):
        a = a_ref[0, t0:t0 + rb, :]
        laf_ref[t0:t0 + rb, :] = _log_decay_tok(a, wupf_ref, biasf_ref)
        lab_ref[t0:t0 + rb, :] = _log_decay_tok(a, wupb_ref, biasb_ref)
    for p in range(n_pairs):
        at = at_ref[0, :, p * PAIR:(p + 1) * PAIR]
        latf_ref[p] = _log_decay_feat(at, wuptf_ref, biastf_ref)
        latb_ref[p] = _log_decay_feat(at, wuptb_ref, biastb_ref)

    lo = lo_ref[...]
    up = up_ref[...]
    rr = lax.broadcasted_iota(jnp.int32, (C, C), 0)
    cc = lax.broadcasted_iota(jnp.int32, (C, C), 1)

    def half_pair(p, la_ref, lat_ref, st_ref, acc_ref, forward):
        m_tok = lo if forward else up
        m_feat = up if forward else lo
        keep = (rr >= cc) if forward else (rr <= cc)
        last = C - 1 if forward else 0
        t0 = pl.multiple_of(p * PAIR, PAIR)
        b_tok = _tri_left(m_tok, la_ref[pl.ds(t0, PAIR), :])
        b_feat = _tri_right(lat_ref[p], m_feat)
        q = qvr_ref[0, pl.ds(t0, PAIR), 0:GLA_QK].astype(F32)
        qe = (q * scale * jnp.exp(b_tok)).astype(BF16)
        kt = kt_ref[0, p].astype(F32)
        ket = (kt * jnp.exp(-b_feat)).astype(BF16)
        v = qvr_ref[0, pl.ds(t0, PAIR), GLA_QK:GLA_QK + GLA_VW]
        for half in ((0, 1) if forward else (1, 0)):
            r0 = half * C
            b_last = b_feat[:, r0 + last:r0 + last + 1]
            kdt = (kt[:, r0:r0 + C] * jnp.exp(b_last - b_feat[:, r0:r0 + C])).astype(BF16)
            g = jnp.exp(b_last)
            for hd in range(H):
                k0, v0 = hd * DK, hd * DV
                qe_h = qe[r0:r0 + C, k0:k0 + DK]
                v_h = v[r0:r0 + C, v0:v0 + DV]
                s_prev = st_ref[hd]
                sc = _dot(qe_h, ket[k0:k0 + DK, r0:r0 + C])
                sc = jnp.where(keep, sc, 0.0).astype(BF16)
                acc_ref[pl.ds(t0 + r0, C), v0:v0 + DV] = _dot(sc, v_h) + _dot(qe_h, s_prev.astype(BF16))
                st_ref[hd] = g[k0:k0 + DK] * s_prev + _dot(kdt[k0:k0 + DK], v_h)

    for hd in range(H):
        stf_ref[hd] = sf_ref[0, hd]
        stb_ref[hd] = sb_ref[0, hd]

    def pair_body(i, carry):
        half_pair(i, laf_ref, latf_ref, stf_ref, accf_ref, True)
        half_pair(n_pairs - 1 - i, lab_ref, latb_ref, stb_ref, accb_ref, False)
        return carry

    lax.fori_loop(0, n_pairs, pair_body, 0, unroll=2)

    for t0 in range(0, L, rb):
        r = qvr_ref[0, t0:t0 + rb, GLA_QK + GLA_VW:GLA_QK + 2 * GLA_VW].astype(F32)
        for hd in range(H):
            v0 = hd * DV
            o = _rms_scale(accf_ref[t0:t0 + rb, v0:v0 + DV] + accb_ref[t0:t0 + rb, v0:v0 + DV])
            y = o * nw_ref[:, v0:v0 + DV] * _silu(r[:, v0:v0 + DV])
            o_ref[0, t0:t0 + rb, v0:v0 + DV] = y.astype(o_ref.dtype)


def gla_call(qvr, kt, a, at, s_f, s_b, wup_f, bias_f, wup_b, bias_b,
             wupt_f, biast_f, wupt_b, biast_b, norm_w, *, rb=256):
    B, L, _ = qvr.shape
    lo, up = _tri_mats(PAIR, GLA_CHUNK)
    const2 = lambda b: (0, 0)
    st_spec = pl.BlockSpec((1, GLA_HEADS, GLA_DK, GLA_DV), lambda b: (b, 0, 0, 0))
    return pl.pallas_call(
        functools.partial(_gla_kernel, L=L, rb=rb),
        grid=(B,),
        in_specs=[pl.BlockSpec((1, L, GLA_QK + 2 * GLA_VW), lambda b: (b, 0, 0)),
                  pl.BlockSpec((1, L // PAIR, GLA_QK, PAIR), lambda b: (b, 0, 0, 0)),
                  pl.BlockSpec((1, L, LANES), lambda b: (b, 0, 0)),
                  pl.BlockSpec((1, LANES, L), lambda b: (b, 0, 0)),
                  st_spec, st_spec,
                  pl.BlockSpec((LANES, GLA_QK), const2),
                  pl.BlockSpec((1, GLA_QK), const2),
                  pl.BlockSpec((LANES, GLA_QK), const2),
                  pl.BlockSpec((1, GLA_QK), const2),
                  pl.BlockSpec((GLA_QK, LANES), const2),
                  pl.BlockSpec((GLA_QK, 1), const2),
                  pl.BlockSpec((GLA_QK, LANES), const2),
                  pl.BlockSpec((GLA_QK, 1), const2),
                  pl.BlockSpec((PAIR, PAIR), const2),
                  pl.BlockSpec((PAIR, PAIR), const2),
                  pl.BlockSpec((1, GLA_VW), const2)],
        out_specs=pl.BlockSpec((1, L, GLA_VW), lambda b: (b, 0, 0)),
        out_shape=jax.ShapeDtypeStruct((B, L, GLA_VW), BF16),
        scratch_shapes=[pltpu.VMEM((L, GLA_QK), F32),
                        pltpu.VMEM((L, GLA_QK), F32),
                        pltpu.VMEM((L // PAIR, GLA_QK, PAIR), F32),
                        pltpu.VMEM((L // PAIR, GLA_QK, PAIR), F32),
                        pltpu.VMEM((L, GLA_VW), F32),
                        pltpu.VMEM((L, GLA_VW), F32),
                        pltpu.VMEM((GLA_HEADS, GLA_DK, GLA_DV), F32),
                        pltpu.VMEM((GLA_HEADS, GLA_DK, GLA_DV), F32)],
        compiler_params=_cp(("arbitrary",)),
        name="gla",
    )(qvr, kt, a, at, s_f, s_b, wup_f, bias_f, wup_b, bias_b,
      wupt_f, biast_f, wupt_b, biast_b, lo, up, norm_w.reshape(1, GLA_VW))


def _merge_kernel(x_ref, yhy_ref, ygla_ref, gate_ref, g1_ref, whp_ref, wgp_ref, wo_ref,
                  nw_ref, sc_ref, sh_ref, xn_ref, h2_ref):
    D = D_MODEL
    g_hy = _sigmoid(gate_ref[0, :, 0:D].astype(F32))
    g_gla = _sigmoid(gate_ref[0, :, D:2 * D].astype(F32))
    merged = g_hy * _dot(yhy_ref[0], whp_ref[...]) + g_gla * _dot(ygla_ref[0], wgp_ref[...])
    mix = _dot(merged.astype(BF16), wo_ref[...])
    xn = x_ref[0] + g1_ref[0] * mix
    xn_ref[0] = xn
    y = _rms_scale(xn) * nw_ref[...]
    h2_ref[0] = (y * (1.0 + sc_ref[0]) + sh_ref[0]).astype(h2_ref.dtype)


def merge_call(x, y_hy, y_gla, gates, g1, w_hy_proj, w_gla_proj, w_out, norm2_w, sc2, sh2, *, tm=512):
    B, L, D = x.shape
    tok = lambda b, i: (b, i, 0)
    bvec = lambda b, i: (b, 0, 0)
    const = lambda b, i: (0, 0)
    return pl.pallas_call(
        _merge_kernel,
        grid=(B, L // tm),
        in_specs=[pl.BlockSpec((1, tm, D), tok),
                  pl.BlockSpec((1, tm, HY_WIDTH), tok),
                  pl.BlockSpec((1, tm, GLA_VW), tok),
                  pl.BlockSpec((1, tm, 2 * D), tok),
                  pl.BlockSpec((1, 1, D), bvec),
                  pl.BlockSpec((HY_WIDTH, D), const),
                  pl.BlockSpec((GLA_VW, D), const),
                  pl.BlockSpec((D, D), const),
                  pl.BlockSpec((1, D), const),
                  pl.BlockSpec((1, 1, D), bvec),
                  pl.BlockSpec((1, 1, D), bvec)],
        out_specs=(pl.BlockSpec((1, tm, D), tok), pl.BlockSpec((1, tm, D), tok)),
        out_shape=(jax.ShapeDtypeStruct((B, L, D), F32), jax.ShapeDtypeStruct((B, L, D), BF16)),
        compiler_params=_cp(("arbitrary", "arbitrary")),
        name="merge",
    )(x, y_hy, y_gla, gates, g1, w_hy_proj, w_gla_proj, w_out, norm2_w.reshape(1, D), sc2, sh2)


FFN_PAD = GRID_W


def _ffn_act_kernel(h_ref, wg_ref, wu_ref, dw_ref, dwb_ref, o_ref,
                    g0_ref, l0_ref, r0_ref, g1_ref, l1_ref, r1_ref, s_ref, *, L, rb, n_chunks):
    j = pl.program_id(1)
    hc = dw_ref.shape[1]
    mb = 128
    sub = lax.broadcasted_iota(jnp.int32, (8, hc), 0)
    sets = ((g0_ref, l0_ref, r0_ref), (g1_ref, l1_ref, r1_ref))

    def project(dst, chunk, t0):
        gate_ref, gl_ref, gr_ref = dst
        gate_ref[FFN_PAD + t0:FFN_PAD + t0 + mb, :] = _dot(h_ref[0, t0:t0 + mb, :], wg_ref[chunk])
        for r0 in range(FFN_PAD + t0, FFN_PAD + t0 + mb, GRID_W):
            g = gate_ref[r0:r0 + GRID_W, :]
            left = pltpu.roll(g, 1, axis=0)
            right = pltpu.roll(g, GRID_W - 1, axis=0)
            gl_ref[r0:r0 + 8, :] = jnp.where(sub == 0, 0.0, left[0:8])
            gl_ref[r0 + 8:r0 + GRID_W, :] = left[8:]
            gr_ref[r0:r0 + GRID_W - 8, :] = right[:GRID_W - 8]
            gr_ref[r0 + GRID_W - 8:r0 + GRID_W, :] = jnp.where(sub == 7, 0.0, right[GRID_W - 8:])

    @pl.when(j == 0)
    def _():
        zeros = jnp.zeros((FFN_PAD, hc), F32)
        for ref in sets[0] + sets[1]:
            ref[0:FFN_PAD, :] = zeros
            ref[FFN_PAD + L:FFN_PAD + L + FFN_PAD, :] = zeros
        for t0 in range(0, L, mb):
            project(sets[0], 0, t0)

    def step(cur, nxt):
        gate_ref, gl_ref, gr_ref = cur
        dw = dw_ref[...]
        nxt_chunk = jnp.minimum(j + 1, n_chunks - 1)
        for t0 in range(0, L, mb):
            project(nxt, nxt_chunk, t0)
            s_ref[t0:t0 + mb, :] = _dot(h_ref[0, t0:t0 + mb, :], wu_ref[j])
            for r0 in range(t0, t0 + mb, rb):
                conv = jnp.broadcast_to(dwb_ref[...], (rb, hc))
                for dy in range(3):
                    rows = pl.ds(FFN_PAD + r0 + (dy - 1) * GRID_W, rb)
                    conv = (conv + gl_ref[rows, :] * dw[3 * dy:3 * dy + 1]
                            + gate_ref[rows, :] * dw[3 * dy + 1:3 * dy + 2]
                            + gr_ref[rows, :] * dw[3 * dy + 2:3 * dy + 3])
                o_ref[0, 0, r0:r0 + rb, :] = (_silu(conv) * s_ref[r0:r0 + rb, :]).astype(o_ref.dtype)

    parity = lax.rem(j, 2)
    pl.when(parity == 0)(lambda: step(sets[0], sets[1]))
    pl.when(parity == 1)(lambda: step(sets[1], sets[0]))


def ffn_act_call(h2, w_gate, w_up, dw, dw_bias, *, hc=256, rb=64):
    B, L, D = h2.shape
    F = w_up.shape[1]
    n_chunks = F // hc
    wg = w_gate.reshape(D, n_chunks, hc).transpose(1, 0, 2)
    wu = w_up.reshape(D, n_chunks, hc).transpose(1, 0, 2)
    resident = pl.BlockSpec((n_chunks, D, hc), lambda b, j: (0, 0, 0), pipeline_mode=pl.Buffered(1))
    return pl.pallas_call(
        functools.partial(_ffn_act_kernel, L=L, rb=rb, n_chunks=n_chunks),
        grid=(B, n_chunks),
        in_specs=[pl.BlockSpec((1, L, D), lambda b, j: (b, 0, 0)),
                  resident, resident,
                  pl.BlockSpec((9, hc), lambda b, j: (0, j)),
                  pl.BlockSpec((1, hc), lambda b, j: (0, j))],
        out_specs=pl.BlockSpec((1, 1, L, hc), lambda b, j: (b, j, 0, 0)),
        out_shape=jax.ShapeDtypeStruct((B, n_chunks, L, hc), BF16),
        scratch_shapes=[pltpu.VMEM((L + 2 * FFN_PAD, hc), F32)] * 6 + [pltpu.VMEM((L, hc), F32)],
        compiler_params=_cp(("arbitrary", "arbitrary")),
        name="ffn_act",
    )(h2, wg, wu, dw, dw_bias.reshape(1, F))


def _ffn_out_kernel(act_ref, xn_ref, g2_ref, wout_ref, fw_ref, o_ref):
    act = jnp.concatenate([act_ref[0, j] for j in range(act_ref.shape[1])], axis=1)
    xo = xn_ref[0] + g2_ref[0] * _dot(act, wout_ref[...])
    o_ref[0] = _rms_scale(xo) * fw_ref[...]


def ffn_out_call(act, x_new, g2, w_out, final_w, *, tm=512):
    B, n_chunks, L, hc = act.shape
    F, D = w_out.shape
    tok = lambda b, i: (b, i, 0)
    const = lambda b, i: (0, 0)
    return pl.pallas_call(
        _ffn_out_kernel,
        grid=(B, L // tm),
        in_specs=[pl.BlockSpec((1, n_chunks, tm, hc), lambda b, i: (b, 0, i, 0)),
                  pl.BlockSpec((1, tm, D), tok),
                  pl.BlockSpec((1, 1, D), lambda b, i: (b, 0, 0)),
                  pl.BlockSpec((F, D), const, pipeline_mode=pl.Buffered(1)),
                  pl.BlockSpec((1, D), const)],
        out_specs=pl.BlockSpec((1, tm, D), tok),
        out_shape=jax.ShapeDtypeStruct((B, L, D), F32),
        compiler_params=_cp(("arbitrary", "arbitrary")),
        name="ffn_out",
    )(act, x_new, g2, w_out, final_w.reshape(1, D))


def _pad_rows(w, rows, at=0):
    out = jnp.zeros((rows, w.shape[1]), w.dtype)
    return lax.dynamic_update_slice(out, w, (at, 0))


def kernel(x, c, ctx, c_ctx, w_ada, b_ada, norm1_w, norm2_w, w_in, hy_conv_w, hy_filt_w1, hy_filt_b1,
           hy_filt_w2, hy_filt_b2, hy_filt_w3, hy_filt_freq, hy_decay, hy_bias, gla_a_up_f, gla_a_bias_f,
           gla_a_up_b, gla_a_bias_b, gla_norm_w, w_hy_proj, w_gla_proj, w_out, ffn_w_in, ffn_dw,
           ffn_dw_bias, ffn_w_out, final_norm_w):
    assert w_ada.shape[0] == 1, "single-layer block"
    B, L, D = x.shape
    R = GLA_LOWRANK

    n_rows = -(-(B + 1) // 8) * 8
    c_rows = jnp.zeros((n_rows, D), F32).at[:B].set(c).at[B].set(c_ctx)
    mod = ada_call(c_rows, w_ada[0], b_ada[0])
    sh1, sc1, g1, sh2, sc2, g2 = [m[:B, None, :] for m in jnp.split(mod, 6, axis=-1)]
    csh1, csc1 = mod[B:B + 1, None, 0:D], mod[B:B + 1, None, D:2 * D]

    w = w_in[0]
    o_q = HY_COLS
    o_k = o_q + GLA_QK
    o_v = o_k + GLA_QK
    o_r = o_v + GLA_VW
    o_af = o_r + GLA_VW
    o_ab = o_af + R
    o_g = o_ab + R
    w_main = jnp.concatenate([w[:, :o_k], w[:, o_v:o_af], w[:, o_g:]], axis=1).astype(BF16)
    w_kt = w[:, o_k:o_v].T.astype(BF16)
    w_v = w[:, o_v:o_r].astype(BF16)
    w_a = jnp.pad(w[:, o_af:o_g], ((0, 0), (0, LANES - 2 * R))).astype(BF16)
    w_at = w_a.T
    wup_f = _pad_rows(gla_a_up_f[0], LANES, 0)
    wup_b = _pad_rows(gla_a_up_b[0], LANES, R)
    wupt_f, wupt_b = wup_f.T, wup_b.T
    bias_f, bias_b = gla_a_bias_f[0].reshape(1, GLA_QK), gla_a_bias_b[0].reshape(1, GLA_QK)
    biast_f, biast_b = bias_f.reshape(GLA_QK, 1), bias_b.reshape(GLA_QK, 1)

    cos32, sin32 = dft_tables(HY_BLOCK)
    a_co, b_co, g_nyq = filter_call(L, hy_filt_w1[0], hy_filt_b1[0], hy_filt_w2[0], hy_filt_b2[0],
                                    hy_filt_w3[0], hy_filt_freq[0], hy_decay[0], cos32, sin32)
    fwd_tab = jnp.concatenate([cos32, sin32], axis=0).astype(BF16)
    inv_tab = jnp.concatenate([cos32, sin32], axis=1).astype(BF16)

    s_f, s_b = ctx_call(ctx, norm1_w[0], csc1, csh1, w_kt, w_v, w_at, wupt_f, biast_f, wupt_b, biast_b)

    u_hy, qvr, gates, kt, a, at = inproj_call(x, norm1_w[0], sc1, sh1, w_main, w_kt, w_a, w_at)
    y_hy = hyena_call(u_hy, hy_conv_w[0], hy_bias[0], fwd_tab, inv_tab, a_co, b_co, g_nyq)
    y_gla = gla_call(qvr, kt, a, at, s_f, s_b, wup_f, bias_f, wup_b, bias_b,
                     wupt_f, biast_f, wupt_b, biast_b, gla_norm_w[0])
    x_new, h2 = merge_call(x, y_hy, y_gla, gates, g1, w_hy_proj[0].astype(BF16), w_gla_proj[0].astype(BF16),
                           w_out[0].astype(BF16), norm2_w[0], sc2, sh2)
    w_ffn = ffn_w_in[0]
    act = ffn_act_call(h2, w_ffn[:, FFN_HIDDEN:].astype(BF16), w_ffn[:, :FFN_HIDDEN].astype(BF16),
                       ffn_dw[0].reshape(9, FFN_HIDDEN), ffn_dw_bias[0])
    return ffn_out_call(act, x_new, g2, ffn_w_out[0].astype(BF16), final_norm_w)
```

```python
import functools
import math

import jax
import jax.numpy as jnp
import numpy as np
from jax import lax
from jax.experimental import pallas as pl
from jax.experimental.pallas import tpu as pltpu

F32 = jnp.float32
BF16 = jnp.bfloat16
HIGHEST = lax.Precision.HIGHEST

D_MODEL = 1024
GRID_W = 64
HY_WIDTH = 512
HY_BANDS = 16
HY_EMB = 1 + 2 * HY_BANDS
HY_FILT_HIDDEN = 64
GLA_HEADS = 4
GLA_DK = 64
GLA_DV = 128
GLA_LOWRANK = 16
GLA_TAU = 16.0
GLA_CHUNK = 64
FFN_HIDDEN = 2816
EPS = 1e-6
HY_COLS = 3 * HY_WIDTH
GLA_QK = GLA_HEADS * GLA_DK
GLA_VW = GLA_HEADS * GLA_DV

LANES = 128
PAIR = 2 * GLA_CHUNK
VMEM_LIMIT = 56 * 1024 * 1024


def _cp(sem):
    return pltpu.CompilerParams(dimension_semantics=sem, vmem_limit_bytes=VMEM_LIMIT)


def _dot(a, b):
    return jnp.dot(a, b, preferred_element_type=F32)


def _dot_hi(a, b):
    return jnp.dot(a, b, preferred_element_type=F32, precision=HIGHEST)


def _sigmoid(x):
    return 1.0 / (1.0 + jnp.exp(-x))


def _silu(x):
    return x * _sigmoid(x)


def _log_sigmoid(x):
    return jnp.minimum(x, 0.0) - jnp.log(1.0 + jnp.exp(-jnp.abs(x)))


def _rms_scale(xf):
    return xf * lax.rsqrt(jnp.mean(xf * xf, axis=-1, keepdims=True) + EPS)


def _ada_kernel(c_ref, w_ref, b_ref, o_ref):
    o_ref[...] = _dot_hi(_silu(c_ref[...]), w_ref[...]) + b_ref[...]


def ada_call(c_rows, w_ada, b_ada):
    rows, d = c_rows.shape
    n = w_ada.shape[1]
    bn = 512
    return pl.pallas_call(
        _ada_kernel,
        grid=(n // bn,),
        in_specs=[pl.BlockSpec((rows, d), lambda j: (0, 0)),
                  pl.BlockSpec((d, bn), lambda j: (0, j)),
                  pl.BlockSpec((1, bn), lambda j: (0, j))],
        out_specs=pl.BlockSpec((rows, bn), lambda j: (0, j)),
        out_shape=jax.ShapeDtypeStruct((rows, n), F32),
        compiler_params=_cp(("arbitrary",)),
        name="ada",
    )(c_rows, w_ada, b_ada.reshape(1, n))


HY_BLOCK = 512


def _filt_mlp_kernel(feats_ref, t_ref, w1_ref, b1_ref, w2_ref, b2_ref, w3_ref, freq_ref, decay_ref, taps_ref):
    freq = freq_ref[...]
    h = jnp.sin(freq * (_dot_hi(feats_ref[...], w1_ref[...]) + b1_ref[...]))
    h = jnp.sin(freq * (_dot_hi(h, w2_ref[...]) + b2_ref[...]))
    k = _dot_hi(h, w3_ref[...]) * jnp.exp(-t_ref[...] * decay_ref[...])
    L = k.shape[0]
    row = lax.broadcasted_iota(jnp.int32, (L, HY_WIDTH), 0)
    k_f = k[:, :HY_WIDTH]
    k_b = jnp.where(row == 0, 0.0, k[:, HY_WIDTH:])
    norm = jnp.sum(jnp.abs(k_f), axis=0, keepdims=True) + jnp.sum(jnp.abs(k_b), axis=0, keepdims=True)
    taps_ref[0:L, :] = k_f / norm
    taps_ref[L:2 * L, :] = k_b / norm


def _filt_blk_kernel(c_ref, s_ref, x_ref, cx_ref, sx_ref, aux_ref, *, M):
    x = x_ref[...]
    cx_ref[0] = _dot_hi(c_ref[...], x)
    sx_ref[0] = _dot_hi(s_ref[...], x)
    row = lax.broadcasted_iota(jnp.int32, (M, HY_WIDTH), 0)
    alt = jnp.sum(jnp.where((row & 1) == 0, x, -x), axis=0, keepdims=True)
    sub = lax.broadcasted_iota(jnp.int32, (8, HY_WIDTH), 0)
    aux_ref[0] = jnp.where(sub == 0, x[0:8], jnp.where(sub == 1, alt, 0.0))


def _filt_seg_kernel(cx1_ref, sx1_ref, aux1_ref, cx2_ref, sx2_ref, aux2_ref, a_ref, b_ref, gn_ref, *, M, P):
    d = pl.program_id(0) - (P - 1)
    f = lax.broadcasted_iota(jnp.int32, (M, 1), 0)
    sf = jnp.where((f & 1) == 0, 1.0, -1.0)
    s = jnp.where(d == 0, 1.0, sf)
    sg1 = jnp.where(d >= 0, -1.0, 1.0)
    sg2 = jnp.where(d >= 1, -1.0, 1.0)
    w = jnp.where(f == 0, 0.5 / M, 1.0 / M)
    x2_0 = aux2_ref[0, 0:1, :]
    a_ref[0] = w * (cx1_ref[0] + s * (cx2_ref[0] - x2_0))
    b_ref[0] = w * (sg1 * sx1_ref[0] + (sg2 * s) * sx2_ref[0])
    gn = (aux1_ref[0, 1:2, :] + aux2_ref[0, 1:2, :] - x2_0) * (0.5 / M)
    gn_ref[0] = jnp.broadcast_to(gn, gn_ref.shape[1:])


def filter_call(L, hy_filt_w1, hy_filt_b1, hy_filt_w2, hy_filt_b2, hy_filt_w3, hy_filt_freq, hy_decay,
                cos32, sin32):
    M = HY_BLOCK
    P = L // M
    t = jnp.linspace(0.0, 1.0, L, dtype=F32)[:, None]
    w = (2.0 * np.pi / L) * jnp.arange(L, dtype=F32)[:, None]
    f = jnp.linspace(1e-4, HY_BANDS - 1, HY_BANDS, dtype=F32)[None, :]
    feats = jnp.concatenate([t, jnp.cos(f * w), -jnp.sin(f * w)], axis=-1)
    feats = jnp.pad(feats, ((0, 0), (0, LANES - HY_EMB)))
    w1 = jnp.pad(hy_filt_w1, ((0, LANES - HY_EMB), (0, 0)))
    hid = HY_FILT_HIDDEN
    taps = pl.pallas_call(
        _filt_mlp_kernel,
        out_shape=jax.ShapeDtypeStruct((2 * L, HY_WIDTH), F32),
        compiler_params=pltpu.CompilerParams(vmem_limit_bytes=VMEM_LIMIT),
        name="filt_mlp",
    )(feats, t, w1, hy_filt_b1.reshape(1, hid), hy_filt_w2, hy_filt_b2.reshape(1, hid), hy_filt_w3,
      hy_filt_freq.reshape(1, hid), hy_decay.reshape(1, 2 * HY_WIDTH))

    blk = lambda n: jax.ShapeDtypeStruct((n, M, HY_WIDTH), F32)
    aux = lambda n: jax.ShapeDtypeStruct((n, 8, HY_WIDTH), F32)
    blk_spec = lambda imap: pl.BlockSpec((1, M, HY_WIDTH), imap)
    aux_spec = lambda imap: pl.BlockSpec((1, 8, HY_WIDTH), imap)
    own = lambda g: (g, 0, 0)
    cx, sx, ax = pl.pallas_call(
        functools.partial(_filt_blk_kernel, M=M),
        grid=(2 * P,),
        in_specs=[pl.BlockSpec((M, M), lambda g: (0, 0)),
                  pl.BlockSpec((M, M), lambda g: (0, 0)),
                  pl.BlockSpec((M, HY_WIDTH), lambda g: (g, 0))],
        out_specs=(blk_spec(own), blk_spec(own), aux_spec(own)),
        out_shape=(blk(2 * P), blk(2 * P), aux(2 * P)),
        compiler_params=_cp(("arbitrary",)),
        name="filt_blk",
    )(cos32, sin32, taps)

    def first(g):
        d = g - (P - 1)
        return (jnp.where(d >= 0, d, P - d), 0, 0)

    def second(g):
        d = g - (P - 1)
        return (jnp.where(d >= 1, d - 1, jnp.where(d == 0, P, P - d - 1)), 0, 0)

    n_seg = 2 * P - 1
    a_co, b_co, g_nyq = pl.pallas_call(
        functools.partial(_filt_seg_kernel, M=M, P=P),
        grid=(n_seg,),
        in_specs=[blk_spec(first), blk_spec(first), aux_spec(first),
                  blk_spec(second), blk_spec(second), aux_spec(second)],
        out_specs=(blk_spec(own), blk_spec(own), aux_spec(own)),
        out_shape=(blk(n_seg), blk(n_seg), aux(n_seg)),
        compiler_params=_cp(("arbitrary",)),
        name="filt_seg",
    )(cx, sx, ax, cx, sx, ax)
    return a_co, b_co, g_nyq


def dft_tables(M):
    n_fft = 2 * M
    idx = (jnp.arange(M, dtype=jnp.int32)[:, None] * jnp.arange(M, dtype=jnp.int32)[None, :]) % n_fft
    ang = idx.astype(F32) * (2.0 * np.pi / n_fft)
    return jnp.cos(ang), jnp.sin(ang)


def _modulated_norm(x_ref, nw_ref, sc_ref, sh_ref):
    y = _rms_scale(x_ref[0].astype(F32)) * nw_ref[...]
    return (y * (1.0 + sc_ref[0]) + sh_ref[0]).astype(BF16)


def _inproj_kernel(x_ref, nw_ref, sc_ref, sh_ref, w_ref, wkt_ref, wa_ref, wat_ref,
                   hy_ref, qvr_ref, gate_ref, kt_ref, a_ref, at_ref, *, tm):
    h = _modulated_norm(x_ref, nw_ref, sc_ref, sh_ref)
    col = 0
    for o_ref in (hy_ref, qvr_ref, gate_ref):
        width = o_ref.shape[-1]
        done = 0
        while done < width:
            step = min(512, width - done)
            o_ref[0, :, done:done + step] = _dot(h, w_ref[:, col:col + step]).astype(o_ref.dtype)
            done += step
            col += step
    nt = (((1,), (1,)), ((), ()))
    kt = lax.dot_general(wkt_ref[...], h, nt, preferred_element_type=F32).astype(BF16)
    for p in range(tm // PAIR):
        kt_ref[0, p] = kt[:, p * PAIR:(p + 1) * PAIR]
    a_ref[0] = _dot(h, wa_ref[...])
    at_ref[0] = lax.dot_general(wat_ref[...], h, nt, preferred_element_type=F32)


def inproj_call(x, norm_w, sc, sh, w_main, w_kt, w_a, w_at, *, tm=1024):
    B, L, D = x.shape
    n_main = w_main.shape[1]
    const = lambda b, i: (0, 0)
    tok = lambda b, i: (b, i, 0)
    bvec = lambda b, i: (b, 0, 0)
    return pl.pallas_call(
        functools.partial(_inproj_kernel, tm=tm),
        grid=(B, L // tm),
        in_specs=[pl.BlockSpec((1, tm, D), tok),
                  pl.BlockSpec((1, D), const),
                  pl.BlockSpec((1, 1, D), bvec),
                  pl.BlockSpec((1, 1, D), bvec),
                  pl.BlockSpec((D, n_main), const, pipeline_mode=pl.Buffered(1)),
                  pl.BlockSpec((GLA_QK, D), const),
                  pl.BlockSpec((D, LANES), const),
                  pl.BlockSpec((LANES, D), const)],
        out_specs=(pl.BlockSpec((1, tm, HY_COLS), tok),
                   pl.BlockSpec((1, tm, GLA_QK + 2 * GLA_VW), tok),
                   pl.BlockSpec((1, tm, 2 * D), tok),
                   pl.BlockSpec((1, tm // PAIR, GLA_QK, PAIR), lambda b, i: (b, i, 0, 0)),
                   pl.BlockSpec((1, tm, LANES), tok),
                   pl.BlockSpec((1, LANES, tm), lambda b, i: (b, 0, i))),
        out_shape=(jax.ShapeDtypeStruct((B, L, HY_COLS), BF16),
                   jax.ShapeDtypeStruct((B, L, GLA_QK + 2 * GLA_VW), BF16),
                   jax.ShapeDtypeStruct((B, L, 2 * D), BF16),
                   jax.ShapeDtypeStruct((B, L // PAIR, GLA_QK, PAIR), BF16),
                   jax.ShapeDtypeStruct((B, L, LANES), F32),
                   jax.ShapeDtypeStruct((B, LANES, L), F32)),
        compiler_params=_cp(("arbitrary", "arbitrary")),
        name="inproj",
    )(x, norm_w.reshape(1, D), sc, sh, w_main, w_kt, w_a, w_at)


def _short_conv_rows(u_ref, w_ref, t0, rows, col0, ncol, L):
    halo = 16
    lo = max(t0 - halo, 0)
    hi = min(t0 + rows + halo, L)
    n = hi - lo
    blk = u_ref[0, lo:hi, col0:col0 + ncol].astype(F32)
    ridx = lax.broadcasted_iota(jnp.int32, (n, 1), 0)
    prev = pltpu.roll(blk, 1, axis=0)
    nxt = pltpu.roll(blk, n - 1, axis=0)
    if lo == 0:
        prev = jnp.where(ridx == 0, 0.0, prev)
    if hi == L:
        nxt = jnp.where(ridx == n - 1, 0.0, nxt)
    w = w_ref[:, col0:col0 + ncol]
    out = prev * w[0:1] + blk * w[1:2] + nxt * w[2:3]
    return out[t0 - lo:t0 - lo + rows]


def _hyena_kernel(u_ref, cw_ref, bias_ref, fwd_ref, inv_ref, a_ref, b_ref, gn_ref, o_ref,
                  zb_ref, x0_ref, zp_ref, uv_ref, y_ref, *, L, M, rb, fb):
    W = HY_WIDTH
    P = L // M

    def sign_rows(n):
        t = lax.broadcasted_iota(jnp.int32, (n, 1), 0)
        return jnp.where((t & 1) == 0, 1.0, -1.0)

    znyq = [jnp.zeros((1, W), F32) for _ in range(P)]
    for t0 in range(0, L, rb):
        x1 = _short_conv_rows(u_ref, cw_ref, t0, rb, W, W, L)
        v = _short_conv_rows(u_ref, cw_ref, t0, rb, 2 * W, W, L)
        z = x1 * v
        zb_ref[t0:t0 + rb, :] = z.astype(BF16)
        x0_ref[t0:t0 + rb, :] = _short_conv_rows(u_ref, cw_ref, t0, rb, 0, W, L).astype(BF16)
        znyq[t0 // M] = znyq[t0 // M] + jnp.sum(sign_rows(rb) * z, axis=0, keepdims=True)

    for j in range(P):
        zp_ref[j] = _dot(fwd_ref[...], zb_ref[j * M:(j + 1) * M, :])

    bias = bias_ref[...]
    for i in range(P):
        for f0 in range(0, M, fb):
            u = jnp.zeros((fb, W), F32)
            v = jnp.zeros((fb, W), F32)
            for j in range(P):
                d = i - j + P - 1
                zre = zp_ref[j, f0:f0 + fb, :]
                p = zp_ref[j, M + f0:M + f0 + fb, :]
                a = a_ref[d, f0:f0 + fb, :]
                b = b_ref[d, f0:f0 + fb, :]
                u = u + zre * a + p * b
                v = v + p * a - zre * b
            uv_ref[f0:f0 + fb, :] = u.astype(BF16)
            uv_ref[M + f0:M + f0 + fb, :] = v.astype(BF16)
        y_ref[...] = _dot(inv_ref[...], uv_ref[...])
        ynyq = jnp.zeros((1, W), F32)
        for j in range(P):
            ynyq = ynyq + znyq[j] * gn_ref[i - j + P - 1, 0:1, :]
        for r0 in range(0, M, rb):
            rows = slice(i * M + r0, i * M + r0 + rb)
            z = zb_ref[rows, :].astype(F32)
            y = y_ref[r0:r0 + rb, :] + sign_rows(rb) * ynyq + bias * z
            o_ref[0, rows, :] = (x0_ref[rows, :].astype(F32) * y).astype(o_ref.dtype)


def hyena_call(u_hy, conv_w, hy_bias, fwd_tab, inv_tab, a_co, b_co, g_nyq, *, rb=256, fb=16):
    B, L, _ = u_hy.shape
    W = HY_WIDTH
    M = HY_BLOCK
    n_seg = a_co.shape[0]
    const2 = lambda b: (0, 0)
    const3 = lambda b: (0, 0, 0)
    return pl.pallas_call(
        functools.partial(_hyena_kernel, L=L, M=M, rb=rb, fb=fb),
        grid=(B,),
        in_specs=[pl.BlockSpec((1, L, HY_COLS), lambda b: (b, 0, 0)),
                  pl.BlockSpec((3, HY_COLS), const2),
                  pl.BlockSpec((1, W), const2),
                  pl.BlockSpec((2 * M, M), const2),
                  pl.BlockSpec((M, 2 * M), const2),
                  pl.BlockSpec((n_seg, M, W), const3, pipeline_mode=pl.Buffered(1)),
                  pl.BlockSpec((n_seg, M, W), const3, pipeline_mode=pl.Buffered(1)),
                  pl.BlockSpec((n_seg, 8, W), const3)],
        out_specs=pl.BlockSpec((1, L, W), lambda b: (b, 0, 0)),
        out_shape=jax.ShapeDtypeStruct((B, L, W), BF16),
        scratch_shapes=[pltpu.VMEM((L, W), BF16),
                        pltpu.VMEM((L, W), BF16),
                        pltpu.VMEM((L // M, 2 * M, W), F32),
                        pltpu.VMEM((2 * M, W), BF16),
                        pltpu.VMEM((M, W), F32)],
        compiler_params=_cp(("arbitrary",)),
        name="hyena",
    )(u_hy, conv_w, hy_bias.reshape(1, W), fwd_tab, inv_tab, a_co, b_co, g_nyq)


def _split3(x):
    x1 = x.astype(BF16)
    r = x - x1.astype(F32)
    x2 = r.astype(BF16)
    x3 = (r - x2.astype(F32)).astype(BF16)
    return x1, x2, x3


def _tri_left(m, x):
    n = x.shape[1]
    r = _dot(m, jnp.concatenate(_split3(x), axis=1))
    return r[:, 0:n] + r[:, n:2 * n] + r[:, 2 * n:3 * n]


def _tri_right(x, m):
    n = x.shape[0]
    r = _dot(jnp.concatenate(_split3(x), axis=0), m)
    return r[0:n] + r[n:2 * n] + r[2 * n:3 * n]


def _tri_mats(n, block):
    r = np.arange(n)
    same = (r[:, None] // block) == (r[None, :] // block)
    lower = same & (r[:, None] >= r[None, :])
    upper = same & (r[:, None] <= r[None, :])
    return jnp.asarray(lower, BF16), jnp.asarray(upper, BF16)


def _log_decay_tok(a, wup_ref, bias_ref):
    return _log_sigmoid(_dot_hi(a, wup_ref[...]) + bias_ref[...]) * (1.0 / GLA_TAU)


def _log_decay_feat(at, wupt_ref, biast_ref):
    return _log_sigmoid(_dot_hi(wupt_ref[...], at) + biast_ref[...]) * (1.0 / GLA_TAU)


def _ctx_kernel(x_ref, nw_ref, sc_ref, sh_ref, wkt_ref, wv_ref, wat_ref,
                wuptf_ref, biastf_ref, wuptb_ref, biastb_ref, lo_ref, up_ref, sf_ref, sb_ref):
    h = _modulated_norm(x_ref, nw_ref, sc_ref, sh_ref)
    nt = (((1,), (1,)), ((), ()))
    kt = lax.dot_general(wkt_ref[...], h, nt, preferred_element_type=F32)
    v = _dot(h, wv_ref[...]).astype(BF16)
    at = lax.dot_general(wat_ref[...], h, nt, preferred_element_type=F32)
    la_f = _log_decay_feat(at, wuptf_ref, biastf_ref)
    la_b = _log_decay_feat(at, wuptb_ref, biastb_ref)
    lo = lo_ref[...]
    up = up_ref[...]
    w_f = jnp.exp(_tri_right(la_f, lo) - la_f)
    w_b = jnp.exp(_tri_right(la_b, up) - la_b)
    kf = (kt * w_f).astype(BF16)
    kb = (kt * w_b).astype(BF16)
    for hd in range(GLA_HEADS):
        r0, c0 = hd * GLA_DK, hd * GLA_DV
        sf_ref[0, hd] = _dot(kf[r0:r0 + GLA_DK], v[:, c0:c0 + GLA_DV])
        sb_ref[0, hd] = _dot(kb[r0:r0 + GLA_DK], v[:, c0:c0 + GLA_DV])


def ctx_call(ctx, norm_w, csc, csh, w_kt, w_v, w_at, wupt_f, biast_f, wupt_b, biast_b):
    B, Lc, D = ctx.shape
    lo, up = _tri_mats(Lc, Lc)
    const2 = lambda b: (0, 0)
    const3 = lambda b: (0, 0, 0)
    st = jax.ShapeDtypeStruct((B, GLA_HEADS, GLA_DK, GLA_DV), F32)
    st_spec = pl.BlockSpec((1, GLA_HEADS, GLA_DK, GLA_DV), lambda b: (b, 0, 0, 0))
    return pl.pallas_call(
        _ctx_kernel,
        grid=(B,),
        in_specs=[pl.BlockSpec((1, Lc, D), lambda b: (b, 0, 0)),
                  pl.BlockSpec((1, D), const2),
                  pl.BlockSpec((1, 1, D), const3),
                  pl.BlockSpec((1, 1, D), const3),
                  pl.BlockSpec((GLA_QK, D), const2),
                  pl.BlockSpec((D, GLA_VW), const2),
                  pl.BlockSpec((LANES, D), const2),
                  pl.BlockSpec((GLA_QK, LANES), const2),
                  pl.BlockSpec((GLA_QK, 1), const2),
                  pl.BlockSpec((GLA_QK, LANES), const2),
                  pl.BlockSpec((GLA_QK, 1), const2),
                  pl.BlockSpec((Lc, Lc), const2),
                  pl.BlockSpec((Lc, Lc), const2)],
        out_specs=(st_spec, st_spec),
        out_shape=(st, st),
        compiler_params=_cp(("arbitrary",)),
        name="ctx",
    )(ctx, norm_w.reshape(1, D), csc, csh, w_kt, w_v, w_at, wupt_f, biast_f, wupt_b, biast_b, lo, up)


def _gla_kernel(qvr_ref, kt_ref, a_ref, at_ref, sf_ref, sb_ref,
                wupf_ref, biasf_ref, wupb_ref, biasb_ref,
                wuptf_ref, biastf_ref, wuptb_ref, biastb_ref,
                lo_ref, up_ref, nw_ref, o_ref,
                laf_ref, lab_ref, latf_ref, latb_ref, accf_ref, accb_ref, stf_ref, stb_ref, *, L, rb):
    n_pairs = L // PAIR
    C = GLA_CHUNK
    H, DK, DV = GLA_HEADS, GLA_DK, GLA_DV
    scale = DK ** -0.5

    for t0 in range(0, L, rb):
        a = a_ref[0, t0:t0 + rb, :]
        laf_ref[t0:t0 + rb, :] = _log_decay_tok(a, wupf_ref, biasf_ref)
        lab_ref[t0:t0 + rb, :] = _log_decay_tok(a, wupb_ref, biasb_ref)
    for p in range(n_pairs):
        at = at_ref[0, :, p * PAIR:(p + 1) * PAIR]
        latf_ref[p] = _log_decay_feat(at, wuptf_ref, biastf_ref)
        latb_ref[p] = _log_decay_feat(at, wuptb_ref, biastb_ref)

    lo = lo_ref[...]
    up = up_ref[...]
    rr = lax.broadcasted_iota(jnp.int32, (C, C), 0)
    cc = lax.broadcasted_iota(jnp.int32, (C, C), 1)

    def half_pair(p, la_ref, lat_ref, st_ref, acc_ref, forward):
        m_tok = lo if forward else up
        m_feat = up if forward else lo
        keep = (rr >= cc) if forward else (rr <= cc)
        last = C - 1 if forward else 0
        t0 = pl.multiple_of(p * PAIR, PAIR)
        b_tok = _tri_left(m_tok, la_ref[pl.ds(t0, PAIR), :])
        b_feat = _tri_right(lat_ref[p], m_feat)
        q = qvr_ref[0, pl.ds(t0, PAIR), 0:GLA_QK].astype(F32)
        qe = (q * scale * jnp.exp(b_tok)).astype(BF16)
        kt = kt_ref[0, p].astype(F32)
        ket = (kt * jnp.exp(-b_feat)).astype(BF16)
        v = qvr_ref[0, pl.ds(t0, PAIR), GLA_QK:GLA_QK + GLA_VW]
        for half in ((0, 1) if forward else (1, 0)):
            r0 = half * C
            b_last = b_feat[:, r0 + last:r0 + last + 1]
            kdt = (kt[:, r0:r0 + C] * jnp.exp(b_last - b_feat[:, r0:r0 + C])).astype(BF16)
            g = jnp.exp(b_last)
            for hd in range(H):
                k0, v0 = hd * DK, hd * DV
                qe_h = qe[r0:r0 + C, k0:k0 + DK]
                v_h = v[r0:r0 + C, v0:v0 + DV]
                s_prev = st_ref[hd]
                sc = _dot(qe_h, ket[k0:k0 + DK, r0:r0 + C])
                sc = jnp.where(keep, sc, 0.0).astype(BF16)
                acc_ref[pl.ds(t0 + r0, C), v0:v0 + DV] = _dot(sc, v_h) + _dot(qe_h, s_prev.astype(BF16))
                st_ref[hd] = g[k0:k0 + DK] * s_prev + _dot(kdt[k0:k0 + DK], v_h)

    for hd in range(H):
        stf_ref[hd] = sf_ref[0, hd]
        stb_ref[hd] = sb_ref[0, hd]

    def pair_body(i, carry):
        half_pair(i, laf_ref, latf_ref, stf_ref, accf_ref, True)
        half_pair(n_pairs - 1 - i, lab_ref, latb_ref, stb_ref, accb_ref, False)
        return carry

    lax.fori_loop(0, n_pairs, pair_body, 0, unroll=2)

    for t0 in range(0, L, rb):
        r = qvr_ref[0, t0:t0 + rb, GLA_QK + GLA_VW:GLA_QK + 2 * GLA_VW].astype(F32)
        for hd in range(H):
            v0 = hd * DV
            o = _rms_scale(accf_ref[t0:t0 + rb, v0:v0 + DV] + accb_ref[t0:t0 + rb, v0:v0 + DV])
            y = o * nw_ref[:, v0:v0 + DV] * _silu(r[:, v0:v0 + DV])
            o_ref[0, t0:t0 + rb, v0:v0 + DV] = y.astype(o_ref.dtype)


def gla_call(qvr, kt, a, at, s_f, s_b, wup_f, bias_f, wup_b, bias_b,
             wupt_f, biast_f, wupt_b, biast_b, norm_w, *, rb=256):
    B, L, _ = qvr.shape
    lo, up = _tri_mats(PAIR, GLA_CHUNK)
    const2 = lambda b: (0, 0)
    st_spec = pl.BlockSpec((1, GLA_HEADS, GLA_DK, GLA_DV), lambda b: (b, 0, 0, 0))
    return pl.pallas_call(
        functools.partial(_gla_kernel, L=L, rb=rb),
        grid=(B,),
        in_specs=[pl.BlockSpec((1, L, GLA_QK + 2 * GLA_VW), lambda b: (b, 0, 0)),
                  pl.BlockSpec((1, L // PAIR, GLA_QK, PAIR), lambda b: (b, 0, 0, 0)),
                  pl.BlockSpec((1, L, LANES), lambda b: (b, 0, 0)),
                  pl.BlockSpec((1, LANES, L), lambda b: (b, 0, 0)),
                  st_spec, st_spec,
                  pl.BlockSpec((LANES, GLA_QK), const2),
                  pl.BlockSpec((1, GLA_QK), const2),
                  pl.BlockSpec((LANES, GLA_QK), const2),
                  pl.BlockSpec((1, GLA_QK), const2),
                  pl.BlockSpec((GLA_QK, LANES), const2),
                  pl.BlockSpec((GLA_QK, 1), const2),
                  pl.BlockSpec((GLA_QK, LANES), const2),
                  pl.BlockSpec((GLA_QK, 1), const2),
                  pl.BlockSpec((PAIR, PAIR), const2),
                  pl.BlockSpec((PAIR, PAIR), const2),
                  pl.BlockSpec((1, GLA_VW), const2)],
        out_specs=pl.BlockSpec((1, L, GLA_VW), lambda b: (b, 0, 0)),
        out_shape=jax.ShapeDtypeStruct((B, L, GLA_VW), BF16),
        scratch_shapes=[pltpu.VMEM((L, GLA_QK), F32),
                        pltpu.VMEM((L, GLA_QK), F32),
                        pltpu.VMEM((L // PAIR, GLA_QK, PAIR), F32),
                        pltpu.VMEM((L // PAIR, GLA_QK, PAIR), F32),
                        pltpu.VMEM((L, GLA_VW), F32),
                        pltpu.VMEM((L, GLA_VW), F32),
                        pltpu.VMEM((GLA_HEADS, GLA_DK, GLA_DV), F32),
                        pltpu.VMEM((GLA_HEADS, GLA_DK, GLA_DV), F32)],
        compiler_params=_cp(("arbitrary",)),
        name="gla",
    )(qvr, kt, a, at, s_f, s_b, wup_f, bias_f, wup_b, bias_b,
      wupt_f, biast_f, wupt_b, biast_b, lo, up, norm_w.reshape(1, GLA_VW))


def _merge_kernel(x_ref, yhy_ref, ygla_ref, gate_ref, g1_ref, whp_ref, wgp_ref, wo_ref,
                  nw_ref, sc_ref, sh_ref, xn_ref, h2_ref):
    D = D_MODEL
    g_hy = _sigmoid(gate_ref[0, :, 0:D].astype(F32))
    g_gla = _sigmoid(gate_ref[0, :, D:2 * D].astype(F32))
    merged = g_hy * _dot(yhy_ref[0], whp_ref[...]) + g_gla * _dot(ygla_ref[0], wgp_ref[...])
    mix = _dot(merged.astype(BF16), wo_ref[...])
    xn = x_ref[0] + g1_ref[0] * mix
    xn_ref[0] = xn
    y = _rms_scale(xn) * nw_ref[...]
    h2_ref[0] = (y * (1.0 + sc_ref[0]) + sh_ref[0]).astype(h2_ref.dtype)


def merge_call(x, y_hy, y_gla, gates, g1, w_hy_proj, w_gla_proj, w_out, norm2_w, sc2, sh2, *, tm=1024):
    B, L, D = x.shape
    tok = lambda b, i: (b, i, 0)
    bvec = lambda b, i: (b, 0, 0)
    const = lambda b, i: (0, 0)
    return pl.pallas_call(
        _merge_kernel,
        grid=(B, L // tm),
        in_specs=[pl.BlockSpec((1, tm, D), tok),
                  pl.BlockSpec((1, tm, HY_WIDTH), tok),
                  pl.BlockSpec((1, tm, GLA_VW), tok),
                  pl.BlockSpec((1, tm, 2 * D), tok),
                  pl.BlockSpec((1, 1, D), bvec),
                  pl.BlockSpec((HY_WIDTH, D), const),
                  pl.BlockSpec((GLA_VW, D), const),
                  pl.BlockSpec((D, D), const),
                  pl.BlockSpec((1, D), const),
                  pl.BlockSpec((1, 1, D), bvec),
                  pl.BlockSpec((1, 1, D), bvec)],
        out_specs=(pl.BlockSpec((1, tm, D), tok), pl.BlockSpec((1, tm, D), tok)),
        out_shape=(jax.ShapeDtypeStruct((B, L, D), F32), jax.ShapeDtypeStruct((B, L, D), BF16)),
        compiler_params=_cp(("arbitrary", "arbitrary")),
        name="merge",
    )(x, y_hy, y_gla, gates, g1, w_hy_proj, w_gla_proj, w_out, norm2_w.reshape(1, D), sc2, sh2)


FFN_PAD = GRID_W


def _ffn_act_kernel(h_ref, wg_ref, wup_ref, dw_ref, dwb_ref, o_ref, g0_ref, g1_ref, s_ref, *, L, n_chunks):
    j = pl.program_id(1)
    hc = wup_ref.shape[1]
    mb = 512
    W = GRID_W
    sub = lax.broadcasted_iota(jnp.int32, (8, hc), 0)

    def project(gate_ref, chunk, t0):
        gate_ref[FFN_PAD + t0:FFN_PAD + t0 + mb, :] = _dot(h_ref[0, t0:t0 + mb, :], wg_ref[chunk])

    @pl.when(j == 0)
    def _():
        zeros = jnp.zeros((FFN_PAD, hc), F32)
        for ref in (g0_ref, g1_ref):
            ref[0:FFN_PAD, :] = zeros
            ref[FFN_PAD + L:FFN_PAD + L + FFN_PAD, :] = zeros
        for t0 in range(0, L, mb):
            project(g0_ref, 0, t0)

    def step(gate_ref, nxt_ref):
        nxt_chunk = jnp.minimum(j + 1, n_chunks - 1)
        dw = dw_ref[...]
        for t0 in range(0, L, mb):
            project(nxt_ref, nxt_chunk, t0)
            s_ref[t0:t0 + mb, :] = _dot(h_ref[0, t0:t0 + mb, :], wup_ref[...])
            for r0 in range(t0, t0 + mb, W):
                above = gate_ref[FFN_PAD + r0 - W:FFN_PAD + r0, :]
                centre = gate_ref[FFN_PAD + r0:FFN_PAD + r0 + W, :]
                below = gate_ref[FFN_PAD + r0 + W:FFN_PAD + r0 + 2 * W, :]
                col = [above * dw[dx:dx + 1] + centre * dw[3 + dx:4 + dx] + below * dw[6 + dx:7 + dx]
                       for dx in range(3)]
                left = pltpu.roll(col[0], 1, axis=0)
                right = pltpu.roll(col[2], W - 1, axis=0)
                left = jnp.concatenate([jnp.where(sub == 0, 0.0, left[0:8]), left[8:]], axis=0)
                right = jnp.concatenate([right[:W - 8], jnp.where(sub == 7, 0.0, right[W - 8:])], axis=0)
                conv = col[1] + left + right + dwb_ref[...]
                o_ref[0, r0:r0 + W, :] = (_silu(conv) * s_ref[r0:r0 + W, :]).astype(o_ref.dtype)

    parity = lax.rem(j, 2)
    pl.when(parity == 0)(lambda: step(g0_ref, g1_ref))
    pl.when(parity == 1)(lambda: step(g1_ref, g0_ref))


def ffn_act_call(h2, w_gate, w_up, dw, dw_bias, *, hc=256):
    B, L, D = h2.shape
    F = w_up.shape[1]
    n_chunks = F // hc
    wg = w_gate.reshape(D, n_chunks, hc).transpose(1, 0, 2)
    return pl.pallas_call(
        functools.partial(_ffn_act_kernel, L=L, n_chunks=n_chunks),
        grid=(B, n_chunks),
        in_specs=[pl.BlockSpec((1, L, D), lambda b, j: (b, 0, 0)),
                  pl.BlockSpec((n_chunks, D, hc), lambda b, j: (0, 0, 0), pipeline_mode=pl.Buffered(1)),
                  pl.BlockSpec((D, hc), lambda b, j: (0, j)),
                  pl.BlockSpec((9, hc), lambda b, j: (0, j)),
                  pl.BlockSpec((1, hc), lambda b, j: (0, j))],
        out_specs=pl.BlockSpec((1, L, hc), lambda b, j: (b, 0, j)),
        out_shape=jax.ShapeDtypeStruct((B, L, F), BF16),
        scratch_shapes=[pltpu.VMEM((L + 2 * FFN_PAD, hc), F32)] * 2 + [pltpu.VMEM((L, hc), F32)],
        compiler_params=_cp(("arbitrary", "arbitrary")),
        name="ffn_act",
    )(h2, wg, w_up, dw, dw_bias.reshape(1, F))


def _ffn_out_kernel(act_ref, xn_ref, g2_ref, wout_ref, fw_ref, o_ref):
    xo = xn_ref[0] + g2_ref[0] * _dot(act_ref[0], wout_ref[...])
    o_ref[0] = _rms_scale(xo) * fw_ref[...]


def ffn_out_call(act, x_new, g2, w_out, final_w, *, tm=1024):
    B, L, F = act.shape
    D = w_out.shape[1]
    tok = lambda b, i: (b, i, 0)
    const = lambda b, i: (0, 0)
    return pl.pallas_call(
        _ffn_out_kernel,
        grid=(B, L // tm),
        in_specs=[pl.BlockSpec((1, tm, F), tok),
                  pl.BlockSpec((1, tm, D), tok),
                  pl.BlockSpec((1, 1, D), lambda b, i: (b, 0, 0)),
                  pl.BlockSpec((F, D), const, pipeline_mode=pl.Buffered(1)),
                  pl.BlockSpec((1, D), const)],
        out_specs=pl.BlockSpec((1, tm, D), tok),
        out_shape=jax.ShapeDtypeStruct((B, L, D), F32),
        compiler_params=_cp(("arbitrary", "arbitrary")),
        name="ffn_out",
    )(act, x_new, g2, w_out, final_w.reshape(1, D))


def _pad_rows(w, rows, at=0):
    out = jnp.zeros((rows, w.shape[1]), w.dtype)
    return lax.dynamic_update_slice(out, w, (at, 0))


def kernel(x, c, ctx, c_ctx, w_ada, b_ada, norm1_w, norm2_w, w_in, hy_conv_w, hy_filt_w1, hy_filt_b1,
           hy_filt_w2, hy_filt_b2, hy_filt_w3, hy_filt_freq, hy_decay, hy_bias, gla_a_up_f, gla_a_bias_f,
           gla_a_up_b, gla_a_bias_b, gla_norm_w, w_hy_proj, w_gla_proj, w_out, ffn_w_in, ffn_dw,
           ffn_dw_bias, ffn_w_out, final_norm_w):
    assert w_ada.shape[0] == 1, "single-layer block"
    B, L, D = x.shape
    R = GLA_LOWRANK

    n_rows = -(-(B + 1) // 8) * 8
    c_rows = jnp.zeros((n_rows, D), F32).at[:B].set(c).at[B].set(c_ctx)
    mod = ada_call(c_rows, w_ada[0], b_ada[0])
    sh1, sc1, g1, sh2, sc2, g2 = [m[:B, None, :] for m in jnp.split(mod, 6, axis=-1)]
    csh1, csc1 = mod[B:B + 1, None, 0:D], mod[B:B + 1, None, D:2 * D]

    w = w_in[0]
    o_q = HY_COLS
    o_k = o_q + GLA_QK
    o_v = o_k + GLA_QK
    o_r = o_v + GLA_VW
    o_af = o_r + GLA_VW
    o_ab = o_af + R
    o_g = o_ab + R
    w_main = jnp.concatenate([w[:, :o_k], w[:, o_v:o_af], w[:, o_g:]], axis=1).astype(BF16)
    w_kt = w[:, o_k:o_v].T.astype(BF16)
    w_v = w[:, o_v:o_r].astype(BF16)
    w_a = jnp.pad(w[:, o_af:o_g], ((0, 0), (0, LANES - 2 * R))).astype(BF16)
    w_at = w_a.T
    wup_f = _pad_rows(gla_a_up_f[0], LANES, 0)
    wup_b = _pad_rows(gla_a_up_b[0], LANES, R)
    wupt_f, wupt_b = wup_f.T, wup_b.T
    bias_f, bias_b = gla_a_bias_f[0].reshape(1, GLA_QK), gla_a_bias_b[0].reshape(1, GLA_QK)
    biast_f, biast_b = bias_f.reshape(GLA_QK, 1), bias_b.reshape(GLA_QK, 1)

    cos32, sin32 = dft_tables(HY_BLOCK)
    a_co, b_co, g_nyq = filter_call(L, hy_filt_w1[0], hy_filt_b1[0], hy_filt_w2[0], hy_filt_b2[0],
                                    hy_filt_w3[0], hy_filt_freq[0], hy_decay[0], cos32, sin32)
    fwd_tab = jnp.concatenate([cos32, sin32], axis=0).astype(BF16)
    inv_tab = jnp.concatenate([cos32, sin32], axis=1).astype(BF16)

    s_f, s_b = ctx_call(ctx, norm1_w[0], csc1, csh1, w_kt, w_v, w_at, wupt_f, biast_f, wupt_b, biast_b)

    u_hy, qvr, gates, kt, a, at = inproj_call(x, norm1_w[0], sc1, sh1, w_main, w_kt, w_a, w_at)
    y_hy = hyena_call(u_hy, hy_conv_w[0], hy_bias[0], fwd_tab, inv_tab, a_co, b_co, g_nyq)
    y_gla = gla_call(qvr, kt, a, at, s_f, s_b, wup_f, bias_f, wup_b, bias_b,
                     wupt_f, biast_f, wupt_b, biast_b, gla_norm_w[0])
    x_new, h2 = merge_call(x, y_hy, y_gla, gates, g1, w_hy_proj[0].astype(BF16), w_gla_proj[0].astype(BF16),
                           w_out[0].astype(BF16), norm2_w[0], sc2, sh2)
    w_ffn = ffn_w_in[0]
    act = ffn_act_call(h2, w_ffn[:, FFN_HIDDEN:].astype(BF16), w_ffn[:, :FFN_HIDDEN].astype(BF16),
                       ffn_dw[0].reshape(9, FFN_HIDDEN), ffn_dw_bias[0])
    return ffn_out_call(act, x_new, g2, ffn_w_out[0].astype(BF16), final_norm_w)
```

```python
import functools
import math

import jax
import jax.numpy as jnp
import numpy as np
from jax import lax
from jax.experimental import pallas as pl
from jax.experimental.pallas import tpu as pltpu

F32 = jnp.float32
BF16 = jnp.bfloat16
HIGHEST = lax.Precision.HIGHEST

D_MODEL = 1024
GRID_W = 64
HY_WIDTH = 512
HY_BANDS = 16
HY_EMB = 1 + 2 * HY_BANDS
HY_FILT_HIDDEN = 64
GLA_HEADS = 4
GLA_DK = 64
GLA_DV = 128
GLA_LOWRANK = 16
GLA_TAU = 16.0
GLA_CHUNK = 64
FFN_HIDDEN = 2816
EPS = 1e-6
HY_COLS = 3 * HY_WIDTH
GLA_QK = GLA_HEADS * GLA_DK
GLA_VW = GLA_HEADS * GLA_DV

LANES = 128
PAIR = 2 * GLA_CHUNK
VMEM_LIMIT = 56 * 1024 * 1024


def _cp(sem):
    return pltpu.CompilerParams(dimension_semantics=sem, vmem_limit_bytes=VMEM_LIMIT)


def _dot(a, b):
    return jnp.dot(a, b, preferred_element_type=F32)


def _dot_hi(a, b):
    return jnp.dot(a, b, preferred_element_type=F32, precision=HIGHEST)


def _split2(x):
    x1 = x.astype(BF16)
    return x1, (x - x1.astype(F32)).astype(BF16)


def _dot_3(a, b):
    a1, a2 = _split2(a)
    b1, b2 = _split2(b)
    return _dot(jnp.concatenate([a1, a1, a2], axis=1), jnp.concatenate([b1, b2, b1], axis=0))


def _sigmoid(x):
    return 1.0 / (1.0 + jnp.exp(-x))


def _silu(x):
    return x * _sigmoid(x)


def _log_sigmoid(x):
    return jnp.minimum(x, 0.0) - jnp.log(1.0 + jnp.exp(-jnp.abs(x)))


def _rms_scale(xf):
    return xf * lax.rsqrt(jnp.mean(xf * xf, axis=-1, keepdims=True) + EPS)


def _ada_kernel(c_ref, w_ref, b_ref, o_ref):
    o_ref[...] = _dot_hi(_silu(c_ref[...]), w_ref[...]) + b_ref[...]


def ada_call(c_rows, w_ada, b_ada):
    rows, d = c_rows.shape
    n = w_ada.shape[1]
    bn = 512
    return pl.pallas_call(
        _ada_kernel,
        grid=(n // bn,),
        in_specs=[pl.BlockSpec((rows, d), lambda j: (0, 0)),
                  pl.BlockSpec((d, bn), lambda j: (0, j)),
                  pl.BlockSpec((1, bn), lambda j: (0, j))],
        out_specs=pl.BlockSpec((rows, bn), lambda j: (0, j)),
        out_shape=jax.ShapeDtypeStruct((rows, n), F32),
        compiler_params=_cp(("arbitrary",)),
        name="ada",
    )(c_rows, w_ada, b_ada.reshape(1, n))


HY_BLOCK = 512


def _filt_mlp_kernel(feats_ref, t_ref, w1_ref, b1_ref, w2_ref, b2_ref, w3_ref, freq_ref, decay_ref, taps_ref):
    freq = freq_ref[...]
    h = jnp.sin(freq * (_dot_hi(feats_ref[...], w1_ref[...]) + b1_ref[...]))
    h = jnp.sin(freq * (_dot_hi(h, w2_ref[...]) + b2_ref[...]))
    k = _dot_hi(h, w3_ref[...]) * jnp.exp(-t_ref[...] * decay_ref[...])
    L = k.shape[0]
    row = lax.broadcasted_iota(jnp.int32, (L, HY_WIDTH), 0)
    k_f = k[:, :HY_WIDTH]
    k_b = jnp.where(row == 0, 0.0, k[:, HY_WIDTH:])
    norm = jnp.sum(jnp.abs(k_f), axis=0, keepdims=True) + jnp.sum(jnp.abs(k_b), axis=0, keepdims=True)
    taps_ref[0:L, :] = k_f / norm
    taps_ref[L:2 * L, :] = k_b / norm


def _filt_blk_kernel(c_ref, s_ref, x_ref, cx_ref, sx_ref, aux_ref, *, M):
    x = x_ref[...]
    cx_ref[0] = _dot_3(c_ref[...], x)
    sx_ref[0] = _dot_3(s_ref[...], x)
    row = lax.broadcasted_iota(jnp.int32, (M, HY_WIDTH), 0)
    alt = jnp.sum(jnp.where((row & 1) == 0, x, -x), axis=0, keepdims=True)
    sub = lax.broadcasted_iota(jnp.int32, (8, HY_WIDTH), 0)
    aux_ref[0] = jnp.where(sub == 0, x[0:8], jnp.where(sub == 1, alt, 0.0))


def _filt_seg_kernel(cx1_ref, sx1_ref, aux1_ref, cx2_ref, sx2_ref, aux2_ref, a_ref, b_ref, gn_ref, *, M, P):
    d = pl.program_id(0) - (P - 1)
    f = lax.broadcasted_iota(jnp.int32, (M, 1), 0)
    sf = jnp.where((f & 1) == 0, 1.0, -1.0)
    s = jnp.where(d == 0, 1.0, sf)
    sg1 = jnp.where(d >= 0, -1.0, 1.0)
    sg2 = jnp.where(d >= 1, -1.0, 1.0)
    w = jnp.where(f == 0, 0.5 / M, 1.0 / M)
    x2_0 = aux2_ref[0, 0:1, :]
    a_ref[0] = w * (cx1_ref[0] + s * (cx2_ref[0] - x2_0))
    b_ref[0] = w * (sg1 * sx1_ref[0] + (sg2 * s) * sx2_ref[0])
    gn = (aux1_ref[0, 1:2, :] + aux2_ref[0, 1:2, :] - x2_0) * (0.5 / M)
    gn_ref[0] = jnp.broadcast_to(gn, gn_ref.shape[1:])


def filter_call(L, hy_filt_w1, hy_filt_b1, hy_filt_w2, hy_filt_b2, hy_filt_w3, hy_filt_freq, hy_decay,
                cos32, sin32):
    M = HY_BLOCK
    P = L // M
    t = jnp.linspace(0.0, 1.0, L, dtype=F32)[:, None]
    w = (2.0 * np.pi / L) * jnp.arange(L, dtype=F32)[:, None]
    f = jnp.linspace(1e-4, HY_BANDS - 1, HY_BANDS, dtype=F32)[None, :]
    feats = jnp.concatenate([t, jnp.cos(f * w), -jnp.sin(f * w)], axis=-1)
    feats = jnp.pad(feats, ((0, 0), (0, LANES - HY_EMB)))
    w1 = jnp.pad(hy_filt_w1, ((0, LANES - HY_EMB), (0, 0)))
    hid = HY_FILT_HIDDEN
    taps = pl.pallas_call(
        _filt_mlp_kernel,
        out_shape=jax.ShapeDtypeStruct((2 * L, HY_WIDTH), F32),
        compiler_params=pltpu.CompilerParams(vmem_limit_bytes=VMEM_LIMIT),
        name="filt_mlp",
    )(feats, t, w1, hy_filt_b1.reshape(1, hid), hy_filt_w2, hy_filt_b2.reshape(1, hid), hy_filt_w3,
      hy_filt_freq.reshape(1, hid), hy_decay.reshape(1, 2 * HY_WIDTH))

    blk = lambda n: jax.ShapeDtypeStruct((n, M, HY_WIDTH), F32)
    aux = lambda n: jax.ShapeDtypeStruct((n, 8, HY_WIDTH), F32)
    blk_spec = lambda imap: pl.BlockSpec((1, M, HY_WIDTH), imap)
    aux_spec = lambda imap: pl.BlockSpec((1, 8, HY_WIDTH), imap)
    own = lambda g: (g, 0, 0)
    cx, sx, ax = pl.pallas_call(
        functools.partial(_filt_blk_kernel, M=M),
        grid=(2 * P,),
        in_specs=[pl.BlockSpec((M, M), lambda g: (0, 0)),
                  pl.BlockSpec((M, M), lambda g: (0, 0)),
                  pl.BlockSpec((M, HY_WIDTH), lambda g: (g, 0))],
        out_specs=(blk_spec(own), blk_spec(own), aux_spec(own)),
        out_shape=(blk(2 * P), blk(2 * P), aux(2 * P)),
        compiler_params=_cp(("arbitrary",)),
        name="filt_blk",
    )(cos32, sin32, taps)

    def first(g):
        d = g - (P - 1)
        return (jnp.where(d >= 0, d, P - d), 0, 0)

    def second(g):
        d = g - (P - 1)
        return (jnp.where(d >= 1, d - 1, jnp.where(d == 0, P, P - d - 1)), 0, 0)

    n_seg = 2 * P - 1
    a_co, b_co, g_nyq = pl.pallas_call(
        functools.partial(_filt_seg_kernel, M=M, P=P),
        grid=(n_seg,),
        in_specs=[blk_spec(first), blk_spec(first), aux_spec(first),
                  blk_spec(second), blk_spec(second), aux_spec(second)],
        out_specs=(blk_spec(own), blk_spec(own), aux_spec(own)),
        out_shape=(blk(n_seg), blk(n_seg), aux(n_seg)),
        compiler_params=_cp(("arbitrary",)),
        name="filt_seg",
    )(cx, sx, ax, cx, sx, ax)
    return a_co, b_co, g_nyq


def dft_tables(M):
    n_fft = 2 * M
    idx = (jnp.arange(M, dtype=jnp.int32)[:, None] * jnp.arange(M, dtype=jnp.int32)[None, :]) % n_fft
    ang = idx.astype(F32) * (2.0 * np.pi / n_fft)
    return jnp.cos(ang), jnp.sin(ang)


def _modulated_norm(x_ref, nw_ref, sc_ref, sh_ref):
    y = _rms_scale(x_ref[0].astype(F32)) * nw_ref[...]
    return (y * (1.0 + sc_ref[0]) + sh_ref[0]).astype(BF16)


def _inproj_kernel(x_ref, nw_ref, sc_ref, sh_ref, w_ref, wkt_ref, wa_ref, wat_ref,
                   hy_ref, qvr_ref, gate_ref, kt_ref, a_ref, at_ref, *, tm):
    h = _modulated_norm(x_ref, nw_ref, sc_ref, sh_ref)
    col = 0
    for o_ref in (hy_ref, qvr_ref, gate_ref):
        width = o_ref.shape[-1]
        done = 0
        while done < width:
            step = min(512, width - done)
            o_ref[0, :, done:done + step] = _dot(h, w_ref[:, col:col + step]).astype(o_ref.dtype)
            done += step
            col += step
    nt = (((1,), (1,)), ((), ()))
    kt = lax.dot_general(wkt_ref[...], h, nt, preferred_element_type=F32).astype(BF16)
    for p in range(tm // PAIR):
        kt_ref[0, p] = kt[:, p * PAIR:(p + 1) * PAIR]
    a_ref[0] = _dot(h, wa_ref[...])
    at_ref[0] = lax.dot_general(wat_ref[...], h, nt, preferred_element_type=F32)


def inproj_call(x, norm_w, sc, sh, w_main, w_kt, w_a, w_at, *, tm=1024):
    B, L, D = x.shape
    n_main = w_main.shape[1]
    const = lambda b, i: (0, 0)
    tok = lambda b, i: (b, i, 0)
    bvec = lambda b, i: (b, 0, 0)
    return pl.pallas_call(
        functools.partial(_inproj_kernel, tm=tm),
        grid=(B, L // tm),
        in_specs=[pl.BlockSpec((1, tm, D), tok),
                  pl.BlockSpec((1, D), const),
                  pl.BlockSpec((1, 1, D), bvec),
                  pl.BlockSpec((1, 1, D), bvec),
                  pl.BlockSpec((D, n_main), const, pipeline_mode=pl.Buffered(1)),
                  pl.BlockSpec((GLA_QK, D), const),
                  pl.BlockSpec((D, LANES), const),
                  pl.BlockSpec((LANES, D), const)],
        out_specs=(pl.BlockSpec((1, tm, HY_COLS), tok),
                   pl.BlockSpec((1, tm, GLA_QK + 2 * GLA_VW), tok),
                   pl.BlockSpec((1, tm, 2 * D), tok),
                   pl.BlockSpec((1, tm // PAIR, GLA_QK, PAIR), lambda b, i: (b, i, 0, 0)),
                   pl.BlockSpec((1, tm, LANES), tok),
                   pl.BlockSpec((1, LANES, tm), lambda b, i: (b, 0, i))),
        out_shape=(jax.ShapeDtypeStruct((B, L, HY_COLS), BF16),
                   jax.ShapeDtypeStruct((B, L, GLA_QK + 2 * GLA_VW), BF16),
                   jax.ShapeDtypeStruct((B, L, 2 * D), BF16),
                   jax.ShapeDtypeStruct((B, L // PAIR, GLA_QK, PAIR), BF16),
                   jax.ShapeDtypeStruct((B, L, LANES), F32),
                   jax.ShapeDtypeStruct((B, LANES, L), F32)),
        compiler_params=_cp(("arbitrary", "arbitrary")),
        name="inproj",
    )(x, norm_w.reshape(1, D), sc, sh, w_main, w_kt, w_a, w_at)


def _short_conv_rows(u_ref, w_ref, t0, rows, col0, ncol, L):
    halo = 16
    lo = max(t0 - halo, 0)
    hi = min(t0 + rows + halo, L)
    n = hi - lo
    blk = u_ref[0, lo:hi, col0:col0 + ncol].astype(F32)
    ridx = lax.broadcasted_iota(jnp.int32, (n, 1), 0)
    prev = pltpu.roll(blk, 1, axis=0)
    nxt = pltpu.roll(blk, n - 1, axis=0)
    if lo == 0:
        prev = jnp.where(ridx == 0, 0.0, prev)
    if hi == L:
        nxt = jnp.where(ridx == n - 1, 0.0, nxt)
    w = w_ref[:, col0:col0 + ncol]
    out = prev * w[0:1] + blk * w[1:2] + nxt * w[2:3]
    return out[t0 - lo:t0 - lo + rows]


def _hyena_kernel(u_ref, cw_ref, bias_ref, fwd_ref, inv_ref, a_ref, b_ref, gn_ref, o_ref,
                  zb_ref, x0_ref, zp_ref, uv_ref, y_ref, *, L, M, rb, fb):
    W = HY_WIDTH
    P = L // M

    def sign_rows(n):
        t = lax.broadcasted_iota(jnp.int32, (n, 1), 0)
        return jnp.where((t & 1) == 0, 1.0, -1.0)

    znyq = [jnp.zeros((1, W), F32) for _ in range(P)]
    for t0 in range(0, L, rb):
        x1 = _short_conv_rows(u_ref, cw_ref, t0, rb, W, W, L)
        v = _short_conv_rows(u_ref, cw_ref, t0, rb, 2 * W, W, L)
        z = x1 * v
        zb_ref[t0:t0 + rb, :] = z.astype(BF16)
        x0_ref[t0:t0 + rb, :] = _short_conv_rows(u_ref, cw_ref, t0, rb, 0, W, L).astype(BF16)
        znyq[t0 // M] = znyq[t0 // M] + jnp.sum(sign_rows(rb) * z, axis=0, keepdims=True)

    for j in range(P):
        zp_ref[j] = _dot(fwd_ref[...], zb_ref[j * M:(j + 1) * M, :])

    bias = bias_ref[...]
    for i in range(P):
        for f0 in range(0, M, fb):
            u = jnp.zeros((fb, W), F32)
            v = jnp.zeros((fb, W), F32)
            for j in range(P):
                d = i - j + P - 1
                zre = zp_ref[j, f0:f0 + fb, :]
                p = zp_ref[j, M + f0:M + f0 + fb, :]
                a = a_ref[d, f0:f0 + fb, :]
                b = b_ref[d, f0:f0 + fb, :]
                u = u + zre * a + p * b
                v = v + p * a - zre * b
            uv_ref[f0:f0 + fb, :] = u.astype(BF16)
            uv_ref[M + f0:M + f0 + fb, :] = v.astype(BF16)
        y_ref[...] = _dot(inv_ref[...], uv_ref[...])
        ynyq = jnp.zeros((1, W), F32)
        for j in range(P):
            ynyq = ynyq + znyq[j] * gn_ref[i - j + P - 1, 0:1, :]
        for r0 in range(0, M, rb):
            rows = slice(i * M + r0, i * M + r0 + rb)
            z = zb_ref[rows, :].astype(F32)
            y = y_ref[r0:r0 + rb, :] + sign_rows(rb) * ynyq + bias * z
            o_ref[0, rows, :] = (x0_ref[rows, :].astype(F32) * y).astype(o_ref.dtype)


def hyena_call(u_hy, conv_w, hy_bias, fwd_tab, inv_tab, a_co, b_co, g_nyq, *, rb=256, fb=16):
    B, L, _ = u_hy.shape
    W = HY_WIDTH
    M = HY_BLOCK
    n_seg = a_co.shape[0]
    const2 = lambda b: (0, 0)
    const3 = lambda b: (0, 0, 0)
    return pl.pallas_call(
        functools.partial(_hyena_kernel, L=L, M=M, rb=rb, fb=fb),
        grid=(B,),
        in_specs=[pl.BlockSpec((1, L, HY_COLS), lambda b: (b, 0, 0)),
                  pl.BlockSpec((3, HY_COLS), const2),
                  pl.BlockSpec((1, W), const2),
                  pl.BlockSpec((2 * M, M), const2),
                  pl.BlockSpec((M, 2 * M), const2),
                  pl.BlockSpec((n_seg, M, W), const3, pipeline_mode=pl.Buffered(1)),
                  pl.BlockSpec((n_seg, M, W), const3, pipeline_mode=pl.Buffered(1)),
                  pl.BlockSpec((n_seg, 8, W), const3)],
        out_specs=pl.BlockSpec((1, L, W), lambda b: (b, 0, 0)),
        out_shape=jax.ShapeDtypeStruct((B, L, W), BF16),
        scratch_shapes=[pltpu.VMEM((L, W), BF16),
                        pltpu.VMEM((L, W), BF16),
                        pltpu.VMEM((L // M, 2 * M, W), F32),
                        pltpu.VMEM((2 * M, W), BF16),
                        pltpu.VMEM((M, W), F32)],
        compiler_params=_cp(("arbitrary",)),
        name="hyena",
    )(u_hy, conv_w, hy_bias.reshape(1, W), fwd_tab, inv_tab, a_co, b_co, g_nyq)


def _split3(x):
    x1 = x.astype(BF16)
    r = x - x1.astype(F32)
    x2 = r.astype(BF16)
    x3 = (r - x2.astype(F32)).astype(BF16)
    return x1, x2, x3


def _tri_left(m, x):
    n = x.shape[1]
    r = _dot(m, jnp.concatenate(_split3(x), axis=1))
    return r[:, 0:n] + r[:, n:2 * n] + r[:, 2 * n:3 * n]


def _tri_right(x, m):
    n = x.shape[0]
    r = _dot(jnp.concatenate(_split3(x), axis=0), m)
    return r[0:n] + r[n:2 * n] + r[2 * n:3 * n]


def _tri_mats(n, block):
    r = np.arange(n)
    same = (r[:, None] // block) == (r[None, :] // block)
    lower = same & (r[:, None] >= r[None, :])
    upper = same & (r[:, None] <= r[None, :])
    return jnp.asarray(lower, BF16), jnp.asarray(upper, BF16)


def _log_decay_tok(a, wup_ref, bias_ref):
    return _log_sigmoid(_dot_3(a, wup_ref[...]) + bias_ref[...]) * (1.0 / GLA_TAU)


def _log_decay_feat(at, wupt_ref, biast_ref):
    return _log_sigmoid(_dot_3(wupt_ref[...], at) + biast_ref[...]) * (1.0 / GLA_TAU)


def _ctx_kernel(x_ref, nw_ref, sc_ref, sh_ref, wkt_ref, wv_ref, wat_ref,
                wuptf_ref, biastf_ref, wuptb_ref, biastb_ref, lo_ref, up_ref, sf_ref, sb_ref):
    h = _modulated_norm(x_ref, nw_ref, sc_ref, sh_ref)
    nt = (((1,), (1,)), ((), ()))
    kt = lax.dot_general(wkt_ref[...], h, nt, preferred_element_type=F32)
    v = _dot(h, wv_ref[...]).astype(BF16)
    at = lax.dot_general(wat_ref[...], h, nt, preferred_element_type=F32)
    la_f = _log_decay_feat(at, wuptf_ref, biastf_ref)
    la_b = _log_decay_feat(at, wuptb_ref, biastb_ref)
    lo = lo_ref[...]
    up = up_ref[...]
    w_f = jnp.exp(_tri_right(la_f, lo) - la_f)
    w_b = jnp.exp(_tri_right(la_b, up) - la_b)
    kf = (kt * w_f).astype(BF16)
    kb = (kt * w_b).astype(BF16)
    for hd in range(GLA_HEADS):
        r0, c0 = hd * GLA_DK, hd * GLA_DV
        sf_ref[0, hd] = _dot(kf[r0:r0 + GLA_DK], v[:, c0:c0 + GLA_DV])
        sb_ref[0, hd] = _dot(kb[r0:r0 + GLA_DK], v[:, c0:c0 + GLA_DV])


def ctx_call(ctx, norm_w, csc, csh, w_kt, w_v, w_at, wupt_f, biast_f, wupt_b, biast_b):
    B, Lc, D = ctx.shape
    lo, up = _tri_mats(Lc, Lc)
    const2 = lambda b: (0, 0)
    const3 = lambda b: (0, 0, 0)
    st = jax.ShapeDtypeStruct((B, GLA_HEADS, GLA_DK, GLA_DV), F32)
    st_spec = pl.BlockSpec((1, GLA_HEADS, GLA_DK, GLA_DV), lambda b: (b, 0, 0, 0))
    return pl.pallas_call(
        _ctx_kernel,
        grid=(B,),
        in_specs=[pl.BlockSpec((1, Lc, D), lambda b: (b, 0, 0)),
                  pl.BlockSpec((1, D), const2),
                  pl.BlockSpec((1, 1, D), const3),
                  pl.BlockSpec((1, 1, D), const3),
                  pl.BlockSpec((GLA_QK, D), const2),
                  pl.BlockSpec((D, GLA_VW), const2),
                  pl.BlockSpec((LANES, D), const2),
                  pl.BlockSpec((GLA_QK, LANES), const2),
                  pl.BlockSpec((GLA_QK, 1), const2),
                  pl.BlockSpec((GLA_QK, LANES), const2),
                  pl.BlockSpec((GLA_QK, 1), const2),
                  pl.BlockSpec((Lc, Lc), const2),
                  pl.BlockSpec((Lc, Lc), const2)],
        out_specs=(st_spec, st_spec),
        out_shape=(st, st),
        compiler_params=_cp(("arbitrary",)),
        name="ctx",
    )(ctx, norm_w.reshape(1, D), csc, csh, w_kt, w_v, w_at, wupt_f, biast_f, wupt_b, biast_b, lo, up)


def _gla_kernel(qvr_ref, kt_ref, a_ref, at_ref, sf_ref, sb_ref,
                wupf_ref, biasf_ref, wupb_ref, biasb_ref,
                wuptf_ref, biastf_ref, wuptb_ref, biastb_ref,
                lo_ref, up_ref, nw_ref, o_ref,
                laf_ref, lab_ref, latf_ref, latb_ref, accf_ref, accb_ref, stf_ref, stb_ref, *, L, rb):
    n_pairs = L // PAIR
    C = GLA_CHUNK
    H, DK, DV = GLA_HEADS, GLA_DK, GLA_DV
    scale = DK ** -0.5

    for t0 in range(0, L, rb):
        a = a_ref[0, t0:t0 + rb, :]
        laf_ref[t0:t0 + rb, :] = _log_decay_tok(a, wupf_ref, biasf_ref)
        lab_ref[t0:t0 + rb, :] = _log_decay_tok(a, wupb_ref, biasb_ref)
    for p in range(n_pairs):
        at = at_ref[0, :, p * PAIR:(p + 1) * PAIR]
        latf_ref[p] = _log_decay_feat(at, wuptf_ref, biastf_ref)
        latb_ref[p] = _log_decay_feat(at, wuptb_ref, biastb_ref)

    lo = lo_ref[...]
    up = up_ref[...]
    rr = lax.broadcasted_iota(jnp.int32, (C, C), 0)
    cc = lax.broadcasted_iota(jnp.int32, (C, C), 1)

    def half_pair(p, la_ref, lat_ref, st_ref, acc_ref, forward):
        m_tok = lo if forward else up
        m_feat = up if forward else lo
        keep = (rr >= cc) if forward else (rr <= cc)
        last = C - 1 if forward else 0
        t0 = pl.multiple_of(p * PAIR, PAIR)
        b_tok = _tri_left(m_tok, la_ref[pl.ds(t0, PAIR), :])
        b_feat = _tri_right(lat_ref[p], m_feat)
        q = qvr_ref[0, pl.ds(t0, PAIR), 0:GLA_QK].astype(F32)
        qe = (q * scale * jnp.exp(b_tok)).astype(BF16)
        kt = kt_ref[0, p].astype(F32)
        ket = (kt * jnp.exp(-b_feat)).astype(BF16)
        v = qvr_ref[0, pl.ds(t0, PAIR), GLA_QK:GLA_QK + GLA_VW]
        for half in ((0, 1) if forward else (1, 0)):
            r0 = half * C
            b_last = b_feat[:, r0 + last:r0 + last + 1]
            kdt = (kt[:, r0:r0 + C] * jnp.exp(b_last - b_feat[:, r0:r0 + C])).astype(BF16)
            g = jnp.exp(b_last)
            for hd in range(H):
                k0, v0 = hd * DK, hd * DV
                qe_h = qe[r0:r0 + C, k0:k0 + DK]
                v_h = v[r0:r0 + C, v0:v0 + DV]
                s_prev = st_ref[hd]
                sc = _dot(qe_h, ket[k0:k0 + DK, r0:r0 + C])
                sc = jnp.where(keep, sc, 0.0).astype(BF16)
                acc_ref[pl.ds(t0 + r0, C), v0:v0 + DV] = _dot(sc, v_h) + _dot(qe_h, s_prev.astype(BF16))
                st_ref[hd] = g[k0:k0 + DK] * s_prev + _dot(kdt[k0:k0 + DK], v_h)

    for hd in range(H):
        stf_ref[hd] = sf_ref[0, hd]
        stb_ref[hd] = sb_ref[0, hd]

    def pair_body(i, carry):
        half_pair(i, laf_ref, latf_ref, stf_ref, accf_ref, True)
        half_pair(n_pairs - 1 - i, lab_ref, latb_ref, stb_ref, accb_ref, False)
        return carry

    lax.fori_loop(0, n_pairs, pair_body, 0, unroll=2)

    for t0 in range(0, L, rb):
        r = qvr_ref[0, t0:t0 + rb, GLA_QK + GLA_VW:GLA_QK + 2 * GLA_VW].astype(F32)
        for hd in range(H):
            v0 = hd * DV
            o = _rms_scale(accf_ref[t0:t0 + rb, v0:v0 + DV] + accb_ref[t0:t0 + rb, v0:v0 + DV])
            y = o * nw_ref[:, v0:v0 + DV] * _silu(r[:, v0:v0 + DV])
            o_ref[0, t0:t0 + rb, v0:v0 + DV] = y.astype(o_ref.dtype)


def gla_call(qvr, kt, a, at, s_f, s_b, wup_f, bias_f, wup_b, bias_b,
             wupt_f, biast_f, wupt_b, biast_b, norm_w, *, rb=256):
    B, L, _ = qvr.shape
    lo, up = _tri_mats(PAIR, GLA_CHUNK)
    const2 = lambda b: (0, 0)
    st_spec = pl.BlockSpec((1, GLA_HEADS, GLA_DK, GLA_DV), lambda b: (b, 0, 0, 0))
    return pl.pallas_call(
        functools.partial(_gla_kernel, L=L, rb=rb),
        grid=(B,),
        in_specs=[pl.BlockSpec((1, L, GLA_QK + 2 * GLA_VW), lambda b: (b, 0, 0)),
                  pl.BlockSpec((1, L // PAIR, GLA_QK, PAIR), lambda b: (b, 0, 0, 0)),
                  pl.BlockSpec((1, L, LANES), lambda b: (b, 0, 0)),
                  pl.BlockSpec((1, LANES, L), lambda b: (b, 0, 0)),
                  st_spec, st_spec,
                  pl.BlockSpec((LANES, GLA_QK), const2),
                  pl.BlockSpec((1, GLA_QK), const2),
                  pl.BlockSpec((LANES, GLA_QK), const2),
                  pl.BlockSpec((1, GLA_QK), const2),
                  pl.BlockSpec((GLA_QK, LANES), const2),
                  pl.BlockSpec((GLA_QK, 1), const2),
                  pl.BlockSpec((GLA_QK, LANES), const2),
                  pl.BlockSpec((GLA_QK, 1), const2),
                  pl.BlockSpec((PAIR, PAIR), const2),
                  pl.BlockSpec((PAIR, PAIR), const2),
                  pl.BlockSpec((1, GLA_VW), const2)],
        out_specs=pl.BlockSpec((1, L, GLA_VW), lambda b: (b, 0, 0)),
        out_shape=jax.ShapeDtypeStruct((B, L, GLA_VW), BF16),
        scratch_shapes=[pltpu.VMEM((L, GLA_QK), F32),
                        pltpu.VMEM((L, GLA_QK), F32),
                        pltpu.VMEM((L // PAIR, GLA_QK, PAIR), F32),
                        pltpu.VMEM((L // PAIR, GLA_QK, PAIR), F32),
                        pltpu.VMEM((L, GLA_VW), F32),
                        pltpu.VMEM((L, GLA_VW), F32),
                        pltpu.VMEM((GLA_HEADS, GLA_DK, GLA_DV), F32),
                        pltpu.VMEM((GLA_HEADS, GLA_DK, GLA_DV), F32)],
        compiler_params=_cp(("arbitrary",)),
        name="gla",
    )(qvr, kt, a, at, s_f, s_b, wup_f, bias_f, wup_b, bias_b,
      wupt_f, biast_f, wupt_b, biast_b, lo, up, norm_w.reshape(1, GLA_VW))


def _merge_kernel(x_ref, yhy_ref, ygla_ref, gate_ref, g1_ref, whp_ref, wgp_ref, wo_ref,
                  nw_ref, sc_ref, sh_ref, xn_ref, h2_ref):
    D = D_MODEL
    g_hy = _sigmoid(gate_ref[0, :, 0:D].astype(F32))
    g_gla = _sigmoid(gate_ref[0, :, D:2 * D].astype(F32))
    merged = g_hy * _dot(yhy_ref[0], whp_ref[...]) + g_gla * _dot(ygla_ref[0], wgp_ref[...])
    mix = _dot(merged.astype(BF16), wo_ref[...])
    xn = x_ref[0] + g1_ref[0] * mix
    xn_ref[0] = xn
    y = _rms_scale(xn) * nw_ref[...]
    h2_ref[0] = (y * (1.0 + sc_ref[0]) + sh_ref[0]).astype(h2_ref.dtype)


def merge_call(x, y_hy, y_gla, gates, g1, w_hy_proj, w_gla_proj, w_out, norm2_w, sc2, sh2, *, tm=1024):
    B, L, D = x.shape
    tok = lambda b, i: (b, i, 0)
    bvec = lambda b, i: (b, 0, 0)
    const = lambda b, i: (0, 0)
    return pl.pallas_call(
        _merge_kernel,
        grid=(B, L // tm),
        in_specs=[pl.BlockSpec((1, tm, D), tok),
                  pl.BlockSpec((1, tm, HY_WIDTH), tok),
                  pl.BlockSpec((1, tm, GLA_VW), tok),
                  pl.BlockSpec((1, tm, 2 * D), tok),
                  pl.BlockSpec((1, 1, D), bvec),
                  pl.BlockSpec((HY_WIDTH, D), const),
                  pl.BlockSpec((GLA_VW, D), const),
                  pl.BlockSpec((D, D), const),
                  pl.BlockSpec((1, D), const),
                  pl.BlockSpec((1, 1, D), bvec),
                  pl.BlockSpec((1, 1, D), bvec)],
        out_specs=(pl.BlockSpec((1, tm, D), tok), pl.BlockSpec((1, tm, D), tok)),
        out_shape=(jax.ShapeDtypeStruct((B, L, D), F32), jax.ShapeDtypeStruct((B, L, D), BF16)),
        compiler_params=_cp(("arbitrary", "arbitrary")),
        name="merge",
    )(x, y_hy, y_gla, gates, g1, w_hy_proj, w_gla_proj, w_out, norm2_w.reshape(1, D), sc2, sh2)


FFN_PAD = GRID_W


def _ffn_act_kernel(h_ref, wg_ref, wup_ref, dw_ref, dwb_ref, o_ref, g0_ref, g1_ref, s_ref, *, L, n_chunks):
    j = pl.program_id(1)
    hc = wup_ref.shape[1]
    mb = 512
    W = GRID_W
    sub = lax.broadcasted_iota(jnp.int32, (8, hc), 0)

    def project(gate_ref, chunk, t0):
        gate_ref[FFN_PAD + t0:FFN_PAD + t0 + mb, :] = _dot(h_ref[0, t0:t0 + mb, :], wg_ref[chunk])

    @pl.when(j == 0)
    def _():
        zeros = jnp.zeros((FFN_PAD, hc), F32)
        for ref in (g0_ref, g1_ref):
            ref[0:FFN_PAD, :] = zeros
            ref[FFN_PAD + L:FFN_PAD + L + FFN_PAD, :] = zeros
        for t0 in range(0, L, mb):
            project(g0_ref, 0, t0)

    def step(gate_ref, nxt_ref):
        nxt_chunk = jnp.minimum(j + 1, n_chunks - 1)
        dw = dw_ref[...]
        for t0 in range(0, L, mb):
            project(nxt_ref, nxt_chunk, t0)
            s_ref[t0:t0 + mb, :] = _dot(h_ref[0, t0:t0 + mb, :], wup_ref[...])
            for r0 in range(t0, t0 + mb, W):
                above = gate_ref[FFN_PAD + r0 - W:FFN_PAD + r0, :]
                centre = gate_ref[FFN_PAD + r0:FFN_PAD + r0 + W, :]
                below = gate_ref[FFN_PAD + r0 + W:FFN_PAD + r0 + 2 * W, :]
                col = [above * dw[dx:dx + 1] + centre * dw[3 + dx:4 + dx] + below * dw[6 + dx:7 + dx]
                       for dx in range(3)]
                left = pltpu.roll(col[0], 1, axis=0)
                right = pltpu.roll(col[2], W - 1, axis=0)
                left = jnp.concatenate([jnp.where(sub == 0, 0.0, left[0:8]), left[8:]], axis=0)
                right = jnp.concatenate([right[:W - 8], jnp.where(sub == 7, 0.0, right[W - 8:])], axis=0)
                conv = col[1] + left + right + dwb_ref[...]
                o_ref[0, r0:r0 + W, :] = (_silu(conv) * s_ref[r0:r0 + W, :]).astype(o_ref.dtype)

    parity = lax.rem(j, 2)
    pl.when(parity == 0)(lambda: step(g0_ref, g1_ref))
    pl.when(parity == 1)(lambda: step(g1_ref, g0_ref))


def ffn_act_call(h2, w_gate, w_up, dw, dw_bias, *, hc=256):
    B, L, D = h2.shape
    F = w_up.shape[1]
    n_chunks = F // hc
    wg = w_gate.reshape(D, n_chunks, hc).transpose(1, 0, 2)
    return pl.pallas_call(
        functools.partial(_ffn_act_kernel, L=L, n_chunks=n_chunks),
        grid=(B, n_chunks),
        in_specs=[pl.BlockSpec((1, L, D), lambda b, j: (b, 0, 0)),
                  pl.BlockSpec((n_chunks, D, hc), lambda b, j: (0, 0, 0), pipeline_mode=pl.Buffered(1)),
                  pl.BlockSpec((D, hc), lambda b, j: (0, j)),
                  pl.BlockSpec((9, hc), lambda b, j: (0, j)),
                  pl.BlockSpec((1, hc), lambda b, j: (0, j))],
        out_specs=pl.BlockSpec((1, L, hc), lambda b, j: (b, 0, j)),
        out_shape=jax.ShapeDtypeStruct((B, L, F), BF16),
        scratch_shapes=[pltpu.VMEM((L + 2 * FFN_PAD, hc), F32)] * 2 + [pltpu.VMEM((L, hc), F32)],
        compiler_params=_cp(("arbitrary", "arbitrary")),
        name="ffn_act",
    )(h2, wg, w_up, dw, dw_bias.reshape(1, F))


def _ffn_out_kernel(act_ref, xn_ref, g2_ref, wout_ref, fw_ref, o_ref):
    xo = xn_ref[0] + g2_ref[0] * _dot(act_ref[0], wout_ref[...])
    o_ref[0] = _rms_scale(xo) * fw_ref[...]


def ffn_out_call(act, x_new, g2, w_out, final_w, *, tm=1024):
    B, L, F = act.shape
    D = w_out.shape[1]
    tok = lambda b, i: (b, i, 0)
    const = lambda b, i: (0, 0)
    return pl.pallas_call(
        _ffn_out_kernel,
        grid=(B, L // tm),
        in_specs=[pl.BlockSpec((1, tm, F), tok),
                  pl.BlockSpec((1, tm, D), tok),
                  pl.BlockSpec((1, 1, D), lambda b, i: (b, 0, 0)),
                  pl.BlockSpec((F, D), const, pipeline_mode=pl.Buffered(1)),
                  pl.BlockSpec((1, D), const)],
        out_specs=pl.BlockSpec((1, tm, D), tok),
        out_shape=jax.ShapeDtypeStruct((B, L, D), F32),
        compiler_params=_cp(("arbitrary", "arbitrary")),
        name="ffn_out",
    )(act, x_new, g2, w_out, final_w.reshape(1, D))


def _pad_rows(w, rows, at=0):
    out = jnp.zeros((rows, w.shape[1]), w.dtype)
    return lax.dynamic_update_slice(out, w, (at, 0))


def kernel(x, c, ctx, c_ctx, w_ada, b_ada, norm1_w, norm2_w, w_in, hy_conv_w, hy_filt_w1, hy_filt_b1,
           hy_filt_w2, hy_filt_b2, hy_filt_w3, hy_filt_freq, hy_decay, hy_bias, gla_a_up_f, gla_a_bias_f,
           gla_a_up_b, gla_a_bias_b, gla_norm_w, w_hy_proj, w_gla_proj, w_out, ffn_w_in, ffn_dw,
           ffn_dw_bias, ffn_w_out, final_norm_w):
    assert w_ada.shape[0] == 1, "single-layer block"
    B, L, D = x.shape
    R = GLA_LOWRANK

    n_rows = -(-(B + 1) // 8) * 8
    c_rows = jnp.zeros((n_rows, D), F32).at[:B].set(c).at[B].set(c_ctx)
    mod = ada_call(c_rows, w_ada[0], b_ada[0])
    sh1, sc1, g1, sh2, sc2, g2 = [m[:B, None, :] for m in jnp.split(mod, 6, axis=-1)]
    csh1, csc1 = mod[B:B + 1, None, 0:D], mod[B:B + 1, None, D:2 * D]

    w = w_in[0]
    o_q = HY_COLS
    o_k = o_q + GLA_QK
    o_v = o_k + GLA_QK
    o_r = o_v + GLA_VW
    o_af = o_r + GLA_VW
    o_ab = o_af + R
    o_g = o_ab + R
    w_main = jnp.concatenate([w[:, :o_k], w[:, o_v:o_af], w[:, o_g:]], axis=1).astype(BF16)
    w_kt = w[:, o_k:o_v].T.astype(BF16)
    w_v = w[:, o_v:o_r].astype(BF16)
    w_a = jnp.pad(w[:, o_af:o_g], ((0, 0), (0, LANES - 2 * R))).astype(BF16)
    w_at = w_a.T
    wup_f = _pad_rows(gla_a_up_f[0], LANES, 0)
    wup_b = _pad_rows(gla_a_up_b[0], LANES, R)
    wupt_f, wupt_b = wup_f.T, wup_b.T
    bias_f, bias_b = gla_a_bias_f[0].reshape(1, GLA_QK), gla_a_bias_b[0].reshape(1, GLA_QK)
    biast_f, biast_b = bias_f.reshape(GLA_QK, 1), bias_b.reshape(GLA_QK, 1)

    cos32, sin32 = dft_tables(HY_BLOCK)
    a_co, b_co, g_nyq = filter_call(L, hy_filt_w1[0], hy_filt_b1[0], hy_filt_w2[0], hy_filt_b2[0],
                                    hy_filt_w3[0], hy_filt_freq[0], hy_decay[0], cos32, sin32)
    fwd_tab = jnp.concatenate([cos32, sin32], axis=0).astype(BF16)
    inv_tab = jnp.concatenate([cos32, sin32], axis=1).astype(BF16)

    s_f, s_b = ctx_call(ctx, norm1_w[0], csc1, csh1, w_kt, w_v, w_at, wupt_f, biast_f, wupt_b, biast_b)

    u_hy, qvr, gates, kt, a, at = inproj_call(x, norm1_w[0], sc1, sh1, w_main, w_kt, w_a, w_at)
    y_hy = hyena_call(u_hy, hy_conv_w[0], hy_bias[0], fwd_tab, inv_tab, a_co, b_co, g_nyq)
    y_gla = gla_call(qvr, kt, a, at, s_f, s_b, wup_f, bias_f, wup_b, bias_b,
                     wupt_f, biast_f, wupt_b, biast_b, gla_norm_w[0])
    x_new, h2 = merge_call(x, y_hy, y_gla, gates, g1, w_hy_proj[0].astype(BF16), w_gla_proj[0].astype(BF16),
                           w_out[0].astype(BF16), norm2_w[0], sc2, sh2)
    w_ffn = ffn_w_in[0]
    act = ffn_act_call(h2, w_ffn[:, FFN_HIDDEN:].astype(BF16), w_ffn[:, :FFN_HIDDEN].astype(BF16),
                       ffn_dw[0].reshape(9, FFN_HIDDEN), ffn_dw_bias[0])
    return ffn_out_call(act, x_new, g2, ffn_w_out[0].astype(BF16), final_norm_w)
```

```python
import functools
import math

import jax
import jax.numpy as jnp
import numpy as np
from jax import lax
from jax.experimental import pallas as pl
from jax.experimental.pallas import tpu as pltpu

F32 = jnp.float32
BF16 = jnp.bfloat16
HIGHEST = lax.Precision.HIGHEST

D_MODEL = 1024
GRID_W = 64
HY_WIDTH = 512
HY_BANDS = 16
HY_EMB = 1 + 2 * HY_BANDS
HY_FILT_HIDDEN = 64
GLA_HEADS = 4
GLA_DK = 64
GLA_DV = 128
GLA_LOWRANK = 16
GLA_TAU = 16.0
GLA_CHUNK = 64
FFN_HIDDEN = 2816
EPS = 1e-6
HY_COLS = 3 * HY_WIDTH
GLA_QK = GLA_HEADS * GLA_DK
GLA_VW = GLA_HEADS * GLA_DV

LANES = 128
PAIR = 2 * GLA_CHUNK
VMEM_LIMIT = 56 * 1024 * 1024


def _cp(sem):
    return pltpu.CompilerParams(dimension_semantics=sem, vmem_limit_bytes=VMEM_LIMIT)


def _dot(a, b):
    return jnp.dot(a, b, preferred_element_type=F32)


def _dot_hi(a, b):
    return jnp.dot(a, b, preferred_element_type=F32, precision=HIGHEST)


def _split2(x):
    x1 = x.astype(BF16)
    return x1, (x - x1.astype(F32)).astype(BF16)


def _dot_3(a, b):
    a1, a2 = _split2(a)
    b1, b2 = _split2(b)
    return _dot(jnp.concatenate([a1, a1, a2], axis=1), jnp.concatenate([b1, b2, b1], axis=0))


def _sigmoid(x):
    return 1.0 / (1.0 + jnp.exp(-x))


def _silu(x):
    return x * _sigmoid(x)


def _log_sigmoid(x):
    return jnp.minimum(x, 0.0) - jnp.log(1.0 + jnp.exp(-jnp.abs(x)))


def _rms_scale(xf):
    return xf * lax.rsqrt(jnp.mean(xf * xf, axis=-1, keepdims=True) + EPS)


def _ada_kernel(c_ref, w_ref, b_ref, o_ref):
    o_ref[...] = _dot_3(_silu(c_ref[...]), w_ref[...]) + b_ref[...]


def ada_call(c_rows, w_ada, b_ada):
    rows, d = c_rows.shape
    n = w_ada.shape[1]
    bn = 512
    return pl.pallas_call(
        _ada_kernel,
        grid=(n // bn,),
        in_specs=[pl.BlockSpec((rows, d), lambda j: (0, 0)),
                  pl.BlockSpec((d, bn), lambda j: (0, j)),
                  pl.BlockSpec((1, bn), lambda j: (0, j))],
        out_specs=pl.BlockSpec((rows, bn), lambda j: (0, j)),
        out_shape=jax.ShapeDtypeStruct((rows, n), F32),
        compiler_params=_cp(("arbitrary",)),
        name="ada",
    )(c_rows, w_ada, b_ada.reshape(1, n))


HY_BLOCK = 512


def _filt_mlp_kernel(feats_ref, t_ref, w1_ref, b1_ref, w2_ref, b2_ref, w3_ref, freq_ref, decay_ref, taps_ref):
    freq = freq_ref[...]
    h = jnp.sin(freq * (_dot_3(feats_ref[...], w1_ref[...]) + b1_ref[...]))
    h = jnp.sin(freq * (_dot_hi(h, w2_ref[...]) + b2_ref[...]))
    k = _dot_hi(h, w3_ref[...]) * jnp.exp(-t_ref[...] * decay_ref[...])
    L = k.shape[0]
    row = lax.broadcasted_iota(jnp.int32, (L, HY_WIDTH), 0)
    k_f = k[:, :HY_WIDTH]
    k_b = jnp.where(row == 0, 0.0, k[:, HY_WIDTH:])
    norm = jnp.sum(jnp.abs(k_f), axis=0, keepdims=True) + jnp.sum(jnp.abs(k_b), axis=0, keepdims=True)
    taps_ref[0:L, :] = k_f / norm
    taps_ref[L:2 * L, :] = k_b / norm


def _filt_blk_kernel(c_ref, s_ref, x_ref, cx_ref, sx_ref, aux_ref, *, M):
    x = x_ref[...]
    cx_ref[0] = _dot_3(c_ref[...], x)
    sx_ref[0] = _dot_3(s_ref[...], x)
    row = lax.broadcasted_iota(jnp.int32, (M, HY_WIDTH), 0)
    alt = jnp.sum(jnp.where((row & 1) == 0, x, -x), axis=0, keepdims=True)
    sub = lax.broadcasted_iota(jnp.int32, (8, HY_WIDTH), 0)
    aux_ref[0] = jnp.where(sub == 0, x[0:8], jnp.where(sub == 1, alt, 0.0))


def _filt_seg_kernel(cx1_ref, sx1_ref, aux1_ref, cx2_ref, sx2_ref, aux2_ref, a_ref, b_ref, gn_ref, *, M, P):
    d = pl.program_id(0) - (P - 1)
    f = lax.broadcasted_iota(jnp.int32, (M, 1), 0)
    sf = jnp.where((f & 1) == 0, 1.0, -1.0)
    s = jnp.where(d == 0, 1.0, sf)
    sg1 = jnp.where(d >= 0, -1.0, 1.0)
    sg2 = jnp.where(d >= 1, -1.0, 1.0)
    w = jnp.where(f == 0, 0.5 / M, 1.0 / M)
    x2_0 = aux2_ref[0, 0:1, :]
    a_ref[0] = w * (cx1_ref[0] + s * (cx2_ref[0] - x2_0))
    b_ref[0] = w * (sg1 * sx1_ref[0] + (sg2 * s) * sx2_ref[0])
    gn = (aux1_ref[0, 1:2, :] + aux2_ref[0, 1:2, :] - x2_0) * (0.5 / M)
    gn_ref[0] = jnp.broadcast_to(gn, gn_ref.shape[1:])


def filter_call(L, hy_filt_w1, hy_filt_b1, hy_filt_w2, hy_filt_b2, hy_filt_w3, hy_filt_freq, hy_decay,
                cos32, sin32):
    M = HY_BLOCK
    P = L // M
    t = jnp.linspace(0.0, 1.0, L, dtype=F32)[:, None]
    w = (2.0 * np.pi / L) * jnp.arange(L, dtype=F32)[:, None]
    f = jnp.linspace(1e-4, HY_BANDS - 1, HY_BANDS, dtype=F32)[None, :]
    feats = jnp.concatenate([t, jnp.cos(f * w), -jnp.sin(f * w)], axis=-1)
    feats = jnp.pad(feats, ((0, 0), (0, LANES - HY_EMB)))
    w1 = jnp.pad(hy_filt_w1, ((0, LANES - HY_EMB), (0, 0)))
    hid = HY_FILT_HIDDEN
    taps = pl.pallas_call(
        _filt_mlp_kernel,
        out_shape=jax.ShapeDtypeStruct((2 * L, HY_WIDTH), F32),
        compiler_params=pltpu.CompilerParams(vmem_limit_bytes=VMEM_LIMIT),
        name="filt_mlp",
    )(feats, t, w1, hy_filt_b1.reshape(1, hid), hy_filt_w2, hy_filt_b2.reshape(1, hid), hy_filt_w3,
      hy_filt_freq.reshape(1, hid), hy_decay.reshape(1, 2 * HY_WIDTH))

    blk = lambda n: jax.ShapeDtypeStruct((n, M, HY_WIDTH), F32)
    aux = lambda n: jax.ShapeDtypeStruct((n, 8, HY_WIDTH), F32)
    blk_spec = lambda imap: pl.BlockSpec((1, M, HY_WIDTH), imap)
    aux_spec = lambda imap: pl.BlockSpec((1, 8, HY_WIDTH), imap)
    own = lambda g: (g, 0, 0)
    cx, sx, ax = pl.pallas_call(
        functools.partial(_filt_blk_kernel, M=M),
        grid=(2 * P,),
        in_specs=[pl.BlockSpec((M, M), lambda g: (0, 0)),
                  pl.BlockSpec((M, M), lambda g: (0, 0)),
                  pl.BlockSpec((M, HY_WIDTH), lambda g: (g, 0))],
        out_specs=(blk_spec(own), blk_spec(own), aux_spec(own)),
        out_shape=(blk(2 * P), blk(2 * P), aux(2 * P)),
        compiler_params=_cp(("arbitrary",)),
        name="filt_blk",
    )(cos32, sin32, taps)

    def first(g):
        d = g - (P - 1)
        return (jnp.where(d >= 0, d, P - d), 0, 0)

    def second(g):
        d = g - (P - 1)
        return (jnp.where(d >= 1, d - 1, jnp.where(d == 0, P, P - d - 1)), 0, 0)

    n_seg = 2 * P - 1
    a_co, b_co, g_nyq = pl.pallas_call(
        functools.partial(_filt_seg_kernel, M=M, P=P),
        grid=(n_seg,),
        in_specs=[blk_spec(first), blk_spec(first), aux_spec(first),
                  blk_spec(second), blk_spec(second), aux_spec(second)],
        out_specs=(blk_spec(own), blk_spec(own), aux_spec(own)),
        out_shape=(blk(n_seg), blk(n_seg), aux(n_seg)),
        compiler_params=_cp(("arbitrary",)),
        name="filt_seg",
    )(cx, sx, ax, cx, sx, ax)
    return a_co, b_co, g_nyq


def dft_tables(M):
    n_fft = 2 * M
    idx = (jnp.arange(M, dtype=jnp.int32)[:, None] * jnp.arange(M, dtype=jnp.int32)[None, :]) % n_fft
    ang = idx.astype(F32) * (2.0 * np.pi / n_fft)
    return jnp.cos(ang), jnp.sin(ang)


def _modulated_norm(x_ref, nw_ref, sc_ref, sh_ref):
    y = _rms_scale(x_ref[0].astype(F32)) * nw_ref[...]
    return (y * (1.0 + sc_ref[0]) + sh_ref[0]).astype(BF16)


def _inproj_kernel(x_ref, nw_ref, sc_ref, sh_ref, w_ref, wkt_ref, wa_ref, wat_ref,
                   hy_ref, qvr_ref, gate_ref, kt_ref, a_ref, at_ref, *, tm):
    h = _modulated_norm(x_ref, nw_ref, sc_ref, sh_ref)
    col = 0
    for o_ref in (hy_ref, qvr_ref, gate_ref):
        width = o_ref.shape[-1]
        done = 0
        while done < width:
            step = min(512, width - done)
            o_ref[0, :, done:done + step] = _dot(h, w_ref[:, col:col + step]).astype(o_ref.dtype)
            done += step
            col += step
    nt = (((1,), (1,)), ((), ()))
    kt = lax.dot_general(wkt_ref[...], h, nt, preferred_element_type=F32).astype(BF16)
    for p in range(tm // PAIR):
        kt_ref[0, p] = kt[:, p * PAIR:(p + 1) * PAIR]
    a_ref[0] = _dot(h, wa_ref[...])
    at_ref[0] = lax.dot_general(wat_ref[...], h, nt, preferred_element_type=F32)


def inproj_call(x, norm_w, sc, sh, w_main, w_kt, w_a, w_at, *, tm=1024):
    B, L, D = x.shape
    n_main = w_main.shape[1]
    const = lambda b, i: (0, 0)
    tok = lambda b, i: (b, i, 0)
    bvec = lambda b, i: (b, 0, 0)
    return pl.pallas_call(
        functools.partial(_inproj_kernel, tm=tm),
        grid=(B, L // tm),
        in_specs=[pl.BlockSpec((1, tm, D), tok),
                  pl.BlockSpec((1, D), const),
                  pl.BlockSpec((1, 1, D), bvec),
                  pl.BlockSpec((1, 1, D), bvec),
                  pl.BlockSpec((D, n_main), const, pipeline_mode=pl.Buffered(1)),
                  pl.BlockSpec((GLA_QK, D), const),
                  pl.BlockSpec((D, LANES), const),
                  pl.BlockSpec((LANES, D), const)],
        out_specs=(pl.BlockSpec((1, tm, HY_COLS), tok),
                   pl.BlockSpec((1, tm, GLA_QK + 2 * GLA_VW), tok),
                   pl.BlockSpec((1, tm, 2 * D), tok),
                   pl.BlockSpec((1, tm // PAIR, GLA_QK, PAIR), lambda b, i: (b, i, 0, 0)),
                   pl.BlockSpec((1, tm, LANES), tok),
                   pl.BlockSpec((1, LANES, tm), lambda b, i: (b, 0, i))),
        out_shape=(jax.ShapeDtypeStruct((B, L, HY_COLS), BF16),
                   jax.ShapeDtypeStruct((B, L, GLA_QK + 2 * GLA_VW), BF16),
                   jax.ShapeDtypeStruct((B, L, 2 * D), BF16),
                   jax.ShapeDtypeStruct((B, L // PAIR, GLA_QK, PAIR), BF16),
                   jax.ShapeDtypeStruct((B, L, LANES), F32),
                   jax.ShapeDtypeStruct((B, LANES, L), F32)),
        compiler_params=_cp(("arbitrary", "arbitrary")),
        name="inproj",
    )(x, norm_w.reshape(1, D), sc, sh, w_main, w_kt, w_a, w_at)


def _short_conv_rows(u_ref, w_ref, t0, rows, col0, ncol, L):
    halo = 16
    lo = max(t0 - halo, 0)
    hi = min(t0 + rows + halo, L)
    n = hi - lo
    blk = u_ref[0, lo:hi, col0:col0 + ncol].astype(F32)
    ridx = lax.broadcasted_iota(jnp.int32, (n, 1), 0)
    prev = pltpu.roll(blk, 1, axis=0)
    nxt = pltpu.roll(blk, n - 1, axis=0)
    if lo == 0:
        prev = jnp.where(ridx == 0, 0.0, prev)
    if hi == L:
        nxt = jnp.where(ridx == n - 1, 0.0, nxt)
    w = w_ref[:, col0:col0 + ncol]
    out = prev * w[0:1] + blk * w[1:2] + nxt * w[2:3]
    return out[t0 - lo:t0 - lo + rows]


def _hyena_kernel(u_ref, cw_ref, bias_ref, fwd_ref, inv_ref, a_ref, b_ref, gn_ref, o_ref,
                  zb_ref, x0_ref, zp_ref, uv_ref, y_ref, *, L, M, rb, fb):
    W = HY_WIDTH
    P = L // M

    def sign_rows(n):
        t = lax.broadcasted_iota(jnp.int32, (n, 1), 0)
        return jnp.where((t & 1) == 0, 1.0, -1.0)

    znyq = [jnp.zeros((1, W), F32) for _ in range(P)]
    for t0 in range(0, L, rb):
        x1 = _short_conv_rows(u_ref, cw_ref, t0, rb, W, W, L)
        v = _short_conv_rows(u_ref, cw_ref, t0, rb, 2 * W, W, L)
        z = x1 * v
        zb_ref[t0:t0 + rb, :] = z.astype(BF16)
        x0_ref[t0:t0 + rb, :] = _short_conv_rows(u_ref, cw_ref, t0, rb, 0, W, L).astype(BF16)
        znyq[t0 // M] = znyq[t0 // M] + jnp.sum(sign_rows(rb) * z, axis=0, keepdims=True)

    for j in range(P):
        zp_ref[j] = _dot(fwd_ref[...], zb_ref[j * M:(j + 1) * M, :])

    bias = bias_ref[...]
    for i in range(P):
        for f0 in range(0, M, fb):
            u = jnp.zeros((fb, W), F32)
            v = jnp.zeros((fb, W), F32)
            for j in range(P):
                d = i - j + P - 1
                zre = zp_ref[j, f0:f0 + fb, :]
                p = zp_ref[j, M + f0:M + f0 + fb, :]
                a = a_ref[d, f0:f0 + fb, :]
                b = b_ref[d, f0:f0 + fb, :]
                u = u + zre * a + p * b
                v = v + p * a - zre * b
            uv_ref[f0:f0 + fb, :] = u.astype(BF16)
            uv_ref[M + f0:M + f0 + fb, :] = v.astype(BF16)
        y_ref[...] = _dot(inv_ref[...], uv_ref[...])
        ynyq = jnp.zeros((1, W), F32)
        for j in range(P):
            ynyq = ynyq + znyq[j] * gn_ref[i - j + P - 1, 0:1, :]
        for r0 in range(0, M, rb):
            rows = slice(i * M + r0, i * M + r0 + rb)
            z = zb_ref[rows, :].astype(F32)
            y = y_ref[r0:r0 + rb, :] + sign_rows(rb) * ynyq + bias * z
            o_ref[0, rows, :] = (x0_ref[rows, :].astype(F32) * y).astype(o_ref.dtype)


def hyena_call(u_hy, conv_w, hy_bias, fwd_tab, inv_tab, a_co, b_co, g_nyq, *, rb=256, fb=16):
    B, L, _ = u_hy.shape
    W = HY_WIDTH
    M = HY_BLOCK
    n_seg = a_co.shape[0]
    const2 = lambda b: (0, 0)
    const3 = lambda b: (0, 0, 0)
    return pl.pallas_call(
        functools.partial(_hyena_kernel, L=L, M=M, rb=rb, fb=fb),
        grid=(B,),
        in_specs=[pl.BlockSpec((1, L, HY_COLS), lambda b: (b, 0, 0)),
                  pl.BlockSpec((3, HY_COLS), const2),
                  pl.BlockSpec((1, W), const2),
                  pl.BlockSpec((2 * M, M), const2),
                  pl.BlockSpec((M, 2 * M), const2),
                  pl.BlockSpec((n_seg, M, W), const3, pipeline_mode=pl.Buffered(1)),
                  pl.BlockSpec((n_seg, M, W), const3, pipeline_mode=pl.Buffered(1)),
                  pl.BlockSpec((n_seg, 8, W), const3)],
        out_specs=pl.BlockSpec((1, L, W), lambda b: (b, 0, 0)),
        out_shape=jax.ShapeDtypeStruct((B, L, W), BF16),
        scratch_shapes=[pltpu.VMEM((L, W), BF16),
                        pltpu.VMEM((L, W), BF16),
                        pltpu.VMEM((L // M, 2 * M, W), F32),
                        pltpu.VMEM((2 * M, W), BF16),
                        pltpu.VMEM((M, W), F32)],
        compiler_params=_cp(("arbitrary",)),
        name="hyena",
    )(u_hy, conv_w, hy_bias.reshape(1, W), fwd_tab, inv_tab, a_co, b_co, g_nyq)


def _split3(x):
    x1 = x.astype(BF16)
    r = x - x1.astype(F32)
    x2 = r.astype(BF16)
    x3 = (r - x2.astype(F32)).astype(BF16)
    return x1, x2, x3


def _tri_left(m, x):
    n = x.shape[1]
    r = _dot(m, jnp.concatenate(_split3(x), axis=1))
    return r[:, 0:n] + r[:, n:2 * n] + r[:, 2 * n:3 * n]


def _tri_right(x, m):
    n = x.shape[0]
    r = _dot(jnp.concatenate(_split3(x), axis=0), m)
    return r[0:n] + r[n:2 * n] + r[2 * n:3 * n]


def _tri_mats(n, block):
    r = np.arange(n)
    same = (r[:, None] // block) == (r[None, :] // block)
    lower = same & (r[:, None] >= r[None, :])
    upper = same & (r[:, None] <= r[None, :])
    return jnp.asarray(lower, BF16), jnp.asarray(upper, BF16)


def _log_decay_tok(a, wup_ref, bias_ref):
    return _log_sigmoid(_dot_3(a, wup_ref[...]) + bias_ref[...]) * (1.0 / GLA_TAU)


def _log_decay_feat(at, wupt_ref, biast_ref):
    return _log_sigmoid(_dot_3(wupt_ref[...], at) + biast_ref[...]) * (1.0 / GLA_TAU)


def _ctx_kernel(x_ref, nw_ref, sc_ref, sh_ref, wkt_ref, wv_ref, wat_ref,
                wuptf_ref, biastf_ref, wuptb_ref, biastb_ref, lo_ref, up_ref, sf_ref, sb_ref):
    h = _modulated_norm(x_ref, nw_ref, sc_ref, sh_ref)
    nt = (((1,), (1,)), ((), ()))
    kt = lax.dot_general(wkt_ref[...], h, nt, preferred_element_type=F32)
    v = _dot(h, wv_ref[...]).astype(BF16)
    at = lax.dot_general(wat_ref[...], h, nt, preferred_element_type=F32)
    la_f = _log_decay_feat(at, wuptf_ref, biastf_ref)
    la_b = _log_decay_feat(at, wuptb_ref, biastb_ref)
    lo = lo_ref[...]
    up = up_ref[...]
    w_f = jnp.exp(_tri_right(la_f, lo) - la_f)
    w_b = jnp.exp(_tri_right(la_b, up) - la_b)
    kf = (kt * w_f).astype(BF16)
    kb = (kt * w_b).astype(BF16)
    for hd in range(GLA_HEADS):
        r0, c0 = hd * GLA_DK, hd * GLA_DV
        sf_ref[0, hd] = _dot(kf[r0:r0 + GLA_DK], v[:, c0:c0 + GLA_DV])
        sb_ref[0, hd] = _dot(kb[r0:r0 + GLA_DK], v[:, c0:c0 + GLA_DV])


def ctx_call(ctx, norm_w, csc, csh, w_kt, w_v, w_at, wupt_f, biast_f, wupt_b, biast_b):
    B, Lc, D = ctx.shape
    lo, up = _tri_mats(Lc, Lc)
    const2 = lambda b: (0, 0)
    const3 = lambda b: (0, 0, 0)
    st = jax.ShapeDtypeStruct((B, GLA_HEADS, GLA_DK, GLA_DV), F32)
    st_spec = pl.BlockSpec((1, GLA_HEADS, GLA_DK, GLA_DV), lambda b: (b, 0, 0, 0))
    return pl.pallas_call(
        _ctx_kernel,
        grid=(B,),
        in_specs=[pl.BlockSpec((1, Lc, D), lambda b: (b, 0, 0)),
                  pl.BlockSpec((1, D), const2),
                  pl.BlockSpec((1, 1, D), const3),
                  pl.BlockSpec((1, 1, D), const3),
                  pl.BlockSpec((GLA_QK, D), const2),
                  pl.BlockSpec((D, GLA_VW), const2),
                  pl.BlockSpec((LANES, D), const2),
                  pl.BlockSpec((GLA_QK, LANES), const2),
                  pl.BlockSpec((GLA_QK, 1), const2),
                  pl.BlockSpec((GLA_QK, LANES), const2),
                  pl.BlockSpec((GLA_QK, 1), const2),
                  pl.BlockSpec((Lc, Lc), const2),
                  pl.BlockSpec((Lc, Lc), const2)],
        out_specs=(st_spec, st_spec),
        out_shape=(st, st),
        compiler_params=_cp(("arbitrary",)),
        name="ctx",
    )(ctx, norm_w.reshape(1, D), csc, csh, w_kt, w_v, w_at, wupt_f, biast_f, wupt_b, biast_b, lo, up)


def _gla_kernel(qvr_ref, kt_ref, a_ref, at_ref, sf_ref, sb_ref,
                wupf_ref, biasf_ref, wupb_ref, biasb_ref,
                wuptf_ref, biastf_ref, wuptb_ref, biastb_ref,
                lo_ref, up_ref, nw_ref, o_ref,
                laf_ref, lab_ref, latf_ref, latb_ref, accf_ref, accb_ref, stf_ref, stb_ref, *, L, rb):
    n_pairs = L // PAIR
    C = GLA_CHUNK
    H, DK, DV = GLA_HEADS, GLA_DK, GLA_DV
    scale = DK ** -0.5

    for t0 in range(0, L, rb):
        a = a_ref[0, t0:t0 + rb, :]
        laf_ref[t0:t0 + rb, :] = _log_decay_tok(a, wupf_ref, biasf_ref)
        lab_ref[t0:t0 + rb, :] = _log_decay_tok(a, wupb_ref, biasb_ref)
    for p in range(n_pairs):
        at = at_ref[0, :, p * PAIR:(p + 1) * PAIR]
        latf_ref[p] = _log_decay_feat(at, wuptf_ref, biastf_ref)
        latb_ref[p] = _log_decay_feat(at, wuptb_ref, biastb_ref)

    lo = lo_ref[...]
    up = up_ref[...]
    rr = lax.broadcasted_iota(jnp.int32, (C, C), 0)
    cc = lax.broadcasted_iota(jnp.int32, (C, C), 1)

    def half_pair(p, la_ref, lat_ref, st_ref, acc_ref, forward):
        m_tok = lo if forward else up
        m_feat = up if forward else lo
        keep = (rr >= cc) if forward else (rr <= cc)
        last = C - 1 if forward else 0
        t0 = pl.multiple_of(p * PAIR, PAIR)
        b_tok = _tri_left(m_tok, la_ref[pl.ds(t0, PAIR), :])
        b_feat = _tri_right(lat_ref[p], m_feat)
        q = qvr_ref[0, pl.ds(t0, PAIR), 0:GLA_QK].astype(F32)
        qe = (q * scale * jnp.exp(b_tok)).astype(BF16)
        kt = kt_ref[0, p].astype(F32)
        ket = (kt * jnp.exp(-b_feat)).astype(BF16)
        v = qvr_ref[0, pl.ds(t0, PAIR), GLA_QK:GLA_QK + GLA_VW]
        for half in ((0, 1) if forward else (1, 0)):
            r0 = half * C
            b_last = b_feat[:, r0 + last:r0 + last + 1]
            kdt = (kt[:, r0:r0 + C] * jnp.exp(b_last - b_feat[:, r0:r0 + C])).astype(BF16)
            g = jnp.exp(b_last)
            for hd in range(H):
                k0, v0 = hd * DK, hd * DV
                qe_h = qe[r0:r0 + C, k0:k0 + DK]
                v_h = v[r0:r0 + C, v0:v0 + DV]
                s_prev = st_ref[hd]
                sc = _dot(qe_h, ket[k0:k0 + DK, r0:r0 + C])
                sc = jnp.where(keep, sc, 0.0).astype(BF16)
                acc_ref[pl.ds(t0 + r0, C), v0:v0 + DV] = _dot(sc, v_h) + _dot(qe_h, s_prev.astype(BF16))
                st_ref[hd] = g[k0:k0 + DK] * s_prev + _dot(kdt[k0:k0 + DK], v_h)

    for hd in range(H):
        stf_ref[hd] = sf_ref[0, hd]
        stb_ref[hd] = sb_ref[0, hd]

    def pair_body(i, carry):
        half_pair(i, laf_ref, latf_ref, stf_ref, accf_ref, True)
        half_pair(n_pairs - 1 - i, lab_ref, latb_ref, stb_ref, accb_ref, False)
        return carry

    lax.fori_loop(0, n_pairs, pair_body, 0, unroll=2)

    for t0 in range(0, L, rb):
        r = qvr_ref[0, t0:t0 + rb, GLA_QK + GLA_VW:GLA_QK + 2 * GLA_VW].astype(F32)
        for hd in range(H):
            v0 = hd * DV
            o = _rms_scale(accf_ref[t0:t0 + rb, v0:v0 + DV] + accb_ref[t0:t0 + rb, v0:v0 + DV])
            y = o * nw_ref[:, v0:v0 + DV] * _silu(r[:, v0:v0 + DV])
            o_ref[0, t0:t0 + rb, v0:v0 + DV] = y.astype(o_ref.dtype)


def gla_call(qvr, kt, a, at, s_f, s_b, wup_f, bias_f, wup_b, bias_b,
             wupt_f, biast_f, wupt_b, biast_b, norm_w, *, rb=256):
    B, L, _ = qvr.shape
    lo, up = _tri_mats(PAIR, GLA_CHUNK)
    const2 = lambda b: (0, 0)
    st_spec = pl.BlockSpec((1, GLA_HEADS, GLA_DK, GLA_DV), lambda b: (b, 0, 0, 0))
    return pl.pallas_call(
        functools.partial(_gla_kernel, L=L, rb=rb),
        grid=(B,),
        in_specs=[pl.BlockSpec((1, L, GLA_QK + 2 * GLA_VW), lambda b: (b, 0, 0)),
                  pl.BlockSpec((1, L // PAIR, GLA_QK, PAIR), lambda b: (b, 0, 0, 0)),
                  pl.BlockSpec((1, L, LANES), lambda b: (b, 0, 0)),
                  pl.BlockSpec((1, LANES, L), lambda b: (b, 0, 0)),
                  st_spec, st_spec,
                  pl.BlockSpec((LANES, GLA_QK), const2),
                  pl.BlockSpec((1, GLA_QK), const2),
                  pl.BlockSpec((LANES, GLA_QK), const2),
                  pl.BlockSpec((1, GLA_QK), const2),
                  pl.BlockSpec((GLA_QK, LANES), const2),
                  pl.BlockSpec((GLA_QK, 1), const2),
                  pl.BlockSpec((GLA_QK, LANES), const2),
                  pl.BlockSpec((GLA_QK, 1), const2),
                  pl.BlockSpec((PAIR, PAIR), const2),
                  pl.BlockSpec((PAIR, PAIR), const2),
                  pl.BlockSpec((1, GLA_VW), const2)],
        out_specs=pl.BlockSpec((1, L, GLA_VW), lambda b: (b, 0, 0)),
        out_shape=jax.ShapeDtypeStruct((B, L, GLA_VW), BF16),
        scratch_shapes=[pltpu.VMEM((L, GLA_QK), F32),
                        pltpu.VMEM((L, GLA_QK), F32),
                        pltpu.VMEM((L // PAIR, GLA_QK, PAIR), F32),
                        pltpu.VMEM((L // PAIR, GLA_QK, PAIR), F32),
                        pltpu.VMEM((L, GLA_VW), F32),
                        pltpu.VMEM((L, GLA_VW), F32),
                        pltpu.VMEM((GLA_HEADS, GLA_DK, GLA_DV), F32),
                        pltpu.VMEM((GLA_HEADS, GLA_DK, GLA_DV), F32)],
        compiler_params=_cp(("arbitrary",)),
        name="gla",
    )(qvr, kt, a, at, s_f, s_b, wup_f, bias_f, wup_b, bias_b,
      wupt_f, biast_f, wupt_b, biast_b, lo, up, norm_w.reshape(1, GLA_VW))


def _merge_kernel(x_ref, yhy_ref, ygla_ref, gate_ref, g1_ref, whp_ref, wgp_ref, wo_ref,
                  nw_ref, sc_ref, sh_ref, xn_ref, h2_ref):
    D = D_MODEL
    g_hy = _sigmoid(gate_ref[0, :, 0:D].astype(F32))
    g_gla = _sigmoid(gate_ref[0, :, D:2 * D].astype(F32))
    merged = g_hy * _dot(yhy_ref[0], whp_ref[...]) + g_gla * _dot(ygla_ref[0], wgp_ref[...])
    mix = _dot(merged.astype(BF16), wo_ref[...])
    xn = x_ref[0] + g1_ref[0] * mix
    xn_ref[0] = xn
    y = _rms_scale(xn) * nw_ref[...]
    h2_ref[0] = (y * (1.0 + sc_ref[0]) + sh_ref[0]).astype(h2_ref.dtype)


def merge_call(x, y_hy, y_gla, gates, g1, w_hy_proj, w_gla_proj, w_out, norm2_w, sc2, sh2, *, tm=1024):
    B, L, D = x.shape
    tok = lambda b, i: (b, i, 0)
    bvec = lambda b, i: (b, 0, 0)
    const = lambda b, i: (0, 0)
    return pl.pallas_call(
        _merge_kernel,
        grid=(B, L // tm),
        in_specs=[pl.BlockSpec((1, tm, D), tok),
                  pl.BlockSpec((1, tm, HY_WIDTH), tok),
                  pl.BlockSpec((1, tm, GLA_VW), tok),
                  pl.BlockSpec((1, tm, 2 * D), tok),
                  pl.BlockSpec((1, 1, D), bvec),
                  pl.BlockSpec((HY_WIDTH, D), const),
                  pl.BlockSpec((GLA_VW, D), const),
                  pl.BlockSpec((D, D), const),
                  pl.BlockSpec((1, D), const),
                  pl.BlockSpec((1, 1, D), bvec),
                  pl.BlockSpec((1, 1, D), bvec)],
        out_specs=(pl.BlockSpec((1, tm, D), tok), pl.BlockSpec((1, tm, D), tok)),
        out_shape=(jax.ShapeDtypeStruct((B, L, D), F32), jax.ShapeDtypeStruct((B, L, D), BF16)),
        compiler_params=_cp(("arbitrary", "arbitrary")),
        name="merge",
    )(x, y_hy, y_gla, gates, g1, w_hy_proj, w_gla_proj, w_out, norm2_w.reshape(1, D), sc2, sh2)


FFN_PAD = GRID_W


def _ffn_act_kernel(h_ref, wg_ref, wup_ref, dw_ref, dwb_ref, o_ref, g0_ref, g1_ref, s_ref, *, L, n_chunks):
    j = pl.program_id(1)
    hc = wup_ref.shape[1]
    mb = 512
    W = GRID_W
    sub = lax.broadcasted_iota(jnp.int32, (8, hc), 0)

    def project(gate_ref, chunk, t0):
        gate_ref[FFN_PAD + t0:FFN_PAD + t0 + mb, :] = _dot(h_ref[0, t0:t0 + mb, :], wg_ref[chunk])

    @pl.when(j == 0)
    def _():
        zeros = jnp.zeros((FFN_PAD, hc), F32)
        for ref in (g0_ref, g1_ref):
            ref[0:FFN_PAD, :] = zeros
            ref[FFN_PAD + L:FFN_PAD + L + FFN_PAD, :] = zeros
        for t0 in range(0, L, mb):
            project(g0_ref, 0, t0)

    def step(gate_ref, nxt_ref):
        nxt_chunk = jnp.minimum(j + 1, n_chunks - 1)
        dw = dw_ref[...]
        for t0 in range(0, L, mb):
            project(nxt_ref, nxt_chunk, t0)
            s_ref[t0:t0 + mb, :] = _dot(h_ref[0, t0:t0 + mb, :], wup_ref[...])
            for r0 in range(t0, t0 + mb, W):
                above = gate_ref[FFN_PAD + r0 - W:FFN_PAD + r0, :]
                centre = gate_ref[FFN_PAD + r0:FFN_PAD + r0 + W, :]
                below = gate_ref[FFN_PAD + r0 + W:FFN_PAD + r0 + 2 * W, :]
                col = [above * dw[dx:dx + 1] + centre * dw[3 + dx:4 + dx] + below * dw[6 + dx:7 + dx]
                       for dx in range(3)]
                left = pltpu.roll(col[0], 1, axis=0)
                right = pltpu.roll(col[2], W - 1, axis=0)
                left = jnp.concatenate([jnp.where(sub == 0, 0.0, left[0:8]), left[8:]], axis=0)
                right = jnp.concatenate([right[:W - 8], jnp.where(sub == 7, 0.0, right[W - 8:])], axis=0)
                conv = col[1] + left + right + dwb_ref[...]
                o_ref[0, r0:r0 + W, :] = (_silu(conv) * s_ref[r0:r0 + W, :]).astype(o_ref.dtype)

    parity = lax.rem(j, 2)
    pl.when(parity == 0)(lambda: step(g0_ref, g1_ref))
    pl.when(parity == 1)(lambda: step(g1_ref, g0_ref))


def ffn_act_call(h2, w_gate, w_up, dw, dw_bias, *, hc=256):
    B, L, D = h2.shape
    F = w_up.shape[1]
    n_chunks = F // hc
    wg = w_gate.reshape(D, n_chunks, hc).transpose(1, 0, 2)
    return pl.pallas_call(
        functools.partial(_ffn_act_kernel, L=L, n_chunks=n_chunks),
        grid=(B, n_chunks),
        in_specs=[pl.BlockSpec((1, L, D), lambda b, j: (b, 0, 0)),
                  pl.BlockSpec((n_chunks, D, hc), lambda b, j: (0, 0, 0), pipeline_mode=pl.Buffered(1)),
                  pl.BlockSpec((D, hc), lambda b, j: (0, j)),
                  pl.BlockSpec((9, hc), lambda b, j: (0, j)),
                  pl.BlockSpec((1, hc), lambda b, j: (0, j))],
        out_specs=pl.BlockSpec((1, L, hc), lambda b, j: (b, 0, j)),
        out_shape=jax.ShapeDtypeStruct((B, L, F), BF16),
        scratch_shapes=[pltpu.VMEM((L + 2 * FFN_PAD, hc), F32)] * 2 + [pltpu.VMEM((L, hc), F32)],
        compiler_params=_cp(("arbitrary", "arbitrary")),
        name="ffn_act",
    )(h2, wg, w_up, dw, dw_bias.reshape(1, F))


def _ffn_out_kernel(act_ref, xn_ref, g2_ref, wout_ref, fw_ref, o_ref):
    xo = xn_ref[0] + g2_ref[0] * _dot(act_ref[0], wout_ref[...])
    o_ref[0] = _rms_scale(xo) * fw_ref[...]


def ffn_out_call(act, x_new, g2, w_out, final_w, *, tm=1024):
    B, L, F = act.shape
    D = w_out.shape[1]
    tok = lambda b, i: (b, i, 0)
    const = lambda b, i: (0, 0)
    return pl.pallas_call(
        _ffn_out_kernel,
        grid=(B, L // tm),
        in_specs=[pl.BlockSpec((1, tm, F), tok),
                  pl.BlockSpec((1, tm, D), tok),
                  pl.BlockSpec((1, 1, D), lambda b, i: (b, 0, 0)),
                  pl.BlockSpec((F, D), const, pipeline_mode=pl.Buffered(1)),
                  pl.BlockSpec((1, D), const)],
        out_specs=pl.BlockSpec((1, tm, D), tok),
        out_shape=jax.ShapeDtypeStruct((B, L, D), F32),
        compiler_params=_cp(("arbitrary", "arbitrary")),
        name="ffn_out",
    )(act, x_new, g2, w_out, final_w.reshape(1, D))


def _pad_rows(w, rows, at=0):
    out = jnp.zeros((rows, w.shape[1]), w.dtype)
    return lax.dynamic_update_slice(out, w, (at, 0))


def kernel(x, c, ctx, c_ctx, w_ada, b_ada, norm1_w, norm2_w, w_in, hy_conv_w, hy_filt_w1, hy_filt_b1,
           hy_filt_w2, hy_filt_b2, hy_filt_w3, hy_filt_freq, hy_decay, hy_bias, gla_a_up_f, gla_a_bias_f,
           gla_a_up_b, gla_a_bias_b, gla_norm_w, w_hy_proj, w_gla_proj, w_out, ffn_w_in, ffn_dw,
           ffn_dw_bias, ffn_w_out, final_norm_w):
    assert w_ada.shape[0] == 1, "single-layer block"
    B, L, D = x.shape
    R = GLA_LOWRANK

    n_rows = -(-(B + 1) // 8) * 8
    c_rows = jnp.zeros((n_rows, D), F32).at[:B].set(c).at[B].set(c_ctx)
    mod = ada_call(c_rows, w_ada[0], b_ada[0])
    sh1, sc1, g1, sh2, sc2, g2 = [m[:B, None, :] for m in jnp.split(mod, 6, axis=-1)]
    csh1, csc1 = mod[B:B + 1, None, 0:D], mod[B:B + 1, None, D:2 * D]

    w = w_in[0]
    o_q = HY_COLS
    o_k = o_q + GLA_QK
    o_v = o_k + GLA_QK
    o_r = o_v + GLA_VW
    o_af = o_r + GLA_VW
    o_ab = o_af + R
    o_g = o_ab + R
    w_main = jnp.concatenate([w[:, :o_k], w[:, o_v:o_af], w[:, o_g:]], axis=1).astype(BF16)
    w_kt = w[:, o_k:o_v].T.astype(BF16)
    w_v = w[:, o_v:o_r].astype(BF16)
    w_a = jnp.pad(w[:, o_af:o_g], ((0, 0), (0, LANES - 2 * R))).astype(BF16)
    w_at = w_a.T
    wup_f = _pad_rows(gla_a_up_f[0], LANES, 0)
    wup_b = _pad_rows(gla_a_up_b[0], LANES, R)
    wupt_f, wupt_b = wup_f.T, wup_b.T
    bias_f, bias_b = gla_a_bias_f[0].reshape(1, GLA_QK), gla_a_bias_b[0].reshape(1, GLA_QK)
    biast_f, biast_b = bias_f.reshape(GLA_QK, 1), bias_b.reshape(GLA_QK, 1)

    cos32, sin32 = dft_tables(HY_BLOCK)
    a_co, b_co, g_nyq = filter_call(L, hy_filt_w1[0], hy_filt_b1[0], hy_filt_w2[0], hy_filt_b2[0],
                                    hy_filt_w3[0], hy_filt_freq[0], hy_decay[0], cos32, sin32)
    fwd_tab = jnp.concatenate([cos32, sin32], axis=0).astype(BF16)
    inv_tab = jnp.concatenate([cos32, sin32], axis=1).astype(BF16)

    s_f, s_b = ctx_call(ctx, norm1_w[0], csc1, csh1, w_kt, w_v, w_at, wupt_f, biast_f, wupt_b, biast_b)

    u_hy, qvr, gates, kt, a, at = inproj_call(x, norm1_w[0], sc1, sh1, w_main, w_kt, w_a, w_at)
    y_hy = hyena_call(u_hy, hy_conv_w[0], hy_bias[0], fwd_tab, inv_tab, a_co, b_co, g_nyq)
    y_gla = gla_call(qvr, kt, a, at, s_f, s_b, wup_f, bias_f, wup_b, bias_b,
                     wupt_f, biast_f, wupt_b, biast_b, gla_norm_w[0])
    x_new, h2 = merge_call(x, y_hy, y_gla, gates, g1, w_hy_proj[0].astype(BF16), w_gla_proj[0].astype(BF16),
                           w_out[0].astype(BF16), norm2_w[0], sc2, sh2)
    w_ffn = ffn_w_in[0]
    act = ffn_act_call(h2, w_ffn[:, FFN_HIDDEN:].astype(BF16), w_ffn[:, :FFN_HIDDEN].astype(BF16),
                       ffn_dw[0].reshape(9, FFN_HIDDEN), ffn_dw_bias[0])
    return ffn_out_call(act, x_new, g2, ffn_w_out[0].astype(BF16), final_norm_w)
```

```python
import functools
import math

import jax
import jax.numpy as jnp
import numpy as np
from jax import lax
from jax.experimental import pallas as pl
from jax.experimental.pallas import tpu as pltpu

F32 = jnp.float32
BF16 = jnp.bfloat16
HIGHEST = lax.Precision.HIGHEST

D_MODEL = 1024
GRID_W = 64
HY_WIDTH = 512
HY_BANDS = 16
HY_EMB = 1 + 2 * HY_BANDS
HY_FILT_HIDDEN = 64
GLA_HEADS = 4
GLA_DK = 64
GLA_DV = 128
GLA_LOWRANK = 16
GLA_TAU = 16.0
GLA_CHUNK = 64
FFN_HIDDEN = 2816
EPS = 1e-6
HY_COLS = 3 * HY_WIDTH
GLA_QK = GLA_HEADS * GLA_DK
GLA_VW = GLA_HEADS * GLA_DV

LANES = 128
PAIR = 2 * GLA_CHUNK
VMEM_LIMIT = 56 * 1024 * 1024


def _cp(sem):
    return pltpu.CompilerParams(dimension_semantics=sem, vmem_limit_bytes=VMEM_LIMIT)


def _dot(a, b):
    return jnp.dot(a, b, preferred_element_type=F32)


def _dot_hi(a, b):
    return jnp.dot(a, b, preferred_element_type=F32, precision=HIGHEST)


def _split2(x):
    x1 = x.astype(BF16)
    return x1, (x - x1.astype(F32)).astype(BF16)


def _dot_3(a, b):
    a1, a2 = _split2(a)
    b1, b2 = _split2(b)
    return _dot(jnp.concatenate([a1, a1, a2], axis=1), jnp.concatenate([b1, b2, b1], axis=0))


def _sigmoid(x):
    return 1.0 / (1.0 + jnp.exp(-x))


def _silu(x):
    return x * _sigmoid(x)


def _log_sigmoid(x):
    return jnp.minimum(x, 0.0) - jnp.log(1.0 + jnp.exp(-jnp.abs(x)))


def _rms_scale(xf):
    return xf * lax.rsqrt(jnp.mean(xf * xf, axis=-1, keepdims=True) + EPS)


def _ada_kernel(c_ref, w_ref, b_ref, o_ref):
    o_ref[...] = _dot_3(_silu(c_ref[...]), w_ref[...]) + b_ref[...]


def ada_call(c_rows, w_ada, b_ada):
    rows, d = c_rows.shape
    n = w_ada.shape[1]
    bn = 512
    return pl.pallas_call(
        _ada_kernel,
        grid=(n // bn,),
        in_specs=[pl.BlockSpec((rows, d), lambda j: (0, 0)),
                  pl.BlockSpec((d, bn), lambda j: (0, j)),
                  pl.BlockSpec((1, bn), lambda j: (0, j))],
        out_specs=pl.BlockSpec((rows, bn), lambda j: (0, j)),
        out_shape=jax.ShapeDtypeStruct((rows, n), F32),
        compiler_params=_cp(("arbitrary",)),
        name="ada",
    )(c_rows, w_ada, b_ada.reshape(1, n))


HY_BLOCK = 512


def _filt_mlp_kernel(feats_ref, t_ref, w1_ref, b1_ref, w2_ref, b2_ref, w3_ref, freq_ref, decay_ref, taps_ref):
    freq = freq_ref[...]
    h = jnp.sin(freq * (_dot_3(feats_ref[...], w1_ref[...]) + b1_ref[...]))
    h = jnp.sin(freq * (_dot_hi(h, w2_ref[...]) + b2_ref[...]))
    k = _dot_hi(h, w3_ref[...]) * jnp.exp(-t_ref[...] * decay_ref[...])
    L = k.shape[0]
    row = lax.broadcasted_iota(jnp.int32, (L, HY_WIDTH), 0)
    k_f = k[:, :HY_WIDTH]
    k_b = jnp.where(row == 0, 0.0, k[:, HY_WIDTH:])
    norm = jnp.sum(jnp.abs(k_f), axis=0, keepdims=True) + jnp.sum(jnp.abs(k_b), axis=0, keepdims=True)
    taps_ref[0:L, :] = k_f / norm
    taps_ref[L:2 * L, :] = k_b / norm


def _filt_blk_kernel(c_ref, s_ref, x_ref, cx_ref, sx_ref, aux_ref, *, M):
    x = x_ref[...]
    cx_ref[0] = _dot_3(c_ref[...], x)
    sx_ref[0] = _dot_3(s_ref[...], x)
    row = lax.broadcasted_iota(jnp.int32, (M, HY_WIDTH), 0)
    alt = jnp.sum(jnp.where((row & 1) == 0, x, -x), axis=0, keepdims=True)
    sub = lax.broadcasted_iota(jnp.int32, (8, HY_WIDTH), 0)
    aux_ref[0] = jnp.where(sub == 0, x[0:8], jnp.where(sub == 1, alt, 0.0))


def _filt_seg_kernel(cx1_ref, sx1_ref, aux1_ref, cx2_ref, sx2_ref, aux2_ref, a_ref, b_ref, gn_ref, *, M, P):
    d = pl.program_id(0) - (P - 1)
    f = lax.broadcasted_iota(jnp.int32, (M, 1), 0)
    sf = jnp.where((f & 1) == 0, 1.0, -1.0)
    s = jnp.where(d == 0, 1.0, sf)
    sg1 = jnp.where(d >= 0, -1.0, 1.0)
    sg2 = jnp.where(d >= 1, -1.0, 1.0)
    w = jnp.where(f == 0, 0.5 / M, 1.0 / M)
    x2_0 = aux2_ref[0, 0:1, :]
    a_ref[0] = w * (cx1_ref[0] + s * (cx2_ref[0] - x2_0))
    b_ref[0] = w * (sg1 * sx1_ref[0] + (sg2 * s) * sx2_ref[0])
    gn = (aux1_ref[0, 1:2, :] + aux2_ref[0, 1:2, :] - x2_0) * (0.5 / M)
    gn_ref[0] = jnp.broadcast_to(gn, gn_ref.shape[1:])


def filter_call(L, hy_filt_w1, hy_filt_b1, hy_filt_w2, hy_filt_b2, hy_filt_w3, hy_filt_freq, hy_decay,
                cos32, sin32):
    M = HY_BLOCK
    P = L // M
    t = jnp.linspace(0.0, 1.0, L, dtype=F32)[:, None]
    w = (2.0 * np.pi / L) * jnp.arange(L, dtype=F32)[:, None]
    f = jnp.linspace(1e-4, HY_BANDS - 1, HY_BANDS, dtype=F32)[None, :]
    feats = jnp.concatenate([t, jnp.cos(f * w), -jnp.sin(f * w)], axis=-1)
    feats = jnp.pad(feats, ((0, 0), (0, LANES - HY_EMB)))
    w1 = jnp.pad(hy_filt_w1, ((0, LANES - HY_EMB), (0, 0)))
    hid = HY_FILT_HIDDEN
    taps = pl.pallas_call(
        _filt_mlp_kernel,
        out_shape=jax.ShapeDtypeStruct((2 * L, HY_WIDTH), F32),
        compiler_params=pltpu.CompilerParams(vmem_limit_bytes=VMEM_LIMIT),
        name="filt_mlp",
    )(feats, t, w1, hy_filt_b1.reshape(1, hid), hy_filt_w2, hy_filt_b2.reshape(1, hid), hy_filt_w3,
      hy_filt_freq.reshape(1, hid), hy_decay.reshape(1, 2 * HY_WIDTH))

    blk = lambda n: jax.ShapeDtypeStruct((n, M, HY_WIDTH), F32)
    aux = lambda n: jax.ShapeDtypeStruct((n, 8, HY_WIDTH), F32)
    blk_spec = lambda imap: pl.BlockSpec((1, M, HY_WIDTH), imap)
    aux_spec = lambda imap: pl.BlockSpec((1, 8, HY_WIDTH), imap)
    own = lambda g: (g, 0, 0)
    cx, sx, ax = pl.pallas_call(
        functools.partial(_filt_blk_kernel, M=M),
        grid=(2 * P,),
        in_specs=[pl.BlockSpec((M, M), lambda g: (0, 0)),
                  pl.BlockSpec((M, M), lambda g: (0, 0)),
                  pl.BlockSpec((M, HY_WIDTH), lambda g: (g, 0))],
        out_specs=(blk_spec(own), blk_spec(own), aux_spec(own)),
        out_shape=(blk(2 * P), blk(2 * P), aux(2 * P)),
        compiler_params=_cp(("arbitrary",)),
        name="filt_blk",
    )(cos32, sin32, taps)

    def first(g):
        d = g - (P - 1)
        return (jnp.where(d >= 0, d, P - d), 0, 0)

    def second(g):
        d = g - (P - 1)
        return (jnp.where(d >= 1, d - 1, jnp.where(d == 0, P, P - d - 1)), 0, 0)

    n_seg = 2 * P - 1
    a_co, b_co, g_nyq = pl.pallas_call(
        functools.partial(_filt_seg_kernel, M=M, P=P),
        grid=(n_seg,),
        in_specs=[blk_spec(first), blk_spec(first), aux_spec(first),
                  blk_spec(second), blk_spec(second), aux_spec(second)],
        out_specs=(blk_spec(own), blk_spec(own), aux_spec(own)),
        out_shape=(blk(n_seg), blk(n_seg), aux(n_seg)),
        compiler_params=_cp(("arbitrary",)),
        name="filt_seg",
    )(cx, sx, ax, cx, sx, ax)
    return a_co, b_co, g_nyq


def dft_tables(M):
    n_fft = 2 * M
    idx = (jnp.arange(M, dtype=jnp.int32)[:, None] * jnp.arange(M, dtype=jnp.int32)[None, :]) % n_fft
    ang = idx.astype(F32) * (2.0 * np.pi / n_fft)
    return jnp.cos(ang), jnp.sin(ang)


def _modulated_norm(x_ref, nw_ref, sc_ref, sh_ref):
    y = _rms_scale(x_ref[0].astype(F32)) * nw_ref[...]
    return (y * (1.0 + sc_ref[0]) + sh_ref[0]).astype(BF16)


def _inproj_kernel(x_ref, nw_ref, sc_ref, sh_ref, w_ref, wkt_ref, wa_ref, wat_ref,
                   hy_ref, qvr_ref, gate_ref, kt_ref, a_ref, at_ref, *, tm):
    h = _modulated_norm(x_ref, nw_ref, sc_ref, sh_ref)
    col = 0
    for o_ref in (hy_ref, qvr_ref, gate_ref):
        width = o_ref.shape[-1]
        done = 0
        while done < width:
            step = min(512, width - done)
            o_ref[0, :, done:done + step] = _dot(h, w_ref[:, col:col + step]).astype(o_ref.dtype)
            done += step
            col += step
    nt = (((1,), (1,)), ((), ()))
    kt = lax.dot_general(wkt_ref[...], h, nt, preferred_element_type=F32).astype(BF16)
    for p in range(tm // PAIR):
        kt_ref[0, p] = kt[:, p * PAIR:(p + 1) * PAIR]
    a_ref[0] = _dot(h, wa_ref[...])
    at_ref[0] = lax.dot_general(wat_ref[...], h, nt, preferred_element_type=F32)


def inproj_call(x, norm_w, sc, sh, w_main, w_kt, w_a, w_at, *, tm=1024):
    B, L, D = x.shape
    n_main = w_main.shape[1]
    const = lambda b, i: (0, 0)
    tok = lambda b, i: (b, i, 0)
    bvec = lambda b, i: (b, 0, 0)
    return pl.pallas_call(
        functools.partial(_inproj_kernel, tm=tm),
        grid=(B, L // tm),
        in_specs=[pl.BlockSpec((1, tm, D), tok),
                  pl.BlockSpec((1, D), const),
                  pl.BlockSpec((1, 1, D), bvec),
                  pl.BlockSpec((1, 1, D), bvec),
                  pl.BlockSpec((D, n_main), const, pipeline_mode=pl.Buffered(1)),
                  pl.BlockSpec((GLA_QK, D), const),
                  pl.BlockSpec((D, LANES), const),
                  pl.BlockSpec((LANES, D), const)],
        out_specs=(pl.BlockSpec((1, tm, HY_COLS), tok),
                   pl.BlockSpec((1, tm, GLA_QK + 2 * GLA_VW), tok),
                   pl.BlockSpec((1, tm, 2 * D), tok),
                   pl.BlockSpec((1, tm // PAIR, GLA_QK, PAIR), lambda b, i: (b, i, 0, 0)),
                   pl.BlockSpec((1, tm, LANES), tok),
                   pl.BlockSpec((1, LANES, tm), lambda b, i: (b, 0, i))),
        out_shape=(jax.ShapeDtypeStruct((B, L, HY_COLS), BF16),
                   jax.ShapeDtypeStruct((B, L, GLA_QK + 2 * GLA_VW), BF16),
                   jax.ShapeDtypeStruct((B, L, 2 * D), BF16),
                   jax.ShapeDtypeStruct((B, L // PAIR, GLA_QK, PAIR), BF16),
                   jax.ShapeDtypeStruct((B, L, LANES), F32),
                   jax.ShapeDtypeStruct((B, LANES, L), F32)),
        compiler_params=_cp(("arbitrary", "arbitrary")),
        name="inproj",
    )(x, norm_w.reshape(1, D), sc, sh, w_main, w_kt, w_a, w_at)


def _short_conv_rows(u_ref, w_ref, t0, rows, col0, ncol, L):
    halo = 16
    lo = max(t0 - halo, 0)
    hi = min(t0 + rows + halo, L)
    n = hi - lo
    blk = u_ref[0, lo:hi, col0:col0 + ncol].astype(F32)
    ridx = lax.broadcasted_iota(jnp.int32, (n, 1), 0)
    prev = pltpu.roll(blk, 1, axis=0)
    nxt = pltpu.roll(blk, n - 1, axis=0)
    if lo == 0:
        prev = jnp.where(ridx == 0, 0.0, prev)
    if hi == L:
        nxt = jnp.where(ridx == n - 1, 0.0, nxt)
    w = w_ref[:, col0:col0 + ncol]
    out = prev * w[0:1] + blk * w[1:2] + nxt * w[2:3]
    return out[t0 - lo:t0 - lo + rows]


def _hyena_kernel(u_ref, cw_ref, bias_ref, fwd_ref, inv_ref, a_ref, b_ref, gn_ref, o_ref,
                  zb_ref, x0_ref, zp_ref, uv_ref, y_ref, *, L, M, rb, fb):
    W = HY_WIDTH
    P = L // M

    def sign_rows(n):
        t = lax.broadcasted_iota(jnp.int32, (n, 1), 0)
        return jnp.where((t & 1) == 0, 1.0, -1.0)

    znyq = [jnp.zeros((1, W), F32) for _ in range(P)]
    for t0 in range(0, L, rb):
        x1 = _short_conv_rows(u_ref, cw_ref, t0, rb, W, W, L)
        v = _short_conv_rows(u_ref, cw_ref, t0, rb, 2 * W, W, L)
        z = x1 * v
        zb_ref[t0:t0 + rb, :] = z.astype(BF16)
        x0_ref[t0:t0 + rb, :] = _short_conv_rows(u_ref, cw_ref, t0, rb, 0, W, L).astype(BF16)
        znyq[t0 // M] = znyq[t0 // M] + jnp.sum(sign_rows(rb) * z, axis=0, keepdims=True)

    for j in range(P):
        zp_ref[j] = _dot(fwd_ref[...], zb_ref[j * M:(j + 1) * M, :])

    bias = bias_ref[...]
    for i in range(P):
        for f0 in range(0, M, fb):
            u = jnp.zeros((fb, W), F32)
            v = jnp.zeros((fb, W), F32)
            for j in range(P):
                d = i - j + P - 1
                zre = zp_ref[j, f0:f0 + fb, :]
                p = zp_ref[j, M + f0:M + f0 + fb, :]
                a = a_ref[d, f0:f0 + fb, :]
                b = b_ref[d, f0:f0 + fb, :]
                u = u + zre * a + p * b
                v = v + p * a - zre * b
            uv_ref[f0:f0 + fb, :] = u.astype(BF16)
            uv_ref[M + f0:M + f0 + fb, :] = v.astype(BF16)
        y_ref[...] = _dot(inv_ref[...], uv_ref[...])
        ynyq = jnp.zeros((1, W), F32)
        for j in range(P):
            ynyq = ynyq + znyq[j] * gn_ref[i - j + P - 1, 0:1, :]
        for r0 in range(0, M, rb):
            rows = slice(i * M + r0, i * M + r0 + rb)
            z = zb_ref[rows, :].astype(F32)
            y = y_ref[r0:r0 + rb, :] + sign_rows(rb) * ynyq + bias * z
            o_ref[0, rows, :] = (x0_ref[rows, :].astype(F32) * y).astype(o_ref.dtype)


def hyena_call(u_hy, conv_w, hy_bias, fwd_tab, inv_tab, a_co, b_co, g_nyq, *, rb=256, fb=16):
    B, L, _ = u_hy.shape
    W = HY_WIDTH
    M = HY_BLOCK
    n_seg = a_co.shape[0]
    const2 = lambda b: (0, 0)
    const3 = lambda b: (0, 0, 0)
    return pl.pallas_call(
        functools.partial(_hyena_kernel, L=L, M=M, rb=rb, fb=fb),
        grid=(B,),
        in_specs=[pl.BlockSpec((1, L, HY_COLS), lambda b: (b, 0, 0)),
                  pl.BlockSpec((3, HY_COLS), const2),
                  pl.BlockSpec((1, W), const2),
                  pl.BlockSpec((2 * M, M), const2),
                  pl.BlockSpec((M, 2 * M), const2),
                  pl.BlockSpec((n_seg, M, W), const3, pipeline_mode=pl.Buffered(1)),
                  pl.BlockSpec((n_seg, M, W), const3, pipeline_mode=pl.Buffered(1)),
                  pl.BlockSpec((n_seg, 8, W), const3)],
        out_specs=pl.BlockSpec((1, L, W), lambda b: (b, 0, 0)),
        out_shape=jax.ShapeDtypeStruct((B, L, W), BF16),
        scratch_shapes=[pltpu.VMEM((L, W), BF16),
                        pltpu.VMEM((L, W), BF16),
                        pltpu.VMEM((L // M, 2 * M, W), F32),
                        pltpu.VMEM((2 * M, W), BF16),
                        pltpu.VMEM((M, W), F32)],
        compiler_params=_cp(("arbitrary",)),
        name="hyena",
    )(u_hy, conv_w, hy_bias.reshape(1, W), fwd_tab, inv_tab, a_co, b_co, g_nyq)


def _split3(x):
    x1 = x.astype(BF16)
    r = x - x1.astype(F32)
    x2 = r.astype(BF16)
    x3 = (r - x2.astype(F32)).astype(BF16)
    return x1, x2, x3


def _tri_left(m, x):
    n = x.shape[1]
    r = _dot(m, jnp.concatenate(_split3(x), axis=1))
    return r[:, 0:n] + r[:, n:2 * n] + r[:, 2 * n:3 * n]


def _tri_right(x, m):
    n = x.shape[0]
    r = _dot(jnp.concatenate(_split3(x), axis=0), m)
    return r[0:n] + r[n:2 * n] + r[2 * n:3 * n]


def _tri_mats(n, block):
    r = np.arange(n)
    same = (r[:, None] // block) == (r[None, :] // block)
    lower = same & (r[:, None] >= r[None, :])
    upper = same & (r[:, None] <= r[None, :])
    return jnp.asarray(lower, BF16), jnp.asarray(upper, BF16)


def _log_decay_tok(a, wup_ref, bias_ref):
    return _log_sigmoid(_dot_3(a, wup_ref[...]) + bias_ref[...]) * (1.0 / GLA_TAU)


def _log_decay_feat(at, wupt_ref, biast_ref):
    return _log_sigmoid(_dot_3(wupt_ref[...], at) + biast_ref[...]) * (1.0 / GLA_TAU)


def _ctx_kernel(x_ref, nw_ref, sc_ref, sh_ref, wkt_ref, wv_ref, wat_ref,
                wuptf_ref, biastf_ref, wuptb_ref, biastb_ref, lo_ref, up_ref, sf_ref, sb_ref):
    h = _modulated_norm(x_ref, nw_ref, sc_ref, sh_ref)
    nt = (((1,), (1,)), ((), ()))
    kt = lax.dot_general(wkt_ref[...], h, nt, preferred_element_type=F32)
    v = _dot(h, wv_ref[...]).astype(BF16)
    at = lax.dot_general(wat_ref[...], h, nt, preferred_element_type=F32)
    la_f = _log_decay_feat(at, wuptf_ref, biastf_ref)
    la_b = _log_decay_feat(at, wuptb_ref, biastb_ref)
    lo = lo_ref[...]
    up = up_ref[...]
    w_f = jnp.exp(_tri_right(la_f, lo) - la_f)
    w_b = jnp.exp(_tri_right(la_b, up) - la_b)
    kf = (kt * w_f).astype(BF16)
    kb = (kt * w_b).astype(BF16)
    for hd in range(GLA_HEADS):
        r0, c0 = hd * GLA_DK, hd * GLA_DV
        sf_ref[0, hd] = _dot(kf[r0:r0 + GLA_DK], v[:, c0:c0 + GLA_DV])
        sb_ref[0, hd] = _dot(kb[r0:r0 + GLA_DK], v[:, c0:c0 + GLA_DV])


def ctx_call(ctx, norm_w, csc, csh, w_kt, w_v, w_at, wupt_f, biast_f, wupt_b, biast_b):
    B, Lc, D = ctx.shape
    lo, up = _tri_mats(Lc, Lc)
    const2 = lambda b: (0, 0)
    const3 = lambda b: (0, 0, 0)
    st = jax.ShapeDtypeStruct((B, GLA_HEADS, GLA_DK, GLA_DV), F32)
    st_spec = pl.BlockSpec((1, GLA_HEADS, GLA_DK, GLA_DV), lambda b: (b, 0, 0, 0))
    return pl.pallas_call(
        _ctx_kernel,
        grid=(B,),
        in_specs=[pl.BlockSpec((1, Lc, D), lambda b: (b, 0, 0)),
                  pl.BlockSpec((1, D), const2),
                  pl.BlockSpec((1, 1, D), const3),
                  pl.BlockSpec((1, 1, D), const3),
                  pl.BlockSpec((GLA_QK, D), const2),
                  pl.BlockSpec((D, GLA_VW), const2),
                  pl.BlockSpec((LANES, D), const2),
                  pl.BlockSpec((GLA_QK, LANES), const2),
                  pl.BlockSpec((GLA_QK, 1), const2),
                  pl.BlockSpec((GLA_QK, LANES), const2),
                  pl.BlockSpec((GLA_QK, 1), const2),
                  pl.BlockSpec((Lc, Lc), const2),
                  pl.BlockSpec((Lc, Lc), const2)],
        out_specs=(st_spec, st_spec),
        out_shape=(st, st),
        compiler_params=_cp(("arbitrary",)),
        name="ctx",
    )(ctx, norm_w.reshape(1, D), csc, csh, w_kt, w_v, w_at, wupt_f, biast_f, wupt_b, biast_b, lo, up)


def _gla_kernel(qvr_ref, kt_ref, a_ref, at_ref, sf_ref, sb_ref,
                wupf_ref, biasf_ref, wupb_ref, biasb_ref,
                wuptf_ref, biastf_ref, wuptb_ref, biastb_ref,
                lo_ref, up_ref, nw_ref, o_ref,
                laf_ref, lab_ref, latf_ref, latb_ref, accf_ref, accb_ref, stf_ref, stb_ref, *, L, rb):
    n_pairs = L // PAIR
    C = GLA_CHUNK
    H, DK, DV = GLA_HEADS, GLA_DK, GLA_DV
    scale = DK ** -0.5

    for t0 in range(0, L, rb):
        a = a_ref[0, t0:t0 + rb, :]
        laf_ref[t0:t0 + rb, :] = _log_decay_tok(a, wupf_ref, biasf_ref)
        lab_ref[t0:t0 + rb, :] = _log_decay_tok(a, wupb_ref, biasb_ref)
    for p in range(n_pairs):
        at = at_ref[0, :, p * PAIR:(p + 1) * PAIR]
        latf_ref[p] = _log_decay_feat(at, wuptf_ref, biastf_ref)
        latb_ref[p] = _log_decay_feat(at, wuptb_ref, biastb_ref)

    lo = lo_ref[...]
    up = up_ref[...]
    rr = lax.broadcasted_iota(jnp.int32, (C, C), 0)
    cc = lax.broadcasted_iota(jnp.int32, (C, C), 1)

    def half_pair(p, la_ref, lat_ref, st_ref, acc_ref, forward):
        m_tok = lo if forward else up
        m_feat = up if forward else lo
        keep = (rr >= cc) if forward else (rr <= cc)
        last = C - 1 if forward else 0
        t0 = pl.multiple_of(p * PAIR, PAIR)
        b_tok = _tri_left(m_tok, la_ref[pl.ds(t0, PAIR), :])
        b_feat = _tri_right(lat_ref[p], m_feat)
        q = qvr_ref[0, pl.ds(t0, PAIR), 0:GLA_QK].astype(F32)
        qe = (q * scale * jnp.exp(b_tok)).astype(BF16)
        kt = kt_ref[0, p].astype(F32)
        ket = (kt * jnp.exp(-b_feat)).astype(BF16)
        v = qvr_ref[0, pl.ds(t0, PAIR), GLA_QK:GLA_QK + GLA_VW]
        for half in ((0, 1) if forward else (1, 0)):
            r0 = half * C
            b_last = b_feat[:, r0 + last:r0 + last + 1]
            kdt = (kt[:, r0:r0 + C] * jnp.exp(b_last - b_feat[:, r0:r0 + C])).astype(BF16)
            g = jnp.exp(b_last)
            for hd in range(H):
                k0, v0 = hd * DK, hd * DV
                qe_h = qe[r0:r0 + C, k0:k0 + DK]
                v_h = v[r0:r0 + C, v0:v0 + DV]
                s_prev = st_ref[hd]
                sc = _dot(qe_h, ket[k0:k0 + DK, r0:r0 + C])
                sc = jnp.where(keep, sc, 0.0).astype(BF16)
                acc_ref[pl.ds(t0 + r0, C), v0:v0 + DV] = _dot(sc, v_h) + _dot(qe_h, s_prev.astype(BF16))
                st_ref[hd] = g[k0:k0 + DK] * s_prev + _dot(kdt[k0:k0 + DK], v_h)

    for hd in range(H):
        stf_ref[hd] = sf_ref[0, hd]
        stb_ref[hd] = sb_ref[0, hd]

    def pair_body(i, carry):
        half_pair(i, laf_ref, latf_ref, stf_ref, accf_ref, True)
        half_pair(n_pairs - 1 - i, lab_ref, latb_ref, stb_ref, accb_ref, False)
        return carry

    lax.fori_loop(0, n_pairs, pair_body, 0, unroll=2)

    for t0 in range(0, L, rb):
        r = qvr_ref[0, t0:t0 + rb, GLA_QK + GLA_VW:GLA_QK + 2 * GLA_VW].astype(F32)
        for hd in range(H):
            v0 = hd * DV
            o = _rms_scale(accf_ref[t0:t0 + rb, v0:v0 + DV] + accb_ref[t0:t0 + rb, v0:v0 + DV])
            y = o * nw_ref[:, v0:v0 + DV] * _silu(r[:, v0:v0 + DV])
            o_ref[0, t0:t0 + rb, v0:v0 + DV] = y.astype(o_ref.dtype)


def gla_call(qvr, kt, a, at, s_f, s_b, wup_f, bias_f, wup_b, bias_b,
             wupt_f, biast_f, wupt_b, biast_b, norm_w, *, rb=256):
    B, L, _ = qvr.shape
    lo, up = _tri_mats(PAIR, GLA_CHUNK)
    const2 = lambda b: (0, 0)
    st_spec = pl.BlockSpec((1, GLA_HEADS, GLA_DK, GLA_DV), lambda b: (b, 0, 0, 0))
    return pl.pallas_call(
        functools.partial(_gla_kernel, L=L, rb=rb),
        grid=(B,),
        in_specs=[pl.BlockSpec((1, L, GLA_QK + 2 * GLA_VW), lambda b: (b, 0, 0)),
                  pl.BlockSpec((1, L // PAIR, GLA_QK, PAIR), lambda b: (b, 0, 0, 0)),
                  pl.BlockSpec((1, L, LANES), lambda b: (b, 0, 0)),
                  pl.BlockSpec((1, LANES, L), lambda b: (b, 0, 0)),
                  st_spec, st_spec,
                  pl.BlockSpec((LANES, GLA_QK), const2),
                  pl.BlockSpec((1, GLA_QK), const2),
                  pl.BlockSpec((LANES, GLA_QK), const2),
                  pl.BlockSpec((1, GLA_QK), const2),
                  pl.BlockSpec((GLA_QK, LANES), const2),
                  pl.BlockSpec((GLA_QK, 1), const2),
                  pl.BlockSpec((GLA_QK, LANES), const2),
                  pl.BlockSpec((GLA_QK, 1), const2),
                  pl.BlockSpec((PAIR, PAIR), const2),
                  pl.BlockSpec((PAIR, PAIR), const2),
                  pl.BlockSpec((1, GLA_VW), const2)],
        out_specs=pl.BlockSpec((1, L, GLA_VW), lambda b: (b, 0, 0)),
        out_shape=jax.ShapeDtypeStruct((B, L, GLA_VW), BF16),
        scratch_shapes=[pltpu.VMEM((L, GLA_QK), F32),
                        pltpu.VMEM((L, GLA_QK), F32),
                        pltpu.VMEM((L // PAIR, GLA_QK, PAIR), F32),
                        pltpu.VMEM((L // PAIR, GLA_QK, PAIR), F32),
                        pltpu.VMEM((L, GLA_VW), F32),
                        pltpu.VMEM((L, GLA_VW), F32),
                        pltpu.VMEM((GLA_HEADS, GLA_DK, GLA_DV), F32),
                        pltpu.VMEM((GLA_HEADS, GLA_DK, GLA_DV), F32)],
        compiler_params=_cp(("arbitrary",)),
        name="gla",
    )(qvr, kt, a, at, s_f, s_b, wup_f, bias_f, wup_b, bias_b,
      wupt_f, biast_f, wupt_b, biast_b, lo, up, norm_w.reshape(1, GLA_VW))


def _merge_kernel(x_ref, yhy_ref, ygla_ref, gate_ref, g1_ref, whp_ref, wgp_ref, wo_ref,
                  nw_ref, sc_ref, sh_ref, xn_ref, h2_ref):
    D = D_MODEL
    g_hy = _sigmoid(gate_ref[0, :, 0:D].astype(F32))
    g_gla = _sigmoid(gate_ref[0, :, D:2 * D].astype(F32))
    merged = g_hy * _dot(yhy_ref[0], whp_ref[...]) + g_gla * _dot(ygla_ref[0], wgp_ref[...])
    mix = _dot(merged.astype(BF16), wo_ref[...])
    xn = x_ref[0] + g1_ref[0] * mix
    xn_ref[0] = xn
    y = _rms_scale(xn) * nw_ref[...]
    h2_ref[0] = (y * (1.0 + sc_ref[0]) + sh_ref[0]).astype(h2_ref.dtype)


def merge_call(x, y_hy, y_gla, gates, g1, w_hy_proj, w_gla_proj, w_out, norm2_w, sc2, sh2, *, tm=1024):
    B, L, D = x.shape
    tok = lambda b, i: (b, i, 0)
    bvec = lambda b, i: (b, 0, 0)
    const = lambda b, i: (0, 0)
    return pl.pallas_call(
        _merge_kernel,
        grid=(B, L // tm),
        in_specs=[pl.BlockSpec((1, tm, D), tok),
                  pl.BlockSpec((1, tm, HY_WIDTH), tok),
                  pl.BlockSpec((1, tm, GLA_VW), tok),
                  pl.BlockSpec((1, tm, 2 * D), tok),
                  pl.BlockSpec((1, 1, D), bvec),
                  pl.BlockSpec((HY_WIDTH, D), const),
                  pl.BlockSpec((GLA_VW, D), const),
                  pl.BlockSpec((D, D), const),
                  pl.BlockSpec((1, D), const),
                  pl.BlockSpec((1, 1, D), bvec),
                  pl.BlockSpec((1, 1, D), bvec)],
        out_specs=(pl.BlockSpec((1, tm, D), tok), pl.BlockSpec((1, tm, D), tok)),
        out_shape=(jax.ShapeDtypeStruct((B, L, D), F32), jax.ShapeDtypeStruct((B, L, D), BF16)),
        compiler_params=_cp(("arbitrary", "arbitrary")),
        name="merge",
    )(x, y_hy, y_gla, gates, g1, w_hy_proj, w_gla_proj, w_out, norm2_w.reshape(1, D), sc2, sh2)


FFN_PAD = GRID_W


def _ffn_act_kernel(h_ref, wg0_ref, wgn_ref, wup_ref, dw_ref, dwb_ref, o_ref, g0_ref, g1_ref, s_ref, *, L):
    j = pl.program_id(1)
    hc = wup_ref.shape[1]
    mb = 512
    W = GRID_W
    sub = lax.broadcasted_iota(jnp.int32, (8, hc), 0)

    def project(gate_ref, w_ref, t0):
        gate_ref[FFN_PAD + t0:FFN_PAD + t0 + mb, :] = _dot(h_ref[0, t0:t0 + mb, :], w_ref[...])

    @pl.when(j == 0)
    def _():
        zeros = jnp.zeros((FFN_PAD, hc), F32)
        for ref in (g0_ref, g1_ref):
            ref[0:FFN_PAD, :] = zeros
            ref[FFN_PAD + L:FFN_PAD + L + FFN_PAD, :] = zeros
        for t0 in range(0, L, mb):
            project(g0_ref, wg0_ref, t0)

    def step(gate_ref, nxt_ref):
        dw = dw_ref[...]
        for t0 in range(0, L, mb):
            project(nxt_ref, wgn_ref, t0)
            s_ref[t0:t0 + mb, :] = _dot(h_ref[0, t0:t0 + mb, :], wup_ref[...])
            for r0 in range(t0, t0 + mb, W):
                above = gate_ref[FFN_PAD + r0 - W:FFN_PAD + r0, :]
                centre = gate_ref[FFN_PAD + r0:FFN_PAD + r0 + W, :]
                below = gate_ref[FFN_PAD + r0 + W:FFN_PAD + r0 + 2 * W, :]
                col = [above * dw[dx:dx + 1] + centre * dw[3 + dx:4 + dx] + below * dw[6 + dx:7 + dx]
                       for dx in range(3)]
                left = pltpu.roll(col[0], 1, axis=0)
                right = pltpu.roll(col[2], W - 1, axis=0)
                left = jnp.concatenate([jnp.where(sub == 0, 0.0, left[0:8]), left[8:]], axis=0)
                right = jnp.concatenate([right[:W - 8], jnp.where(sub == 7, 0.0, right[W - 8:])], axis=0)
                conv = col[1] + left + right + dwb_ref[...]
                o_ref[0, r0:r0 + W, :] = (_silu(conv) * s_ref[r0:r0 + W, :]).astype(o_ref.dtype)

    parity = lax.rem(j, 2)
    pl.when(parity == 0)(lambda: step(g0_ref, g1_ref))
    pl.when(parity == 1)(lambda: step(g1_ref, g0_ref))


def ffn_act_call(h2, w_gate, w_up, dw, dw_bias, *, hc=256):
    B, L, D = h2.shape
    F = w_up.shape[1]
    n_chunks = F // hc
    return pl.pallas_call(
        functools.partial(_ffn_act_kernel, L=L),
        grid=(B, n_chunks),
        in_specs=[pl.BlockSpec((1, L, D), lambda b, j: (b, 0, 0)),
                  pl.BlockSpec((D, hc), lambda b, j: (0, 0)),
                  pl.BlockSpec((D, hc), lambda b, j: (0, jnp.minimum(j + 1, n_chunks - 1))),
                  pl.BlockSpec((D, hc), lambda b, j: (0, j)),
                  pl.BlockSpec((9, hc), lambda b, j: (0, j)),
                  pl.BlockSpec((1, hc), lambda b, j: (0, j))],
        out_specs=pl.BlockSpec((1, L, hc), lambda b, j: (b, 0, j)),
        out_shape=jax.ShapeDtypeStruct((B, L, F), BF16),
        scratch_shapes=[pltpu.VMEM((L + 2 * FFN_PAD, hc), F32)] * 2 + [pltpu.VMEM((L, hc), F32)],
        compiler_params=_cp(("arbitrary", "arbitrary")),
        name="ffn_act",
    )(h2, w_gate, w_gate, w_up, dw, dw_bias.reshape(1, F))


def _ffn_out_kernel(act_ref, xn_ref, g2_ref, wout_ref, fw_ref, o_ref):
    xo = xn_ref[0] + g2_ref[0] * _dot(act_ref[0], wout_ref[...])
    o_ref[0] = _rms_scale(xo) * fw_ref[...]


def ffn_out_call(act, x_new, g2, w_out, final_w, *, tm=1024):
    B, L, F = act.shape
    D = w_out.shape[1]
    tok = lambda b, i: (b, i, 0)
    const = lambda b, i: (0, 0)
    return pl.pallas_call(
        _ffn_out_kernel,
        grid=(B, L // tm),
        in_specs=[pl.BlockSpec((1, tm, F), tok),
                  pl.BlockSpec((1, tm, D), tok),
                  pl.BlockSpec((1, 1, D), lambda b, i: (b, 0, 0)),
                  pl.BlockSpec((F, D), const, pipeline_mode=pl.Buffered(1)),
                  pl.BlockSpec((1, D), const)],
        out_specs=pl.BlockSpec((1, tm, D), tok),
        out_shape=jax.ShapeDtypeStruct((B, L, D), F32),
        compiler_params=_cp(("arbitrary", "arbitrary")),
        name="ffn_out",
    )(act, x_new, g2, w_out, final_w.reshape(1, D))


def _pad_rows(w, rows, at=0):
    out = jnp.zeros((rows, w.shape[1]), w.dtype)
    return lax.dynamic_update_slice(out, w, (at, 0))


def kernel(x, c, ctx, c_ctx, w_ada, b_ada, norm1_w, norm2_w, w_in, hy_conv_w, hy_filt_w1, hy_filt_b1,
           hy_filt_w2, hy_filt_b2, hy_filt_w3, hy_filt_freq, hy_decay, hy_bias, gla_a_up_f, gla_a_bias_f,
           gla_a_up_b, gla_a_bias_b, gla_norm_w, w_hy_proj, w_gla_proj, w_out, ffn_w_in, ffn_dw,
           ffn_dw_bias, ffn_w_out, final_norm_w):
    assert w_ada.shape[0] == 1, "single-layer block"
    B, L, D = x.shape
    R = GLA_LOWRANK

    n_rows = -(-(B + 1) // 8) * 8
    c_rows = jnp.zeros((n_rows, D), F32).at[:B].set(c).at[B].set(c_ctx)
    mod = ada_call(c_rows, w_ada[0], b_ada[0])
    sh1, sc1, g1, sh2, sc2, g2 = [m[:B, None, :] for m in jnp.split(mod, 6, axis=-1)]
    csh1, csc1 = mod[B:B + 1, None, 0:D], mod[B:B + 1, None, D:2 * D]

    w = w_in[0]
    o_q = HY_COLS
    o_k = o_q + GLA_QK
    o_v = o_k + GLA_QK
    o_r = o_v + GLA_VW
    o_af = o_r + GLA_VW
    o_ab = o_af + R
    o_g = o_ab + R
    w_main = jnp.concatenate([w[:, :o_k], w[:, o_v:o_af], w[:, o_g:]], axis=1).astype(BF16)
    w_kt = w[:, o_k:o_v].T.astype(BF16)
    w_v = w[:, o_v:o_r].astype(BF16)
    w_a = jnp.pad(w[:, o_af:o_g], ((0, 0), (0, LANES - 2 * R))).astype(BF16)
    w_at = w_a.T
    wup_f = _pad_rows(gla_a_up_f[0], LANES, 0)
    wup_b = _pad_rows(gla_a_up_b[0], LANES, R)
    wupt_f, wupt_b = wup_f.T, wup_b.T
    bias_f, bias_b = gla_a_bias_f[0].reshape(1, GLA_QK), gla_a_bias_b[0].reshape(1, GLA_QK)
    biast_f, biast_b = bias_f.reshape(GLA_QK, 1), bias_b.reshape(GLA_QK, 1)

    cos32, sin32 = dft_tables(HY_BLOCK)
    a_co, b_co, g_nyq = filter_call(L, hy_filt_w1[0], hy_filt_b1[0], hy_filt_w2[0], hy_filt_b2[0],
                                    hy_filt_w3[0], hy_filt_freq[0], hy_decay[0], cos32, sin32)
    fwd_tab = jnp.concatenate([cos32, sin32], axis=0).astype(BF16)
    inv_tab = jnp.concatenate([cos32, sin32], axis=1).astype(BF16)

    s_f, s_b = ctx_call(ctx, norm1_w[0], csc1, csh1, w_kt, w_v, w_at, wupt_f, biast_f, wupt_b, biast_b)

    u_hy, qvr, gates, kt, a, at = inproj_call(x, norm1_w[0], sc1, sh1, w_main, w_kt, w_a, w_at)
    y_hy = hyena_call(u_hy, hy_conv_w[0], hy_bias[0], fwd_tab, inv_tab, a_co, b_co, g_nyq)
    y_gla = gla_call(qvr, kt, a, at, s_f, s_b, wup_f, bias_f, wup_b, bias_b,
                     wupt_f, biast_f, wupt_b, biast_b, gla_norm_w[0])
    x_new, h2 = merge_call(x, y_hy, y_gla, gates, g1, w_hy_proj[0].astype(BF16), w_gla_proj[0].astype(BF16),
                           w_out[0].astype(BF16), norm2_w[0], sc2, sh2)
    w_ffn = ffn_w_in[0]
    act = ffn_act_call(h2, w_ffn[:, FFN_HIDDEN:].astype(BF16), w_ffn[:, :FFN_HIDDEN].astype(BF16),
                       ffn_dw[0].reshape(9, FFN_HIDDEN), ffn_dw_bias[0])
    return ffn_out_call(act, x_new, g2, ffn_w_out[0].astype(BF16), final_norm_w)
```

```python
import functools
import math

import jax
import jax.numpy as jnp
import numpy as np
from jax import lax
from jax.experimental import pallas as pl
from jax.experimental.pallas import tpu as pltpu

F32 = jnp.float32
BF16 = jnp.bfloat16
HIGHEST = lax.Precision.HIGHEST

D_MODEL = 1024
GRID_W = 64
HY_WIDTH = 512
HY_BANDS = 16
HY_EMB = 1 + 2 * HY_BANDS
HY_FILT_HIDDEN = 64
GLA_HEADS = 4
GLA_DK = 64
GLA_DV = 128
GLA_LOWRANK = 16
GLA_TAU = 16.0
GLA_CHUNK = 64
FFN_HIDDEN = 2816
EPS = 1e-6
HY_COLS = 3 * HY_WIDTH
GLA_QK = GLA_HEADS * GLA_DK
GLA_VW = GLA_HEADS * GLA_DV

LANES = 128
PAIR = 2 * GLA_CHUNK
VMEM_LIMIT = 56 * 1024 * 1024


def _cp(sem):
    return pltpu.CompilerParams(dimension_semantics=sem, vmem_limit_bytes=VMEM_LIMIT)


def _dot(a, b):
    return jnp.dot(a, b, preferred_element_type=F32)


def _dot_hi(a, b):
    return jnp.dot(a, b, preferred_element_type=F32, precision=HIGHEST)


def _split2(x):
    x1 = x.astype(BF16)
    return x1, (x - x1.astype(F32)).astype(BF16)


def _dot_3(a, b):
    a1, a2 = _split2(a)
    b1, b2 = _split2(b)
    return _dot(jnp.concatenate([a1, a1, a2], axis=1), jnp.concatenate([b1, b2, b1], axis=0))


def _sigmoid(x):
    return 1.0 / (1.0 + jnp.exp(-x))


def _silu(x):
    return x * _sigmoid(x)


def _log_sigmoid(x):
    return jnp.minimum(x, 0.0) - jnp.log(1.0 + jnp.exp(-jnp.abs(x)))


def _rms_scale(xf):
    return xf * lax.rsqrt(jnp.mean(xf * xf, axis=-1, keepdims=True) + EPS)


def _ada_kernel(c_ref, w_ref, b_ref, o_ref):
    o_ref[...] = _dot_3(_silu(c_ref[...]), w_ref[...]) + b_ref[...]


def ada_call(c_rows, w_ada, b_ada):
    rows, d = c_rows.shape
    n = w_ada.shape[1]
    bn = 512
    return pl.pallas_call(
        _ada_kernel,
        grid=(n // bn,),
        in_specs=[pl.BlockSpec((rows, d), lambda j: (0, 0)),
                  pl.BlockSpec((d, bn), lambda j: (0, j)),
                  pl.BlockSpec((1, bn), lambda j: (0, j))],
        out_specs=pl.BlockSpec((rows, bn), lambda j: (0, j)),
        out_shape=jax.ShapeDtypeStruct((rows, n), F32),
        compiler_params=_cp(("arbitrary",)),
        name="ada",
    )(c_rows, w_ada, b_ada.reshape(1, n))


HY_BLOCK = 512


def _filt_mlp_kernel(feats_ref, t_ref, w1_ref, b1_ref, w2_ref, b2_ref, w3_ref, freq_ref, decay_ref, taps_ref):
    freq = freq_ref[...]
    h = jnp.sin(freq * (_dot_3(feats_ref[...], w1_ref[...]) + b1_ref[...]))
    h = jnp.sin(freq * (_dot_hi(h, w2_ref[...]) + b2_ref[...]))
    k = _dot_hi(h, w3_ref[...]) * jnp.exp(-t_ref[...] * decay_ref[...])
    L = k.shape[0]
    row = lax.broadcasted_iota(jnp.int32, (L, HY_WIDTH), 0)
    k_f = k[:, :HY_WIDTH]
    k_b = jnp.where(row == 0, 0.0, k[:, HY_WIDTH:])
    norm = jnp.sum(jnp.abs(k_f), axis=0, keepdims=True) + jnp.sum(jnp.abs(k_b), axis=0, keepdims=True)
    taps_ref[0:L, :] = k_f / norm
    taps_ref[L:2 * L, :] = k_b / norm


def _filt_blk_kernel(c_ref, s_ref, x_ref, cx_ref, sx_ref, aux_ref, *, M):
    x = x_ref[...]
    cx_ref[0] = _dot_3(c_ref[...], x)
    sx_ref[0] = _dot_3(s_ref[...], x)
    row = lax.broadcasted_iota(jnp.int32, (M, HY_WIDTH), 0)
    alt = jnp.sum(jnp.where((row & 1) == 0, x, -x), axis=0, keepdims=True)
    sub = lax.broadcasted_iota(jnp.int32, (8, HY_WIDTH), 0)
    aux_ref[0] = jnp.where(sub == 0, x[0:8], jnp.where(sub == 1, alt, 0.0))


def _filt_seg_kernel(cx1_ref, sx1_ref, aux1_ref, cx2_ref, sx2_ref, aux2_ref, a_ref, b_ref, gn_ref, *, M, P):
    d = pl.program_id(0) - (P - 1)
    f = lax.broadcasted_iota(jnp.int32, (M, 1), 0)
    sf = jnp.where((f & 1) == 0, 1.0, -1.0)
    s = jnp.where(d == 0, 1.0, sf)
    sg1 = jnp.where(d >= 0, -1.0, 1.0)
    sg2 = jnp.where(d >= 1, -1.0, 1.0)
    w = jnp.where(f == 0, 0.5 / M, 1.0 / M)
    x2_0 = aux2_ref[0, 0:1, :]
    a_ref[0] = w * (cx1_ref[0] + s * (cx2_ref[0] - x2_0))
    b_ref[0] = w * (sg1 * sx1_ref[0] + (sg2 * s) * sx2_ref[0])
    gn = (aux1_ref[0, 1:2, :] + aux2_ref[0, 1:2, :] - x2_0) * (0.5 / M)
    gn_ref[0] = jnp.broadcast_to(gn, gn_ref.shape[1:])


def filter_call(L, hy_filt_w1, hy_filt_b1, hy_filt_w2, hy_filt_b2, hy_filt_w3, hy_filt_freq, hy_decay,
                cos32, sin32):
    M = HY_BLOCK
    P = L // M
    t = jnp.linspace(0.0, 1.0, L, dtype=F32)[:, None]
    w = (2.0 * np.pi / L) * jnp.arange(L, dtype=F32)[:, None]
    f = jnp.linspace(1e-4, HY_BANDS - 1, HY_BANDS, dtype=F32)[None, :]
    feats = jnp.concatenate([t, jnp.cos(f * w), -jnp.sin(f * w)], axis=-1)
    feats = jnp.pad(feats, ((0, 0), (0, LANES - HY_EMB)))
    w1 = jnp.pad(hy_filt_w1, ((0, LANES - HY_EMB), (0, 0)))
    hid = HY_FILT_HIDDEN
    taps = pl.pallas_call(
        _filt_mlp_kernel,
        out_shape=jax.ShapeDtypeStruct((2 * L, HY_WIDTH), F32),
        compiler_params=pltpu.CompilerParams(vmem_limit_bytes=VMEM_LIMIT),
        name="filt_mlp",
    )(feats, t, w1, hy_filt_b1.reshape(1, hid), hy_filt_w2, hy_filt_b2.reshape(1, hid), hy_filt_w3,
      hy_filt_freq.reshape(1, hid), hy_decay.reshape(1, 2 * HY_WIDTH))

    blk = lambda n: jax.ShapeDtypeStruct((n, M, HY_WIDTH), F32)
    aux = lambda n: jax.ShapeDtypeStruct((n, 8, HY_WIDTH), F32)
    blk_spec = lambda imap: pl.BlockSpec((1, M, HY_WIDTH), imap)
    aux_spec = lambda imap: pl.BlockSpec((1, 8, HY_WIDTH), imap)
    own = lambda g: (g, 0, 0)
    cx, sx, ax = pl.pallas_call(
        functools.partial(_filt_blk_kernel, M=M),
        grid=(2 * P,),
        in_specs=[pl.BlockSpec((M, M), lambda g: (0, 0)),
                  pl.BlockSpec((M, M), lambda g: (0, 0)),
                  pl.BlockSpec((M, HY_WIDTH), lambda g: (g, 0))],
        out_specs=(blk_spec(own), blk_spec(own), aux_spec(own)),
        out_shape=(blk(2 * P), blk(2 * P), aux(2 * P)),
        compiler_params=_cp(("arbitrary",)),
        name="filt_blk",
    )(cos32, sin32, taps)

    def first(g):
        d = g - (P - 1)
        return (jnp.where(d >= 0, d, P - d), 0, 0)

    def second(g):
        d = g - (P - 1)
        return (jnp.where(d >= 1, d - 1, jnp.where(d == 0, P, P - d - 1)), 0, 0)

    n_seg = 2 * P - 1
    a_co, b_co, g_nyq = pl.pallas_call(
        functools.partial(_filt_seg_kernel, M=M, P=P),
        grid=(n_seg,),
        in_specs=[blk_spec(first), blk_spec(first), aux_spec(first),
                  blk_spec(second), blk_spec(second), aux_spec(second)],
        out_specs=(blk_spec(own), blk_spec(own), aux_spec(own)),
        out_shape=(blk(n_seg), blk(n_seg), aux(n_seg)),
        compiler_params=_cp(("arbitrary",)),
        name="filt_seg",
    )(cx, sx, ax, cx, sx, ax)
    return a_co, b_co, g_nyq


def dft_tables(M):
    n_fft = 2 * M
    idx = (jnp.arange(M, dtype=jnp.int32)[:, None] * jnp.arange(M, dtype=jnp.int32)[None, :]) % n_fft
    ang = idx.astype(F32) * (2.0 * np.pi / n_fft)
    return jnp.cos(ang), jnp.sin(ang)


def _modulated_norm(x_ref, nw_ref, sc_ref, sh_ref):
    y = _rms_scale(x_ref[0].astype(F32)) * nw_ref[...]
    return (y * (1.0 + sc_ref[0]) + sh_ref[0]).astype(BF16)


def _inproj_kernel(x_ref, nw_ref, sc_ref, sh_ref, w_ref, wkt_ref, wa_ref, wat_ref,
                   hy_ref, qvr_ref, kt_ref, a_ref, at_ref, *, tm):
    h = _modulated_norm(x_ref, nw_ref, sc_ref, sh_ref)
    col = 0
    for o_ref in (hy_ref, qvr_ref):
        width = o_ref.shape[-1]
        done = 0
        while done < width:
            step = min(512, width - done)
            o_ref[0, :, done:done + step] = _dot(h, w_ref[:, col:col + step]).astype(o_ref.dtype)
            done += step
            col += step
    nt = (((1,), (1,)), ((), ()))
    kt = lax.dot_general(wkt_ref[...], h, nt, preferred_element_type=F32).astype(BF16)
    for p in range(tm // PAIR):
        kt_ref[0, p] = kt[:, p * PAIR:(p + 1) * PAIR]
    a_ref[0] = _dot(h, wa_ref[...])
    at_ref[0] = lax.dot_general(wat_ref[...], h, nt, preferred_element_type=F32)


def inproj_call(x, norm_w, sc, sh, w_main, w_kt, w_a, w_at, *, tm=1024):
    B, L, D = x.shape
    n_main = w_main.shape[1]
    const = lambda b, i: (0, 0)
    tok = lambda b, i: (b, i, 0)
    bvec = lambda b, i: (b, 0, 0)
    return pl.pallas_call(
        functools.partial(_inproj_kernel, tm=tm),
        grid=(B, L // tm),
        in_specs=[pl.BlockSpec((1, tm, D), tok),
                  pl.BlockSpec((1, D), const),
                  pl.BlockSpec((1, 1, D), bvec),
                  pl.BlockSpec((1, 1, D), bvec),
                  pl.BlockSpec((D, n_main), const, pipeline_mode=pl.Buffered(1)),
                  pl.BlockSpec((GLA_QK, D), const),
                  pl.BlockSpec((D, LANES), const),
                  pl.BlockSpec((LANES, D), const)],
        out_specs=(pl.BlockSpec((1, tm, HY_COLS), tok),
                   pl.BlockSpec((1, tm, GLA_QK + 2 * GLA_VW), tok),
                   pl.BlockSpec((1, tm // PAIR, GLA_QK, PAIR), lambda b, i: (b, i, 0, 0)),
                   pl.BlockSpec((1, tm, LANES), tok),
                   pl.BlockSpec((1, LANES, tm), lambda b, i: (b, 0, i))),
        out_shape=(jax.ShapeDtypeStruct((B, L, HY_COLS), BF16),
                   jax.ShapeDtypeStruct((B, L, GLA_QK + 2 * GLA_VW), BF16),
                   jax.ShapeDtypeStruct((B, L // PAIR, GLA_QK, PAIR), BF16),
                   jax.ShapeDtypeStruct((B, L, LANES), F32),
                   jax.ShapeDtypeStruct((B, LANES, L), F32)),
        compiler_params=_cp(("arbitrary", "arbitrary")),
        name="inproj",
    )(x, norm_w.reshape(1, D), sc, sh, w_main, w_kt, w_a, w_at)


def _short_conv_rows(u_ref, w_ref, t0, rows, col0, ncol, L):
    halo = 16
    lo = max(t0 - halo, 0)
    hi = min(t0 + rows + halo, L)
    n = hi - lo
    blk = u_ref[0, lo:hi, col0:col0 + ncol].astype(F32)
    ridx = lax.broadcasted_iota(jnp.int32, (n, 1), 0)
    prev = pltpu.roll(blk, 1, axis=0)
    nxt = pltpu.roll(blk, n - 1, axis=0)
    if lo == 0:
        prev = jnp.where(ridx == 0, 0.0, prev)
    if hi == L:
        nxt = jnp.where(ridx == n - 1, 0.0, nxt)
    w = w_ref[:, col0:col0 + ncol]
    out = prev * w[0:1] + blk * w[1:2] + nxt * w[2:3]
    return out[t0 - lo:t0 - lo + rows]


def _hyena_kernel(u_ref, cw_ref, bias_ref, fwd_ref, inv_ref, a_ref, b_ref, gn_ref, o_ref,
                  zb_ref, x0_ref, zp_ref, uv_ref, y_ref, *, L, M, rb, fb):
    W = HY_WIDTH
    P = L // M

    def sign_rows(n):
        t = lax.broadcasted_iota(jnp.int32, (n, 1), 0)
        return jnp.where((t & 1) == 0, 1.0, -1.0)

    znyq = [jnp.zeros((1, W), F32) for _ in range(P)]
    for t0 in range(0, L, rb):
        x1 = _short_conv_rows(u_ref, cw_ref, t0, rb, W, W, L)
        v = _short_conv_rows(u_ref, cw_ref, t0, rb, 2 * W, W, L)
        z = x1 * v
        zb_ref[t0:t0 + rb, :] = z.astype(BF16)
        x0_ref[t0:t0 + rb, :] = _short_conv_rows(u_ref, cw_ref, t0, rb, 0, W, L).astype(BF16)
        znyq[t0 // M] = znyq[t0 // M] + jnp.sum(sign_rows(rb) * z, axis=0, keepdims=True)

    for j in range(P):
        zp_ref[j] = _dot(fwd_ref[...], zb_ref[j * M:(j + 1) * M, :])

    bias = bias_ref[...]
    for i in range(P):
        for f0 in range(0, M, fb):
            u = jnp.zeros((fb, W), F32)
            v = jnp.zeros((fb, W), F32)
            for j in range(P):
                d = i - j + P - 1
                zre = zp_ref[j, f0:f0 + fb, :]
                p = zp_ref[j, M + f0:M + f0 + fb, :]
                a = a_ref[d, f0:f0 + fb, :]
                b = b_ref[d, f0:f0 + fb, :]
                u = u + zre * a + p * b
                v = v + p * a - zre * b
            uv_ref[f0:f0 + fb, :] = u.astype(BF16)
            uv_ref[M + f0:M + f0 + fb, :] = v.astype(BF16)
        y_ref[...] = _dot(inv_ref[...], uv_ref[...])
        ynyq = jnp.zeros((1, W), F32)
        for j in range(P):
            ynyq = ynyq + znyq[j] * gn_ref[i - j + P - 1, 0:1, :]
        for r0 in range(0, M, rb):
            rows = slice(i * M + r0, i * M + r0 + rb)
            z = zb_ref[rows, :].astype(F32)
            y = y_ref[r0:r0 + rb, :] + sign_rows(rb) * ynyq + bias * z
            o_ref[0, rows, :] = (x0_ref[rows, :].astype(F32) * y).astype(o_ref.dtype)


def hyena_call(u_hy, conv_w, hy_bias, fwd_tab, inv_tab, a_co, b_co, g_nyq, *, rb=256, fb=16):
    B, L, _ = u_hy.shape
    W = HY_WIDTH
    M = HY_BLOCK
    n_seg = a_co.shape[0]
    const2 = lambda b: (0, 0)
    const3 = lambda b: (0, 0, 0)
    return pl.pallas_call(
        functools.partial(_hyena_kernel, L=L, M=M, rb=rb, fb=fb),
        grid=(B,),
        in_specs=[pl.BlockSpec((1, L, HY_COLS), lambda b: (b, 0, 0)),
                  pl.BlockSpec((3, HY_COLS), const2),
                  pl.BlockSpec((1, W), const2),
                  pl.BlockSpec((2 * M, M), const2),
                  pl.BlockSpec((M, 2 * M), const2),
                  pl.BlockSpec((n_seg, M, W), const3, pipeline_mode=pl.Buffered(1)),
                  pl.BlockSpec((n_seg, M, W), const3, pipeline_mode=pl.Buffered(1)),
                  pl.BlockSpec((n_seg, 8, W), const3)],
        out_specs=pl.BlockSpec((1, L, W), lambda b: (b, 0, 0)),
        out_shape=jax.ShapeDtypeStruct((B, L, W), BF16),
        scratch_shapes=[pltpu.VMEM((L, W), BF16),
                        pltpu.VMEM((L, W), BF16),
                        pltpu.VMEM((L // M, 2 * M, W), F32),
                        pltpu.VMEM((2 * M, W), BF16),
                        pltpu.VMEM((M, W), F32)],
        compiler_params=_cp(("arbitrary",)),
        name="hyena",
    )(u_hy, conv_w, hy_bias.reshape(1, W), fwd_tab, inv_tab, a_co, b_co, g_nyq)


def _split3(x):
    x1 = x.astype(BF16)
    r = x - x1.astype(F32)
    x2 = r.astype(BF16)
    x3 = (r - x2.astype(F32)).astype(BF16)
    return x1, x2, x3


def _tri_left(m, x):
    n = x.shape[1]
    r = _dot(m, jnp.concatenate(_split3(x), axis=1))
    return r[:, 0:n] + r[:, n:2 * n] + r[:, 2 * n:3 * n]


def _tri_right(x, m):
    n = x.shape[0]
    r = _dot(jnp.concatenate(_split3(x), axis=0), m)
    return r[0:n] + r[n:2 * n] + r[2 * n:3 * n]


def _tri_mats(n, block):
    r = np.arange(n)
    same = (r[:, None] // block) == (r[None, :] // block)
    lower = same & (r[:, None] >= r[None, :])
    upper = same & (r[:, None] <= r[None, :])
    return jnp.asarray(lower, BF16), jnp.asarray(upper, BF16)


def _log_decay_tok(a, wup_ref, bias_ref):
    return _log_sigmoid(_dot_3(a, wup_ref[...]) + bias_ref[...]) * (1.0 / GLA_TAU)


def _log_decay_feat(at, wupt_ref, biast_ref):
    return _log_sigmoid(_dot_3(wupt_ref[...], at) + biast_ref[...]) * (1.0 / GLA_TAU)


def _ctx_kernel(x_ref, nw_ref, sc_ref, sh_ref, wkt_ref, wv_ref, wat_ref,
                wuptf_ref, biastf_ref, wuptb_ref, biastb_ref, lo_ref, up_ref, sf_ref, sb_ref):
    h = _modulated_norm(x_ref, nw_ref, sc_ref, sh_ref)
    nt = (((1,), (1,)), ((), ()))
    kt = lax.dot_general(wkt_ref[...], h, nt, preferred_element_type=F32)
    v = _dot(h, wv_ref[...]).astype(BF16)
    at = lax.dot_general(wat_ref[...], h, nt, preferred_element_type=F32)
    la_f = _log_decay_feat(at, wuptf_ref, biastf_ref)
    la_b = _log_decay_feat(at, wuptb_ref, biastb_ref)
    lo = lo_ref[...]
    up = up_ref[...]
    w_f = jnp.exp(_tri_right(la_f, lo) - la_f)
    w_b = jnp.exp(_tri_right(la_b, up) - la_b)
    kf = (kt * w_f).astype(BF16)
    kb = (kt * w_b).astype(BF16)
    for hd in range(GLA_HEADS):
        r0, c0 = hd * GLA_DK, hd * GLA_DV
        sf_ref[0, hd] = _dot(kf[r0:r0 + GLA_DK], v[:, c0:c0 + GLA_DV])
        sb_ref[0, hd] = _dot(kb[r0:r0 + GLA_DK], v[:, c0:c0 + GLA_DV])


def ctx_call(ctx, norm_w, csc, csh, w_kt, w_v, w_at, wupt_f, biast_f, wupt_b, biast_b):
    B, Lc, D = ctx.shape
    lo, up = _tri_mats(Lc, Lc)
    const2 = lambda b: (0, 0)
    const3 = lambda b: (0, 0, 0)
    st = jax.ShapeDtypeStruct((B, GLA_HEADS, GLA_DK, GLA_DV), F32)
    st_spec = pl.BlockSpec((1, GLA_HEADS, GLA_DK, GLA_DV), lambda b: (b, 0, 0, 0))
    return pl.pallas_call(
        _ctx_kernel,
        grid=(B,),
        in_specs=[pl.BlockSpec((1, Lc, D), lambda b: (b, 0, 0)),
                  pl.BlockSpec((1, D), const2),
                  pl.BlockSpec((1, 1, D), const3),
                  pl.BlockSpec((1, 1, D), const3),
                  pl.BlockSpec((GLA_QK, D), const2),
                  pl.BlockSpec((D, GLA_VW), const2),
                  pl.BlockSpec((LANES, D), const2),
                  pl.BlockSpec((GLA_QK, LANES), const2),
                  pl.BlockSpec((GLA_QK, 1), const2),
                  pl.BlockSpec((GLA_QK, LANES), const2),
                  pl.BlockSpec((GLA_QK, 1), const2),
                  pl.BlockSpec((Lc, Lc), const2),
                  pl.BlockSpec((Lc, Lc), const2)],
        out_specs=(st_spec, st_spec),
        out_shape=(st, st),
        compiler_params=_cp(("arbitrary",)),
        name="ctx",
    )(ctx, norm_w.reshape(1, D), csc, csh, w_kt, w_v, w_at, wupt_f, biast_f, wupt_b, biast_b, lo, up)


def _gla_kernel(qvr_ref, kt_ref, a_ref, at_ref, sf_ref, sb_ref,
                wupf_ref, biasf_ref, wupb_ref, biasb_ref,
                wuptf_ref, biastf_ref, wuptb_ref, biastb_ref,
                lo_ref, up_ref, nw_ref, o_ref,
                laf_ref, lab_ref, latf_ref, latb_ref, accf_ref, accb_ref, stf_ref, stb_ref, *, L, rb):
    n_pairs = L // PAIR
    C = GLA_CHUNK
    H, DK, DV = GLA_HEADS, GLA_DK, GLA_DV
    scale = DK ** -0.5

    for t0 in range(0, L, rb):
        a = a_ref[0, t0:t0 + rb, :]
        laf_ref[t0:t0 + rb, :] = _log_decay_tok(a, wupf_ref, biasf_ref)
        lab_ref[t0:t0 + rb, :] = _log_decay_tok(a, wupb_ref, biasb_ref)
    for p in range(n_pairs):
        at = at_ref[0, :, p * PAIR:(p + 1) * PAIR]
        latf_ref[p] = _log_decay_feat(at, wuptf_ref, biastf_ref)
        latb_ref[p] = _log_decay_feat(at, wuptb_ref, biastb_ref)

    lo = lo_ref[...]
    up = up_ref[...]
    rr = lax.broadcasted_iota(jnp.int32, (C, C), 0)
    cc = lax.broadcasted_iota(jnp.int32, (C, C), 1)

    def half_pair(p, la_ref, lat_ref, st_ref, acc_ref, forward):
        m_tok = lo if forward else up
        m_feat = up if forward else lo
        keep = (rr >= cc) if forward else (rr <= cc)
        last = C - 1 if forward else 0
        t0 = pl.multiple_of(p * PAIR, PAIR)
        b_tok = _tri_left(m_tok, la_ref[pl.ds(t0, PAIR), :])
        b_feat = _tri_right(lat_ref[p], m_feat)
        q = qvr_ref[0, pl.ds(t0, PAIR), 0:GLA_QK].astype(F32)
        qe = (q * scale * jnp.exp(b_tok)).astype(BF16)
        kt = kt_ref[0, p].astype(F32)
        ket = (kt * jnp.exp(-b_feat)).astype(BF16)
        v = qvr_ref[0, pl.ds(t0, PAIR), GLA_QK:GLA_QK + GLA_VW]
        for half in ((0, 1) if forward else (1, 0)):
            r0 = half * C
            b_last = b_feat[:, r0 + last:r0 + last + 1]
            kdt = (kt[:, r0:r0 + C] * jnp.exp(b_last - b_feat[:, r0:r0 + C])).astype(BF16)
            g = jnp.exp(b_last)
            for hd in range(H):
                k0, v0 = hd * DK, hd * DV
                qe_h = qe[r0:r0 + C, k0:k0 + DK]
                v_h = v[r0:r0 + C, v0:v0 + DV]
                s_prev = st_ref[hd]
                sc = _dot(qe_h, ket[k0:k0 + DK, r0:r0 + C])
                sc = jnp.where(keep, sc, 0.0).astype(BF16)
                acc_ref[pl.ds(t0 + r0, C), v0:v0 + DV] = _dot(sc, v_h) + _dot(qe_h, s_prev.astype(BF16))
                st_ref[hd] = g[k0:k0 + DK] * s_prev + _dot(kdt[k0:k0 + DK], v_h)

    for hd in range(H):
        stf_ref[hd] = sf_ref[0, hd]
        stb_ref[hd] = sb_ref[0, hd]

    def pair_body(i, carry):
        half_pair(i, laf_ref, latf_ref, stf_ref, accf_ref, True)
        half_pair(n_pairs - 1 - i, lab_ref, latb_ref, stb_ref, accb_ref, False)
        return carry

    lax.fori_loop(0, n_pairs, pair_body, 0, unroll=2)

    for t0 in range(0, L, rb):
        r = qvr_ref[0, t0:t0 + rb, GLA_QK + GLA_VW:GLA_QK + 2 * GLA_VW].astype(F32)
        for hd in range(H):
            v0 = hd * DV
            o = _rms_scale(accf_ref[t0:t0 + rb, v0:v0 + DV] + accb_ref[t0:t0 + rb, v0:v0 + DV])
            y = o * nw_ref[:, v0:v0 + DV] * _silu(r[:, v0:v0 + DV])
            o_ref[0, t0:t0 + rb, v0:v0 + DV] = y.astype(o_ref.dtype)


def gla_call(qvr, kt, a, at, s_f, s_b, wup_f, bias_f, wup_b, bias_b,
             wupt_f, biast_f, wupt_b, biast_b, norm_w, *, rb=256):
    B, L, _ = qvr.shape
    lo, up = _tri_mats(PAIR, GLA_CHUNK)
    const2 = lambda b: (0, 0)
    st_spec = pl.BlockSpec((1, GLA_HEADS, GLA_DK, GLA_DV), lambda b: (b, 0, 0, 0))
    return pl.pallas_call(
        functools.partial(_gla_kernel, L=L, rb=rb),
        grid=(B,),
        in_specs=[pl.BlockSpec((1, L, GLA_QK + 2 * GLA_VW), lambda b: (b, 0, 0)),
                  pl.BlockSpec((1, L // PAIR, GLA_QK, PAIR), lambda b: (b, 0, 0, 0)),
                  pl.BlockSpec((1, L, LANES), lambda b: (b, 0, 0)),
                  pl.BlockSpec((1, LANES, L), lambda b: (b, 0, 0)),
                  st_spec, st_spec,
                  pl.BlockSpec((LANES, GLA_QK), const2),
                  pl.BlockSpec((1, GLA_QK), const2),
                  pl.BlockSpec((LANES, GLA_QK), const2),
                  pl.BlockSpec((1, GLA_QK), const2),
                  pl.BlockSpec((GLA_QK, LANES), const2),
                  pl.BlockSpec((GLA_QK, 1), const2),
                  pl.BlockSpec((GLA_QK, LANES), const2),
                  pl.BlockSpec((GLA_QK, 1), const2),
                  pl.BlockSpec((PAIR, PAIR), const2),
                  pl.BlockSpec((PAIR, PAIR), const2),
                  pl.BlockSpec((1, GLA_VW), const2)],
        out_specs=pl.BlockSpec((1, L, GLA_VW), lambda b: (b, 0, 0)),
        out_shape=jax.ShapeDtypeStruct((B, L, GLA_VW), BF16),
        scratch_shapes=[pltpu.VMEM((L, GLA_QK), F32),
                        pltpu.VMEM((L, GLA_QK), F32),
                        pltpu.VMEM((L // PAIR, GLA_QK, PAIR), F32),
                        pltpu.VMEM((L // PAIR, GLA_QK, PAIR), F32),
                        pltpu.VMEM((L, GLA_VW), F32),
                        pltpu.VMEM((L, GLA_VW), F32),
                        pltpu.VMEM((GLA_HEADS, GLA_DK, GLA_DV), F32),
                        pltpu.VMEM((GLA_HEADS, GLA_DK, GLA_DV), F32)],
        compiler_params=_cp(("arbitrary",)),
        name="gla",
    )(qvr, kt, a, at, s_f, s_b, wup_f, bias_f, wup_b, bias_b,
      wupt_f, biast_f, wupt_b, biast_b, lo, up, norm_w.reshape(1, GLA_VW))


def _merge_kernel(x_ref, yhy_ref, ygla_ref, g1_ref, nw1_ref, sc1_ref, sh1_ref, wgate_ref, whp_ref, wgp_ref, wo_ref,
                  nw_ref, sc_ref, sh_ref, xn_ref, h2_ref):
    D = D_MODEL
    h = _modulated_norm(x_ref, nw1_ref, sc1_ref, sh1_ref)
    merged = _sigmoid(_dot(h, wgate_ref[:, 0:D])) * _dot(yhy_ref[0], whp_ref[...])
    merged = merged + _sigmoid(_dot(h, wgate_ref[:, D:2 * D])) * _dot(ygla_ref[0], wgp_ref[...])
    mix = _dot(merged.astype(BF16), wo_ref[...])
    xn = x_ref[0] + g1_ref[0] * mix
    xn_ref[0] = xn
    y = _rms_scale(xn) * nw_ref[...]
    h2_ref[0] = (y * (1.0 + sc_ref[0]) + sh_ref[0]).astype(h2_ref.dtype)


def merge_call(x, y_hy, y_gla, g1, norm1_w, sc1, sh1, w_gates, w_hy_proj, w_gla_proj, w_out, norm2_w, sc2, sh2, *, tm=512):
    B, L, D = x.shape
    tok = lambda b, i: (b, i, 0)
    bvec = lambda b, i: (b, 0, 0)
    const = lambda b, i: (0, 0)
    return pl.pallas_call(
        _merge_kernel,
        grid=(B, L // tm),
        in_specs=[pl.BlockSpec((1, tm, D), tok),
                  pl.BlockSpec((1, tm, HY_WIDTH), tok),
                  pl.BlockSpec((1, tm, GLA_VW), tok),
                  pl.BlockSpec((1, 1, D), bvec),
                  pl.BlockSpec((1, D), const),
                  pl.BlockSpec((1, 1, D), bvec),
                  pl.BlockSpec((1, 1, D), bvec),
                  pl.BlockSpec((D, 2 * D), const, pipeline_mode=pl.Buffered(1)),
                  pl.BlockSpec((HY_WIDTH, D), const),
                  pl.BlockSpec((GLA_VW, D), const),
                  pl.BlockSpec((D, D), const),
                  pl.BlockSpec((1, D), const),
                  pl.BlockSpec((1, 1, D), bvec),
                  pl.BlockSpec((1, 1, D), bvec)],
        out_specs=(pl.BlockSpec((1, tm, D), tok), pl.BlockSpec((1, tm, D), tok)),
        out_shape=(jax.ShapeDtypeStruct((B, L, D), F32), jax.ShapeDtypeStruct((B, L, D), BF16)),
        compiler_params=_cp(("arbitrary", "arbitrary")),
        name="merge",
    )(x, y_hy, y_gla, g1, norm1_w.reshape(1, D), sc1, sh1, w_gates, w_hy_proj, w_gla_proj, w_out,
      norm2_w.reshape(1, D), sc2, sh2)


FFN_PAD = GRID_W


def _ffn_act_kernel(h_ref, wg0_ref, wgn_ref, wup_ref, dw_ref, dwb_ref, o_ref, g0_ref, g1_ref, s_ref, *, L):
    j = pl.program_id(1)
    hc = wup_ref.shape[1]
    mb = 512
    W = GRID_W
    sub = lax.broadcasted_iota(jnp.int32, (8, hc), 0)

    def project(gate_ref, w_ref, t0):
        gate_ref[FFN_PAD + t0:FFN_PAD + t0 + mb, :] = _dot(h_ref[0, t0:t0 + mb, :], w_ref[...])

    @pl.when(j == 0)
    def _():
        zeros = jnp.zeros((FFN_PAD, hc), F32)
        for ref in (g0_ref, g1_ref):
            ref[0:FFN_PAD, :] = zeros
            ref[FFN_PAD + L:FFN_PAD + L + FFN_PAD, :] = zeros
        for t0 in range(0, L, mb):
            project(g0_ref, wg0_ref, t0)

    def step(gate_ref, nxt_ref):
        dw = dw_ref[...]
        for t0 in range(0, L, mb):
            project(nxt_ref, wgn_ref, t0)
            s_ref[t0:t0 + mb, :] = _dot(h_ref[0, t0:t0 + mb, :], wup_ref[...])
            for r0 in range(t0, t0 + mb, W):
                above = gate_ref[FFN_PAD + r0 - W:FFN_PAD + r0, :]
                centre = gate_ref[FFN_PAD + r0:FFN_PAD + r0 + W, :]
                below = gate_ref[FFN_PAD + r0 + W:FFN_PAD + r0 + 2 * W, :]
                col = [above * dw[dx:dx + 1] + centre * dw[3 + dx:4 + dx] + below * dw[6 + dx:7 + dx]
                       for dx in range(3)]
                left = pltpu.roll(col[0], 1, axis=0)
                right = pltpu.roll(col[2], W - 1, axis=0)
                left = jnp.concatenate([jnp.where(sub == 0, 0.0, left[0:8]), left[8:]], axis=0)
                right = jnp.concatenate([right[:W - 8], jnp.where(sub == 7, 0.0, right[W - 8:])], axis=0)
                conv = col[1] + left + right + dwb_ref[...]
                o_ref[0, r0:r0 + W, :] = (_silu(conv) * s_ref[r0:r0 + W, :]).astype(o_ref.dtype)

    parity = lax.rem(j, 2)
    pl.when(parity == 0)(lambda: step(g0_ref, g1_ref))
    pl.when(parity == 1)(lambda: step(g1_ref, g0_ref))


def ffn_act_call(h2, w_gate, w_up, dw, dw_bias, *, hc=256):
    B, L, D = h2.shape
    F = w_up.shape[1]
    n_chunks = F // hc
    return pl.pallas_call(
        functools.partial(_ffn_act_kernel, L=L),
        grid=(B, n_chunks),
        in_specs=[pl.BlockSpec((1, L, D), lambda b, j: (b, 0, 0)),
                  pl.BlockSpec((D, hc), lambda b, j: (0, 0)),
                  pl.BlockSpec((D, hc), lambda b, j: (0, jnp.minimum(j + 1, n_chunks - 1))),
                  pl.BlockSpec((D, hc), lambda b, j: (0, j)),
                  pl.BlockSpec((9, hc), lambda b, j: (0, j)),
                  pl.BlockSpec((1, hc), lambda b, j: (0, j))],
        out_specs=pl.BlockSpec((1, L, hc), lambda b, j: (b, 0, j)),
        out_shape=jax.ShapeDtypeStruct((B, L, F), BF16),
        scratch_shapes=[pltpu.VMEM((L + 2 * FFN_PAD, hc), F32)] * 2 + [pltpu.VMEM((L, hc), F32)],
        compiler_params=_cp(("arbitrary", "arbitrary")),
        name="ffn_act",
    )(h2, w_gate, w_gate, w_up, dw, dw_bias.reshape(1, F))


def _ffn_out_kernel(act_ref, xn_ref, g2_ref, wout_ref, fw_ref, o_ref):
    xo = xn_ref[0] + g2_ref[0] * _dot(act_ref[0], wout_ref[...])
    o_ref[0] = _rms_scale(xo) * fw_ref[...]


def ffn_out_call(act, x_new, g2, w_out, final_w, *, tm=1024):
    B, L, F = act.shape
    D = w_out.shape[1]
    tok = lambda b, i: (b, i, 0)
    const = lambda b, i: (0, 0)
    return pl.pallas_call(
        _ffn_out_kernel,
        grid=(B, L // tm),
        in_specs=[pl.BlockSpec((1, tm, F), tok),
                  pl.BlockSpec((1, tm, D), tok),
                  pl.BlockSpec((1, 1, D), lambda b, i: (b, 0, 0)),
                  pl.BlockSpec((F, D), const, pipeline_mode=pl.Buffered(1)),
                  pl.BlockSpec((1, D), const)],
        out_specs=pl.BlockSpec((1, tm, D), tok),
        out_shape=jax.ShapeDtypeStruct((B, L, D), F32),
        compiler_params=_cp(("arbitrary", "arbitrary")),
        name="ffn_out",
    )(act, x_new, g2, w_out, final_w.reshape(1, D))


def _pad_rows(w, rows, at=0):
    out = jnp.zeros((rows, w.shape[1]), w.dtype)
    return lax.dynamic_update_slice(out, w, (at, 0))


def kernel(x, c, ctx, c_ctx, w_ada, b_ada, norm1_w, norm2_w, w_in, hy_conv_w, hy_filt_w1, hy_filt_b1,
           hy_filt_w2, hy_filt_b2, hy_filt_w3, hy_filt_freq, hy_decay, hy_bias, gla_a_up_f, gla_a_bias_f,
           gla_a_up_b, gla_a_bias_b, gla_norm_w, w_hy_proj, w_gla_proj, w_out, ffn_w_in, ffn_dw,
           ffn_dw_bias, ffn_w_out, final_norm_w):
    assert w_ada.shape[0] == 1, "single-layer block"
    B, L, D = x.shape
    R = GLA_LOWRANK

    n_rows = -(-(B + 1) // 8) * 8
    c_rows = jnp.zeros((n_rows, D), F32).at[:B].set(c).at[B].set(c_ctx)
    mod = ada_call(c_rows, w_ada[0], b_ada[0])
    sh1, sc1, g1, sh2, sc2, g2 = [m[:B, None, :] for m in jnp.split(mod, 6, axis=-1)]
    csh1, csc1 = mod[B:B + 1, None, 0:D], mod[B:B + 1, None, D:2 * D]

    w = w_in[0]
    o_q = HY_COLS
    o_k = o_q + GLA_QK
    o_v = o_k + GLA_QK
    o_r = o_v + GLA_VW
    o_af = o_r + GLA_VW
    o_ab = o_af + R
    o_g = o_ab + R
    w_main = jnp.concatenate([w[:, :o_k], w[:, o_v:o_af]], axis=1).astype(BF16)
    w_gates = w[:, o_g:].astype(BF16)
    w_kt = w[:, o_k:o_v].T.astype(BF16)
    w_v = w[:, o_v:o_r].astype(BF16)
    w_a = jnp.pad(w[:, o_af:o_g], ((0, 0), (0, LANES - 2 * R))).astype(BF16)
    w_at = w_a.T
    wup_f = _pad_rows(gla_a_up_f[0], LANES, 0)
    wup_b = _pad_rows(gla_a_up_b[0], LANES, R)
    wupt_f, wupt_b = wup_f.T, wup_b.T
    bias_f, bias_b = gla_a_bias_f[0].reshape(1, GLA_QK), gla_a_bias_b[0].reshape(1, GLA_QK)
    biast_f, biast_b = bias_f.reshape(GLA_QK, 1), bias_b.reshape(GLA_QK, 1)

    cos32, sin32 = dft_tables(HY_BLOCK)
    a_co, b_co, g_nyq = filter_call(L, hy_filt_w1[0], hy_filt_b1[0], hy_filt_w2[0], hy_filt_b2[0],
                                    hy_filt_w3[0], hy_filt_freq[0], hy_decay[0], cos32, sin32)
    fwd_tab = jnp.concatenate([cos32, sin32], axis=0).astype(BF16)
    inv_tab = jnp.concatenate([cos32, sin32], axis=1).astype(BF16)

    s_f, s_b = ctx_call(ctx, norm1_w[0], csc1, csh1, w_kt, w_v, w_at, wupt_f, biast_f, wupt_b, biast_b)

    u_hy, qvr, kt, a, at = inproj_call(x, norm1_w[0], sc1, sh1, w_main, w_kt, w_a, w_at)
    y_hy = hyena_call(u_hy, hy_conv_w[0], hy_bias[0], fwd_tab, inv_tab, a_co, b_co, g_nyq)
    y_gla = gla_call(qvr, kt, a, at, s_f, s_b, wup_f, bias_f, wup_b, bias_b,
                     wupt_f, biast_f, wupt_b, biast_b, gla_norm_w[0])
    x_new, h2 = merge_call(x, y_hy, y_gla, g1, norm1_w[0], sc1, sh1, w_gates, w_hy_proj[0].astype(BF16), w_gla_proj[0].astype(BF16),
                           w_out[0].astype(BF16), norm2_w[0], sc2, sh2)
    w_ffn = ffn_w_in[0]
    act = ffn_act_call(h2, w_ffn[:, FFN_HIDDEN:].astype(BF16), w_ffn[:, :FFN_HIDDEN].astype(BF16),
                       ffn_dw[0].reshape(9, FFN_HIDDEN), ffn_dw_bias[0])
    return ffn_out_call(act, x_new, g2, ffn_w_out[0].astype(BF16), final_norm_w)
```
